```python
import jax
import jax.numpy as jnp
from jax import lax
import numpy as np


D_MODEL = 2048
BATCH = 1
SEQ = 16384
DEPTH = 2

HEAD_DIM = 128
ROT_DIM = HEAD_DIM // 4
ROPE_THETA = 500000.0
N_BRANCH = 4
NORM_EPS = 1e-6

GLA_HEADS = 4
GLA_DK = 64
GLA_DV = 128
GLA_RANK = 16
GLA_NORMALIZER = 16.0
GLA_CHUNK = 64

DIL_PAIRS = ((128, 1), (512, 4), (2048, 16))
DIL_HEADS_PER_GROUP = 2
DIL_HEADS = DIL_HEADS_PER_GROUP * len(DIL_PAIRS)

RWKV_HEAD_SIZE = 64
RWKV_HEADS = 8
RWKV_WIDTH = RWKV_HEADS * RWKV_HEAD_SIZE
RWKV_DECAY_LORA = 96
RWKV_AAA_LORA = 96
RWKV_MV_LORA = 64
RWKV_GATE_LORA = 256
RWKV_LNX_EPS = 64e-5

MOBA_HEADS = 4
MOBA_BLOCK = 256
MOBA_TOPK = 3
MOBA_QCHUNK = 128

FFN_HIDDEN = -(-8 * D_MODEL // (3 * 256)) * 256

GLA_SIZES = (GLA_HEADS * GLA_DK, GLA_HEADS * GLA_DK, GLA_HEADS * GLA_DV, GLA_HEADS * GLA_DV, GLA_RANK)
RWKV_SIZES = (RWKV_WIDTH, RWKV_WIDTH, RWKV_WIDTH, RWKV_DECAY_LORA, RWKV_AAA_LORA, RWKV_GATE_LORA)
GLA_IN = sum(GLA_SIZES)
DIL_IN = 3 * DIL_HEADS * HEAD_DIM
RWKV_IN = sum(RWKV_SIZES)
MOBA_IN = 3 * MOBA_HEADS * HEAD_DIM
IN_SIZES = (N_BRANCH * D_MODEL, GLA_IN, DIL_IN, RWKV_IN, MOBA_IN)
IN_TOTAL = sum(IN_SIZES)
GLA_OUT = GLA_HEADS * GLA_DV
DIL_OUT = DIL_HEADS_PER_GROUP * HEAD_DIM
RWKV_OUT = RWKV_WIDTH
MOBA_OUT = MOBA_HEADS * HEAD_DIM

kernel_name = 'hybrid_gated_parallel_mixers_adaln'


def split_sizes(t, sizes):
    offsets = [int(o) for o in np.cumsum(sizes)[:-1]]
    return jnp.split(t, offsets, axis=-1)


def rms_norm(x, gain):
    xf = x.astype(jnp.float32)
    y = xf * lax.rsqrt(jnp.mean(xf * xf, axis=-1, keepdims=True) + NORM_EPS)
    return (y * gain.astype(jnp.float32)).astype(x.dtype)


def to_heads(t, n_heads):
    B, S, _ = t.shape
    return t.reshape(B, S, n_heads, -1).transpose(0, 2, 1, 3)


def rope_tables(positions):
    inv_freq = ROPE_THETA ** (-jnp.arange(0, ROT_DIM, 2, dtype=jnp.float32) / ROT_DIM)
    ang = positions.astype(jnp.float32)[..., None] * inv_freq
    return jnp.cos(ang)[:, None], jnp.sin(ang)[:, None]


def apply_rope(x, cos, sin):
    x_rot, x_pass = x[..., :ROT_DIM], x[..., ROT_DIM:]
    x1, x2 = jnp.split(x_rot, 2, axis=-1)
    rotated = jnp.concatenate([x1 * cos - x2 * sin, x2 * cos + x1 * sin], axis=-1)
    return jnp.concatenate([rotated.astype(x.dtype), x_pass], axis=-1)


def gla_mixer(p, w_a2, b_a2, g_norm):
    dtype = p.dtype
    B, S, _ = p.shape
    q, k, v, g, a_low = split_sizes(p.astype(jnp.float32), GLA_SIZES)
    gk = jax.nn.log_sigmoid(a_low @ w_a2.astype(jnp.float32) + b_a2.astype(jnp.float32)) / GLA_NORMALIZER
    n = S // GLA_CHUNK

    def chunked(t):
        return t.reshape(B, n, GLA_CHUNK, GLA_HEADS, -1).transpose(1, 0, 3, 2, 4)

    causal = jnp.tril(jnp.ones((GLA_CHUNK, GLA_CHUNK), dtype=bool))[:, :, None]

    def step(state, inp):
        q_c, k_c, v_c, g_c = inp
        b = jnp.cumsum(g_c, axis=-2)
        b_last = b[:, :, -1:, :]
        rel = jnp.exp(jnp.where(causal, b[:, :, :, None, :] - b[:, :, None, :, :], -jnp.inf))
        scores = jnp.einsum('bhtd,bhsd,bhtsd->bhts', q_c, k_c, rel)
        out = (jnp.einsum('bhtd,bhde->bhte', q_c * jnp.exp(b), state)
               + jnp.einsum('bhts,bhse->bhte', scores, v_c))
        state = (jnp.exp(b_last)[:, :, 0, :, None] * state
                 + jnp.einsum('bhsd,bhse->bhde', k_c * jnp.exp(b_last - b), v_c))
        return state, out

    state0 = jnp.zeros((B, GLA_HEADS, GLA_DK, GLA_DV), jnp.float32)
    _, o = lax.scan(step, state0, (chunked(q * GLA_DK ** -0.5), chunked(k), chunked(v), chunked(gk)))
    o = o.transpose(1, 0, 3, 2, 4).reshape(B, S, GLA_HEADS, GLA_DV)
    o = o * lax.rsqrt(jnp.mean(o * o, axis=-1, keepdims=True) + NORM_EPS) * g_norm.astype(jnp.float32)
    return (o.reshape(B, S, GLA_OUT) * jax.nn.silu(g)).astype(dtype)


def banded_attention(q, k, v, window):
    N, L, dh = q.shape
    blk = window
    nb = -(-L // blk)
    pad = nb * blk - L

    def blocks(t):
        return jnp.pad(t, ((0, 0), (0, pad), (0, 0))).reshape(N, nb, blk, dh)

    def with_prev(t):
        return jnp.concatenate([jnp.pad(t[:, :-1], ((0, 0), (1, 0), (0, 0), (0, 0))), t], axis=2)

    qb = blocks(q)
    k2, v2 = with_prev(blocks(k)), with_prev(blocks(v))
    qi = jnp.arange(blk)[:, None]
    ki = jnp.arange(2 * blk)[None, :]
    dist = blk + qi - ki
    band = (dist >= 0) & (dist <= window)
    mask = band[None] & ((jnp.arange(nb)[:, None, None] > 0) | (ki >= blk)[None])
    s = jnp.einsum('nbqd,nbkd->nbqk', qb, k2).astype(jnp.float32) * dh ** -0.5
    s = jnp.where(mask, s, -jnp.inf)
    m = jnp.max(s, axis=-1, keepdims=True)
    pr = jnp.exp(s - m)
    den = jnp.sum(pr, axis=-1, keepdims=True)
    o = jnp.einsum('nbqk,nbkd->nbqd', pr, v2.astype(jnp.float32)) / den
    lse = (m + jnp.log(den))[..., 0]
    return o.reshape(N, nb * blk, dh)[:, :L], lse.reshape(N, nb * blk)[:, :L]


def fold_dilated(t, dil):
    B, h, S, d = t.shape
    return t.reshape(B, h, S // dil, dil, d).transpose(0, 1, 3, 2, 4).reshape(B * h * dil, S // dil, d)


def dilated_mixer(p, cos, sin):
    dtype = p.dtype
    B, S, _ = p.shape
    q, k, v = [to_heads(t, DIL_HEADS) for t in jnp.split(p.astype(jnp.float32), 3, axis=-1)]
    q, k = apply_rope(q, cos, sin), apply_rope(k, cos, sin)
    hp = DIL_HEADS_PER_GROUP
    outs, lses = [], []
    for g, (window, dil) in enumerate(DIL_PAIRS):
        hs = slice(g * hp, (g + 1) * hp)
        o, lse = banded_attention(fold_dilated(q[:, hs], dil), fold_dilated(k[:, hs], dil),
                                  fold_dilated(v[:, hs], dil), window // dil)
        outs.append(o.reshape(B, hp, dil, S // dil, HEAD_DIM).transpose(0, 1, 3, 2, 4).reshape(B, hp, S, HEAD_DIM))
        lses.append(lse.reshape(B, hp, dil, S // dil).transpose(0, 1, 3, 2).reshape(B, hp, S))
    alpha = jax.nn.softmax(jnp.stack(lses, axis=0), axis=0)
    o = jnp.sum(alpha[..., None] * jnp.stack(outs, axis=0), axis=0)
    return o.transpose(0, 2, 1, 3).reshape(B, S, DIL_OUT).astype(dtype)


def rwkv_heads(t):
    return t.reshape(t.shape[:-1] + (RWKV_HEADS, RWKV_HEAD_SIZE))


def rwkv7_mixer(p, mu, w0, w2, a0, a2, g2, k_k, k_a, r_k, lnx_w, lnx_b, v_first, v_mix):
    dtype = p.dtype
    B, S, _ = p.shape
    p = p.astype(jnp.float32)
    p_prev = jnp.pad(p[:, :-1], ((0, 0), (1, 0), (0, 0)))
    xs = p + (p_prev - p) * mu.astype(jnp.float32)
    r, k, v, w_low, a_low, g_low = split_sizes(xs, RWKV_SIZES)
    log_w = -jax.nn.softplus(-(w0 + jnp.tanh(w_low) @ w2)) - 0.5
    decay = jnp.exp(-jnp.exp(log_w))
    a = jax.nn.sigmoid(a0 + a_low @ a2)
    g = jax.nn.sigmoid(g_low) @ g2
    if v_mix is None:
        v_first = v
    else:
        v0, v1, v2 = v_mix
        v = v + (v_first - v) * jax.nn.sigmoid(v0 + (v @ v1) @ v2)
    kk = rwkv_heads(k * k_k)
    kk = kk / jnp.maximum(jnp.sqrt(jnp.sum(kk * kk, axis=-1, keepdims=True)), 1e-12)
    k = k * (1.0 + (a - 1.0) * k_a)
    rh, kh, vh, wh, ah = rwkv_heads(r), rwkv_heads(k), rwkv_heads(v), rwkv_heads(decay), rwkv_heads(a)

    def step(state, inp):
        r_t, w_t, k_t, v_t, a_t, b_t = inp
        sa = jnp.einsum('bhij,bhj->bhi', state, a_t)
        state = (state * w_t[..., None, :] + sa[..., :, None] * b_t[..., None, :]
                 + v_t[..., :, None] * k_t[..., None, :])
        return state, jnp.einsum('bhij,bhj->bhi', state, r_t)

    def seq_first(t):
        return t.transpose(1, 0, 2, 3)

    state0 = jnp.zeros((B, RWKV_HEADS, RWKV_HEAD_SIZE, RWKV_HEAD_SIZE), jnp.float32)
    _, y = lax.scan(step, state0, (seq_first(rh), seq_first(wh), seq_first(kh), seq_first(vh),
                                   seq_first(-kk), seq_first(kk * ah)))
    y = y.transpose(1, 0, 2, 3)
    mean = jnp.mean(y, axis=-1, keepdims=True)
    var = jnp.mean(jnp.square(y - mean), axis=-1, keepdims=True)
    y = ((y - mean) * lax.rsqrt(var + RWKV_LNX_EPS)).reshape(B, S, RWKV_WIDTH) * lnx_w + lnx_b
    bonus = jnp.sum(rh * kh * rwkv_heads(r_k), axis=-1, keepdims=True) * vh
    y = y + bonus.reshape(B, S, RWKV_WIDTH)
    return (y * g).astype(dtype), v_first


def moba_mixer(p, cos, sin):
    dtype = p.dtype
    B, S, _ = p.shape
    H, blk, qc_len = MOBA_HEADS, MOBA_BLOCK, MOBA_QCHUNK
    q, k, v = [to_heads(t, H) for t in jnp.split(p.astype(jnp.float32), 3, axis=-1)]
    q = apply_rope(q, cos, sin) * HEAD_DIM ** -0.5
    k = apply_rope(k, cos, sin)
    nb = -(-S // blk)
    pad = nb * blk - S
    kb = jnp.pad(k, ((0, 0), (0, 0), (0, pad), (0, 0))).reshape(B, H, nb, blk, HEAD_DIM)
    vb = jnp.pad(v, ((0, 0), (0, 0), (0, pad), (0, 0))).reshape(B, H, nb, blk, HEAD_DIM)
    k_mean = jnp.mean(kb, axis=3)
    top_k = min(MOBA_TOPK, nb)
    nq = S // qc_len
    q_chunks = q.reshape(B, H, nq, qc_len, HEAD_DIM).transpose(2, 0, 1, 3, 4)
    b_idx = jnp.arange(B)[:, None, None, None]
    h_idx = jnp.arange(H)[None, :, None, None]

    def chunk_attend(args):
        q_i, ci = args
        q_pos = ci * qc_len + jnp.arange(qc_len)
        cur = (ci * qc_len) // blk
        gate = jnp.einsum('bhqd,bhnd->bhqn', q_i, k_mean)
        gate = jnp.where(jnp.arange(nb) < cur, gate, -jnp.inf)
        sel_score, sel = lax.top_k(gate, top_k)
        sel_ok = jnp.isfinite(sel_score)
        k_sel = kb[b_idx, h_idx, sel]
        v_sel = vb[b_idx, h_idx, sel]
        s_sel = jnp.einsum('bhqd,bhqkld->bhqkl', q_i, k_sel)
        s_sel = jnp.where(sel_ok[..., None], s_sel, -jnp.inf).reshape(B, H, qc_len, top_k * blk)
        k_own = lax.dynamic_index_in_dim(kb, cur, axis=2, keepdims=False)
        v_own = lax.dynamic_index_in_dim(vb, cur, axis=2, keepdims=False)
        s_own = jnp.einsum('bhqd,bhld->bhql', q_i, k_own)
        own_pos = cur * blk + jnp.arange(blk)
        s_own = jnp.where(own_pos[None, :] <= q_pos[:, None], s_own, -jnp.inf)
        probs = jax.nn.softmax(jnp.concatenate([s_sel, s_own], axis=-1), axis=-1)
        p_sel = probs[..., :top_k * blk].reshape(B, H, qc_len, top_k, blk)
        p_own = probs[..., top_k * blk:]
        return (jnp.einsum('bhqkl,bhqkld->bhqd', p_sel, v_sel)
                + jnp.einsum('bhql,bhld->bhqd', p_own, v_own))

    o = lax.map(chunk_attend, (q_chunks, jnp.arange(nq)))
    return o.transpose(1, 0, 3, 2, 4).reshape(B, S, MOBA_OUT).astype(dtype)


def setup_inputs(seed: int = 0) -> dict:
    key = jax.random.key(seed)
    ks = iter(jax.random.split(key, 40))
    L = DEPTH
    W = RWKV_WIDTH

    def nrm(shape, scale):
        return jax.random.normal(next(ks), shape, jnp.float32) * scale

    return {
        'x': nrm((BATCH, SEQ, D_MODEL), 1.0),
        'c': nrm((BATCH, D_MODEL), 1.0),
        'positions': (jnp.arange(SEQ, dtype=jnp.int32)[None, :]
                      + jax.random.randint(next(ks), (BATCH, 1), 0, 1024, dtype=jnp.int32)),
        'w_ada': nrm((L, D_MODEL, 6 * D_MODEL), 0.5 * D_MODEL ** -0.5),
        'b_ada': nrm((L, 6 * D_MODEL), 0.02),
        'norm1': 1.0 + nrm((L, D_MODEL), 0.02),
        'w_in': nrm((L, D_MODEL, IN_TOTAL), D_MODEL ** -0.5),
        'gla_w_a2': nrm((L, GLA_RANK, GLA_HEADS * GLA_DK), GLA_RANK ** -0.5),
        'gla_b_a2': nrm((L, GLA_HEADS * GLA_DK), 0.1),
        'gla_gnorm': 1.0 + nrm((L, GLA_DV), 0.02),
        'rwkv_mu': jax.random.uniform(next(ks), (L, RWKV_IN), jnp.float32),
        'rwkv_w0': -1.0 + nrm((L, W), 0.5),
        'rwkv_w2': nrm((L, RWKV_DECAY_LORA, W), 0.5 * RWKV_DECAY_LORA ** -0.5),
        'rwkv_a0': nrm((L, W), 0.5),
        'rwkv_a2': nrm((L, RWKV_AAA_LORA, W), RWKV_AAA_LORA ** -0.5),
        'rwkv_g2': nrm((L, RWKV_GATE_LORA, W), RWKV_GATE_LORA ** -0.5),
        'rwkv_k_k': 0.85 + nrm((L, W), 0.05),
        'rwkv_k_a': 1.0 + nrm((L, W), 0.05),
        'rwkv_r_k': nrm((L, W), 0.1),
        'rwkv_lnx_w': 1.0 + nrm((L, W), 0.02),
        'rwkv_lnx_b': nrm((L, W), 0.02),
        'rwkv_v0': nrm((L - 1, W), 0.5),
        'rwkv_v1': nrm((L - 1, W, RWKV_MV_LORA), W ** -0.5),
        'rwkv_v2': nrm((L - 1, RWKV_MV_LORA, W), 0.5 * RWKV_MV_LORA ** -0.5),
        'w_branch_a': nrm((L, GLA_OUT, D_MODEL), GLA_OUT ** -0.5),
        'w_branch_b': nrm((L, DIL_OUT, D_MODEL), DIL_OUT ** -0.5),
        'w_branch_c': nrm((L, RWKV_OUT, D_MODEL), RWKV_OUT ** -0.5),
        'w_branch_d': nrm((L, MOBA_OUT, D_MODEL), MOBA_OUT ** -0.5),
        'w_out': nrm((L, D_MODEL, D_MODEL), D_MODEL ** -0.5),
        'norm2': 1.0 + nrm((L, D_MODEL), 0.02),
        'w_ffn_in': nrm((L, D_MODEL, 2 * FFN_HIDDEN), D_MODEL ** -0.5),
        'w_ffn_out': nrm((L, FFN_HIDDEN, D_MODEL), FFN_HIDDEN ** -0.5),
        'norm_f': 1.0 + nrm((D_MODEL,), 0.02),
    }


def reference(x, c, positions, w_ada, b_ada, norm1, w_in, gla_w_a2, gla_b_a2, gla_gnorm,
              rwkv_mu, rwkv_w0, rwkv_w2, rwkv_a0, rwkv_a2, rwkv_g2, rwkv_k_k, rwkv_k_a, rwkv_r_k,
              rwkv_lnx_w, rwkv_lnx_b, rwkv_v0, rwkv_v1, rwkv_v2,
              w_branch_a, w_branch_b, w_branch_c, w_branch_d, w_out, norm2, w_ffn_in, w_ffn_out, norm_f):
    B, S, _ = x.shape
    cos, sin = rope_tables(positions)
    c_act = jax.nn.silu(c)
    v_first = None
    for l in range(DEPTH):
        mod = c_act @ w_ada[l] + b_ada[l]
        shift1, scale1, gate1, shift2, scale2, gate2 = [m[:, None, :] for m in jnp.split(mod, 6, axis=-1)]

        h = rms_norm(x, norm1[l]) * (1.0 + scale1) + shift1
        proj = h @ w_in[l]
        p_gate, p_gla, p_dil, p_rwkv, p_moba = split_sizes(proj, IN_SIZES)
        gates = jax.nn.sigmoid(p_gate).reshape(B, S, N_BRANCH, D_MODEL)
        o_gla = gla_mixer(p_gla, gla_w_a2[l], gla_b_a2[l], gla_gnorm[l])
        o_dil = dilated_mixer(p_dil, cos, sin)
        v_mix = None if l == 0 else (rwkv_v0[l - 1], rwkv_v1[l - 1], rwkv_v2[l - 1])
        o_rwkv, v_first = rwkv7_mixer(p_rwkv, rwkv_mu[l], rwkv_w0[l], rwkv_w2[l], rwkv_a0[l], rwkv_a2[l],
                                      rwkv_g2[l], rwkv_k_k[l], rwkv_k_a[l], rwkv_r_k[l],
                                      rwkv_lnx_w[l], rwkv_lnx_b[l], v_first, v_mix)
        o_moba = moba_mixer(p_moba, cos, sin)
        merged = (gates[:, :, 0] * (o_gla @ w_branch_a[l])
                  + gates[:, :, 1] * (o_dil @ w_branch_b[l])
                  + gates[:, :, 2] * (o_rwkv @ w_branch_c[l])
                  + gates[:, :, 3] * (o_moba @ w_branch_d[l]))
        x = x + gate1 * (merged @ w_out[l])

        h2 = rms_norm(x, norm2[l]) * (1.0 + scale2) + shift2
        g_ffn, u_ffn = jnp.split(h2 @ w_ffn_in[l], 2, axis=-1)
        x = x + gate2 * ((jax.nn.silu(g_ffn) * u_ffn) @ w_ffn_out[l])
    return rms_norm(x, norm_f)
```

```python
import functools

import numpy as np
import jax
import jax.numpy as jnp
from jax import lax
from jax.experimental import pallas as pl
from jax.experimental.pallas import tpu as pltpu

F32 = jnp.float32
BF16 = jnp.bfloat16
HI = lax.Precision.HIGHEST

D_MODEL = 2048
DEPTH = 2
HEAD_DIM = 128
ROT_DIM = HEAD_DIM // 4
ROPE_THETA = 500000.0
N_BRANCH = 4
NORM_EPS = 1e-6

GLA_HEADS = 4
GLA_DK = 64
GLA_DV = 128
GLA_RANK = 16
GLA_NORMALIZER = 16.0
GLA_SUB = 16

DIL_PAIRS = ((128, 1), (512, 4), (2048, 16))
DIL_HEADS_PER_GROUP = 2
DIL_HEADS = DIL_HEADS_PER_GROUP * len(DIL_PAIRS)
DIL_BLK = 128

RWKV_HEAD_SIZE = 64
RWKV_HEADS = 8
RWKV_WIDTH = RWKV_HEADS * RWKV_HEAD_SIZE
RWKV_DECAY_LORA = 96
RWKV_AAA_LORA = 96
RWKV_MV_LORA = 64
RWKV_GATE_LORA = 256
RWKV_LNX_EPS = 64e-5
RWKV_CHUNK = 64
LORA_PAD = 128

MOBA_HEADS = 4
MOBA_BLOCK = 256
MOBA_TOPK = 3

FFN_HIDDEN = -(-8 * D_MODEL // (3 * 256)) * 256

GLA_SIZES = (GLA_HEADS * GLA_DK, GLA_HEADS * GLA_DK, GLA_HEADS * GLA_DV, GLA_HEADS * GLA_DV, GLA_RANK)
RWKV_SIZES = (RWKV_WIDTH, RWKV_WIDTH, RWKV_WIDTH, RWKV_DECAY_LORA, RWKV_AAA_LORA, RWKV_GATE_LORA)
GLA_IN = sum(GLA_SIZES)
DIL_IN = 3 * DIL_HEADS * HEAD_DIM
RWKV_IN = sum(RWKV_SIZES)
MOBA_IN = 3 * MOBA_HEADS * HEAD_DIM
GLA_OUT = GLA_HEADS * GLA_DV
DIL_OUT = DIL_HEADS_PER_GROUP * HEAD_DIM
RWKV_OUT = RWKV_WIDTH
MOBA_OUT = MOBA_HEADS * HEAD_DIM

GLA_IN_PAD = GLA_IN - GLA_RANK + 128
RWKV_IN_PAD = 3 * RWKV_WIDTH + 2 * LORA_PAD + RWKV_GATE_LORA

VMEM_LIMIT_BYTES = 56 * 1024 * 1024


def _cp(*sem):
    return pltpu.CompilerParams(dimension_semantics=sem, vmem_limit_bytes=VMEM_LIMIT_BYTES)


def _dot(a, b):
    return jnp.dot(a, b, preferred_element_type=F32)


def _dot_hi(a, b):
    return jnp.dot(a, b, precision=HI, preferred_element_type=F32)


def _dot_nt(a, b, precision=None):
    return lax.dot_general(a, b, (((1,), (1,)), ((), ())), precision=precision, preferred_element_type=F32)


def _dot_tn(a, b, precision=None):
    return lax.dot_general(a, b, (((0,), (0,)), ((), ())), precision=precision, preferred_element_type=F32)


def _log_sigmoid(z):
    return jnp.minimum(z, 0.0) - jnp.log1p(jnp.exp(-jnp.abs(z)))


def _sigmoid(z):
    return 1.0 / (1.0 + jnp.exp(-z))


def _rope(x, cos_f, sin_f):
    half = ROT_DIM // 2
    lane = lax.broadcasted_iota(jnp.int32, x.shape, 1)
    partner = jnp.where(lane < half, pltpu.roll(x, HEAD_DIM - half, axis=1), pltpu.roll(x, half, axis=1))
    return x * cos_f + partner * sin_f


def _rope_table_kernel(pos_ref, inv_ref, cos_ref, sin_ref):
    ang = pos_ref[...] * inv_ref[...]
    lane = lax.broadcasted_iota(jnp.int32, ang.shape, 1)
    half = ROT_DIM // 2
    c = jnp.cos(ang)
    s = jnp.sin(ang)
    cos_ref[...] = jnp.where(lane < ROT_DIM, c, 1.0)
    sin_ref[...] = jnp.where(lane < half, -s, jnp.where(lane < ROT_DIM, s, 0.0))


def _rope_tables(positions):
    S = positions.shape[1]
    inv = ROPE_THETA ** (-jnp.arange(0, ROT_DIM, 2, dtype=F32) / ROT_DIM)
    inv_full = jnp.concatenate([inv, inv, jnp.zeros((HEAD_DIM - ROT_DIM,), F32)])[None, :]
    pos = positions.astype(F32).reshape(S, 1)
    tm = min(S, 1024)
    return pl.pallas_call(
        _rope_table_kernel,
        grid=(S // tm,),
        in_specs=[pl.BlockSpec((tm, 1), lambda i: (i, 0)), pl.BlockSpec((1, HEAD_DIM), lambda i: (0, 0))],
        out_specs=[pl.BlockSpec((tm, HEAD_DIM), lambda i: (i, 0))] * 2,
        out_shape=[jax.ShapeDtypeStruct((S, HEAD_DIM), F32)] * 2,
        compiler_params=_cp("parallel"),
        name="rope_tables",
    )(pos, inv_full)


def _ada_kernel(c_ref, w_ref, b_ref, o_ref):
    c = c_ref[...]
    o_ref[...] = _dot_hi(c * _sigmoid(c), w_ref[...]) + b_ref[...]


def _adaln(c8, w_ada, b_ada, layer):
    D = D_MODEL
    tn = 1536
    return pl.pallas_call(
        _ada_kernel,
        grid=(6 * D // tn,),
        in_specs=[pl.BlockSpec((8, D), lambda n: (0, 0)),
                  pl.BlockSpec((None, D, tn), lambda n: (layer, 0, n)),
                  pl.BlockSpec((None, 1, tn), lambda n: (layer, 0, n))],
        out_specs=pl.BlockSpec((8, tn), lambda n: (0, n)),
        out_shape=jax.ShapeDtypeStruct((8, 6 * D), F32),
        compiler_params=_cp("parallel"),
        name="adaln",
    )(c8, w_ada, b_ada.reshape(DEPTH, 1, 6 * D))


def _normmod_kernel(x_ref, g_ref, sc_ref, sh_ref, o_ref):
    x = x_ref[...]
    y = x * lax.rsqrt(jnp.mean(x * x, axis=-1, keepdims=True) + NORM_EPS) * g_ref[...]
    o_ref[...] = (y * (1.0 + sc_ref[...]) + sh_ref[...]).astype(o_ref.dtype)


def _normmod(x, gain, scale, shift, out_dtype):
    S, D = x.shape
    tm = min(S, 512)
    row = pl.BlockSpec((1, D), lambda i: (0, 0))
    return pl.pallas_call(
        _normmod_kernel,
        grid=(S // tm,),
        in_specs=[pl.BlockSpec((tm, D), lambda i: (i, 0)), row, row, row],
        out_specs=pl.BlockSpec((tm, D), lambda i: (i, 0)),
        out_shape=jax.ShapeDtypeStruct((S, D), out_dtype),
        compiler_params=_cp("parallel"),
        name="normmod",
    )(x, gain, scale, shift)


def _mm_kernel(a_ref, b_ref, o_ref):
    o_ref[...] = _dot(a_ref[...], b_ref[...]).astype(o_ref.dtype)


def _mm(a, b, tm, tn, out_dtype, name):
    M, K = a.shape
    N = b.shape[1]
    tm, tn = min(tm, M), min(tn, N)
    return pl.pallas_call(
        _mm_kernel,
        grid=(N // tn, M // tm),
        in_specs=[pl.BlockSpec((tm, K), lambda n, m: (m, 0)), pl.BlockSpec((K, tn), lambda n, m: (0, n))],
        out_specs=pl.BlockSpec((tm, tn), lambda n, m: (m, n)),
        out_shape=jax.ShapeDtypeStruct((M, N), out_dtype),
        compiler_params=_cp("parallel", "parallel"),
        name=name,
    )(a, b)


def _swiglu_kernel(a_ref, bg_ref, bu_ref, o_ref):
    a = a_ref[...]
    g = _dot(a, bg_ref[...])
    u = _dot(a, bu_ref[...])
    o_ref[...] = (g * _sigmoid(g) * u).astype(o_ref.dtype)


def _mm_swiglu(a, w, tm, tn):
    M, K = a.shape
    H = w.shape[1] // 2
    tm = min(tm, M)
    nb = H // tn
    return pl.pallas_call(
        _swiglu_kernel,
        grid=(nb, M // tm),
        in_specs=[pl.BlockSpec((tm, K), lambda n, m: (m, 0)),
                  pl.BlockSpec((K, tn), lambda n, m: (0, n)),
                  pl.BlockSpec((K, tn), lambda n, m: (0, nb + n))],
        out_specs=pl.BlockSpec((tm, tn), lambda n, m: (m, n)),
        out_shape=jax.ShapeDtypeStruct((M, H), BF16),
        compiler_params=_cp("parallel", "parallel"),
        name="ffn_in_swiglu",
    )(a, w, w)


def _resid_kernel(a_ref, b_ref, x_ref, g_ref, o_ref):
    o_ref[...] = x_ref[...] + g_ref[...] * _dot(a_ref[...], b_ref[...])


def _mm_resid(a, b, x, gate, tm, tn, name):
    M, K = a.shape
    N = b.shape[1]
    tm = min(tm, M)
    return pl.pallas_call(
        _resid_kernel,
        grid=(N // tn, M // tm),
        in_specs=[pl.BlockSpec((tm, K), lambda n, m: (m, 0)),
                  pl.BlockSpec((K, tn), lambda n, m: (0, n)),
                  pl.BlockSpec((tm, tn), lambda n, m: (m, n)),
                  pl.BlockSpec((1, tn), lambda n, m: (0, n))],
        out_specs=pl.BlockSpec((tm, tn), lambda n, m: (m, n)),
        out_shape=jax.ShapeDtypeStruct((M, N), F32),
        compiler_params=_cp("parallel", "parallel"),
        name=name,
    )(a, b, x, gate)


def _merge_kernel(h_ref, wg0, wg1, wg2, wg3, oa, ob, oc, od, wa, wb, wc, wd, o_ref):
    h = h_ref[...]
    acc = None
    for wg, o_br, w_br in ((wg0, oa, wa), (wg1, ob, wb), (wg2, oc, wc), (wg3, od, wd)):
        term = _sigmoid(_dot(h, wg[...])) * _dot(o_br[...], w_br[...])
        acc = term if acc is None else acc + term
    o_ref[...] = acc.astype(o_ref.dtype)


def _merge(h, w_gate, branches, w_branches, tm, tn):
    M, D = h.shape
    tm = min(tm, M)
    nb = D // tn
    in_specs = [pl.BlockSpec((tm, D), lambda n, m: (m, 0))]
    for i in range(N_BRANCH):
        in_specs.append(pl.BlockSpec((D, tn), lambda n, m, i=i: (0, i * nb + n)))
    for br in branches:
        in_specs.append(pl.BlockSpec((tm, br.shape[1]), lambda n, m: (m, 0)))
    for wb in w_branches:
        in_specs.append(pl.BlockSpec((wb.shape[0], tn), lambda n, m: (0, n)))
    return pl.pallas_call(
        _merge_kernel,
        grid=(nb, M // tm),
        in_specs=in_specs,
        out_specs=pl.BlockSpec((tm, tn), lambda n, m: (m, n)),
        out_shape=jax.ShapeDtypeStruct((M, D), BF16),
        compiler_params=_cp("parallel", "parallel"),
        name="gated_merge",
    )(h, w_gate, w_gate, w_gate, w_gate, *branches, *w_branches)


def _gla_kernel(q_ref, k_ref, v_ref, g_ref, al_ref, wa_ref, ba_ref, gn_ref, lt_ref, blk_ref, e_ref, sel_ref,
                o_ref, st_ref, qe_ref, ke_ref, b_ref, dec_ref, acc_ref):
    C = GLA_SUB
    R = q_ref.shape[0]

    @pl.when(pl.program_id(0) == 0)
    def _():
        st_ref[...] = jnp.zeros_like(st_ref)

    gk = _log_sigmoid(_dot_hi(al_ref[...], wa_ref[...]) + ba_ref[...]) * (1.0 / GLA_NORMALIZER)
    b = _dot_hi(lt_ref[...], gk)
    tot = _dot_hi(blk_ref[...], gk)
    scale = GLA_DK ** -0.5
    qe_ref[...] = (q_ref[...] * scale * jnp.exp(b)).astype(BF16)
    ke_ref[...] = (k_ref[...] * jnp.exp(tot - b)).astype(BF16)
    b_ref[...] = b
    dec_ref[...] = jnp.exp(tot)

    e_mat = e_ref[...]
    sel = sel_ref[...]
    s_idx = lax.broadcasted_iota(jnp.int32, (C, GLA_HEADS * GLA_DK), 0)

    def body(c, carry):
        r0 = pl.multiple_of(c * C, C)
        rows = pl.ds(r0, C)
        qe = qe_ref[rows, :]
        ke = ke_ref[rows, :]
        vs = v_ref[rows, :]
        v16 = vs.astype(BF16)
        dec = dec_ref[pl.ds(r0, 1), :]
        qs = q_ref[rows, :] * scale
        ks = k_ref[rows, :]
        bs = b_ref[rows, :]
        pairs = []
        for t in range(C):
            e = jnp.exp(jnp.where(s_idx <= t, bs[t:t + 1, :] - bs, -jnp.inf))
            pairs.append(qs[t:t + 1, :] * ks * e)
        p_st = jnp.concatenate(pairs, axis=0).astype(BF16)
        sc = _dot(p_st, e_mat)
        w = (sc * jnp.concatenate([vs] * C, axis=0)).astype(BF16)
        intra = _dot(sel, w)
        outs = []
        for h in range(GLA_HEADS):
            st = st_ref[h]
            kh = slice(h * GLA_DK, (h + 1) * GLA_DK)
            vh = slice(h * GLA_DV, (h + 1) * GLA_DV)
            outs.append(_dot_nt(qe[:, kh], st.astype(BF16)))
            st_ref[h] = st * dec[:, kh] + _dot_tn(v16[:, vh], ke[:, kh])
        acc_ref[rows, :] = jnp.concatenate(outs, axis=1) + intra
        return carry

    lax.fori_loop(0, R // C, body, 0)

    o = acc_ref[...]
    g = g_ref[...]
    gn = gn_ref[...]
    parts = []
    for h in range(GLA_HEADS):
        oh = o[:, h * GLA_DV:(h + 1) * GLA_DV]
        parts.append(oh * lax.rsqrt(jnp.mean(oh * oh, axis=-1, keepdims=True) + NORM_EPS) * gn)
    o_ref[...] = (jnp.concatenate(parts, axis=1) * (g * _sigmoid(g))).astype(o_ref.dtype)


def _gla(p, w_a2, b_a2, g_norm):
    S = p.shape[0]
    R = min(S, 512)
    C = GLA_SUB
    HK, HV = GLA_HEADS * GLA_DK, GLA_HEADS * GLA_DV
    wa = jnp.zeros((128, HK), F32).at[:GLA_RANK].set(w_a2)
    ri = np.arange(R)
    same = (ri[:, None] // C) == (ri[None, :] // C)
    ltri = jnp.asarray((same & (ri[None, :] <= ri[:, None])).astype(np.float32))
    blk = jnp.asarray(same.astype(np.float32))
    e_mat = jnp.asarray((np.arange(HK)[:, None] // GLA_DK == np.arange(HV)[None, :] // GLA_DV).astype(np.float32), BF16)
    sel = jnp.asarray((np.arange(C)[:, None] == np.arange(C * C)[None, :] // C).astype(np.float32), BF16)
    const = lambda shape: pl.BlockSpec(shape, lambda i: (0, 0))
    return pl.pallas_call(
        _gla_kernel,
        grid=(S // R,),
        in_specs=[pl.BlockSpec((R, HK), lambda i: (i, 0)),
                  pl.BlockSpec((R, HK), lambda i: (i, 1)),
                  pl.BlockSpec((R, HV), lambda i: (i, 1)),
                  pl.BlockSpec((R, HV), lambda i: (i, 2)),
                  pl.BlockSpec((R, 128), lambda i: (i, (2 * HK + 2 * HV) // 128)),
                  const((128, HK)), const((1, HK)), const((1, GLA_DV)),
                  const((R, R)), const((R, R)), const((HK, HV)), const((C, C * C))],
        out_specs=pl.BlockSpec((R, HV), lambda i: (i, 0)),
        out_shape=jax.ShapeDtypeStruct((S, HV), BF16),
        scratch_shapes=[pltpu.VMEM((GLA_HEADS, GLA_DV, GLA_DK), F32),
                        pltpu.VMEM((R, HK), BF16), pltpu.VMEM((R, HK), BF16),
                        pltpu.VMEM((R, HK), F32), pltpu.VMEM((R, HK), F32),
                        pltpu.VMEM((R, HV), F32)],
        compiler_params=_cp("arbitrary"),
        name="gla",
    )(p, p, p, p, p, wa, b_a2[None, :], g_norm[None, :], ltri, blk, e_mat, sel)


def _band_kernel(q_ref, kp_ref, kc_ref, vp_ref, vc_ref, cc_ref, sc_ref, cp_ref, sp_ref, o_ref, l_ref):
    mb = pl.program_id(2)
    blk = DIL_BLK
    cos_c, sin_c = cc_ref[...], sc_ref[...]
    q = _rope(q_ref[...], cos_c, sin_c) * (HEAD_DIM ** -0.5)
    kc = _rope(kc_ref[...], cos_c, sin_c)
    kp = _rope(kp_ref[...], cp_ref[...], sp_ref[...])
    k2 = jnp.concatenate([kp, kc], axis=0).astype(BF16)
    s = _dot_nt(q.astype(BF16), k2)
    qi = lax.broadcasted_iota(jnp.int32, s.shape, 0)
    ki = lax.broadcasted_iota(jnp.int32, s.shape, 1)
    dist = blk + qi - ki
    ok = jnp.where(dist >= 0, jnp.where(dist <= blk, 1, 0), 0) * jnp.where(ki >= blk, 1, jnp.where(mb > 0, 1, 0))
    s = jnp.where(ok > 0, s, -jnp.inf)
    m = jnp.max(s, axis=-1, keepdims=True)
    pr = jnp.exp(s - m)
    den = jnp.sum(pr, axis=-1, keepdims=True)
    v2 = jnp.concatenate([vp_ref[...], vc_ref[...]], axis=0).astype(BF16)
    o_ref[...] = _dot(pr.astype(BF16), v2) / den
    l_ref[...] = jnp.broadcast_to(m + jnp.log(den), l_ref.shape)


def _banded_group(p_dil, cos_f, sin_f, group, dil):
    S = p_dil.shape[0]
    L = S // dil
    nb = L // DIL_BLK
    hp = DIL_HEADS_PER_GROUP
    ncol = DIL_IN // HEAD_DIM
    pf = p_dil.reshape(L, dil * DIL_IN)
    cf = cos_f.reshape(L, dil * HEAD_DIM)
    sf = sin_f.reshape(L, dil * HEAD_DIM)

    def col(part):
        return lambda h, r, mb: (mb, r * ncol + part * DIL_HEADS + group * hp + h)

    def col_prev(part):
        return lambda h, r, mb: (jnp.maximum(mb - 1, 0), r * ncol + part * DIL_HEADS + group * hp + h)

    blk = (DIL_BLK, HEAD_DIM)
    tab_c = lambda h, r, mb: (mb, r)
    tab_p = lambda h, r, mb: (jnp.maximum(mb - 1, 0), r)
    out_map = lambda h, r, mb: (mb, r * hp + h)
    o, l = pl.pallas_call(
        _band_kernel,
        grid=(hp, dil, nb),
        in_specs=[pl.BlockSpec(blk, col(0)), pl.BlockSpec(blk, col_prev(1)), pl.BlockSpec(blk, col(1)),
                  pl.BlockSpec(blk, col_prev(2)), pl.BlockSpec(blk, col(2)),
                  pl.BlockSpec(blk, tab_c), pl.BlockSpec(blk, tab_c), pl.BlockSpec(blk, tab_p), pl.BlockSpec(blk, tab_p)],
        out_specs=[pl.BlockSpec(blk, out_map)] * 2,
        out_shape=[jax.ShapeDtypeStruct((L, dil * DIL_OUT), F32)] * 2,
        compiler_params=_cp("parallel", "parallel", "parallel"),
        name=f"dilated_band_d{dil}",
    )(pf, pf, pf, pf, pf, cf, sf, cf, sf)
    return o.reshape(S, DIL_OUT), l.reshape(S, DIL_OUT)


def _dil_combine_kernel(o0, o1, o2, l0, l1, l2, out_ref):
    a, b, c = l0[...], l1[...], l2[...]
    m = jnp.maximum(jnp.maximum(a, b), c)
    ea, eb, ec = jnp.exp(a - m), jnp.exp(b - m), jnp.exp(c - m)
    out_ref[...] = ((ea * o0[...] + eb * o1[...] + ec * o2[...]) / (ea + eb + ec)).astype(out_ref.dtype)


def _dilated(p_dil, cos_f, sin_f):
    S = p_dil.shape[0]
    outs, lses = [], []
    for g, (window, dil) in enumerate(DIL_PAIRS):
        assert window // dil == DIL_BLK
        o, l = _banded_group(p_dil, cos_f, sin_f, g, dil)
        outs.append(o)
        lses.append(l)
    tm = min(S, 1024)
    spec = pl.BlockSpec((tm, DIL_OUT), lambda i: (i, 0))
    return pl.pallas_call(
        _dil_combine_kernel,
        grid=(S // tm,),
        in_specs=[spec] * 6,
        out_specs=spec,
        out_shape=jax.ShapeDtypeStruct((S, DIL_OUT), BF16),
        compiler_params=_cp("parallel"),
        name="dilated_combine",
    )(*outs, *lses)


def _rwkv_prep_kernel(has_vmix, *refs):
    if has_vmix:
        (p_ref, prev_ref, mu_ref, w0_ref, w2_ref, a0_ref, a2_ref, g2_ref, kk_ref, ka_ref, hd_ref,
         vf_ref, v0_ref, v1_ref, v2_ref, r_o, lw_o, k_o, v_o, a_o, b_o, g_o) = refs
    else:
        (p_ref, prev_ref, mu_ref, w0_ref, w2_ref, a0_ref, a2_ref, g2_ref, kk_ref, ka_ref, hd_ref,
         r_o, lw_o, k_o, v_o, a_o, b_o, g_o) = refs
    W = RWKV_WIDTH
    p = p_ref[...]
    last = jnp.where(pl.program_id(0) == 0, 0.0, prev_ref[7:8, :])
    row = lax.broadcasted_iota(jnp.int32, p.shape, 0)
    p_prev = jnp.where(row == 0, last, pltpu.roll(p, 1, axis=0))
    xs = p + (p_prev - p) * mu_ref[...]
    r, k, v = xs[:, 0:W], xs[:, W:2 * W], xs[:, 2 * W:3 * W]
    w_low = xs[:, 3 * W:3 * W + LORA_PAD]
    a_low = xs[:, 3 * W + LORA_PAD:3 * W + 2 * LORA_PAD]
    g_low = xs[:, 3 * W + 2 * LORA_PAD:]
    log_w = _log_sigmoid(w0_ref[...] + _dot_hi(jnp.tanh(w_low), w2_ref[...])) - 0.5
    a = _sigmoid(a0_ref[...] + _dot_hi(a_low, a2_ref[...]))
    g = _dot_hi(_sigmoid(g_low), g2_ref[...])
    if has_vmix:
        mix = _sigmoid(v0_ref[...] + _dot_hi(_dot_hi(v, v1_ref[...]), v2_ref[...]))
        v = v + (vf_ref[...] - v) * mix
    kk = k * kk_ref[...]
    kk = kk / jnp.maximum(jnp.sqrt(_dot_hi(kk * kk, hd_ref[...])), 1e-12)
    r_o[...] = r
    lw_o[...] = -jnp.exp(log_w)
    k_o[...] = k * (1.0 + (a - 1.0) * ka_ref[...])
    v_o[...] = v
    a_o[...] = -kk
    b_o[...] = kk * a
    g_o[...] = g


def _pad_rows(w, rows):
    return jnp.zeros((rows, w.shape[1]), w.dtype).at[:w.shape[0]].set(w)


def _pad_rwkv_cols(t):
    W = RWKV_WIDTH
    z = jnp.zeros(t.shape[:-1] + (LORA_PAD - RWKV_DECAY_LORA,), t.dtype)
    o1, o2 = 3 * W + RWKV_DECAY_LORA, 3 * W + RWKV_DECAY_LORA + RWKV_AAA_LORA
    return jnp.concatenate([t[..., :o1], z, t[..., o1:o2], z, t[..., o2:]], axis=-1)


def _head_indicator():
    i = np.arange(RWKV_WIDTH) // RWKV_HEAD_SIZE
    return jnp.asarray((i[:, None] == i[None, :]).astype(np.float32))


def _rwkv_prep(p, mu, w0, w2, a0, a2, g2, k_k, k_a, v_first, v_mix):
    S = p.shape[0]
    W = RWKV_WIDTH
    R = min(S, 512)
    has_vmix = v_mix is not None
    row = lambda n: pl.BlockSpec((1, n), lambda i: (0, 0))
    full = lambda a: pl.BlockSpec(a.shape, lambda i: (0, 0))
    tile = pl.BlockSpec((R, W), lambda i: (i, 0))
    w2p, a2p = _pad_rows(w2, LORA_PAD), _pad_rows(a2, LORA_PAD)
    hd = _head_indicator()
    args = [p, p, _pad_rwkv_cols(mu)[None, :], w0[None, :], w2p, a0[None, :], a2p, g2, k_k[None, :], k_a[None, :], hd]
    in_specs = [pl.BlockSpec((R, RWKV_IN_PAD), lambda i: (i, 0)),
                pl.BlockSpec((8, RWKV_IN_PAD), lambda i: (jnp.maximum(i * (R // 8) - 1, 0), 0)),
                row(RWKV_IN_PAD), row(W), full(w2p), row(W), full(a2p), full(g2), row(W), row(W), full(hd)]
    if has_vmix:
        v0, v1, v2 = v_mix
        args += [v_first, v0[None, :], v1, v2]
        in_specs += [tile, row(W), full(v1), full(v2)]
    return pl.pallas_call(
        functools.partial(_rwkv_prep_kernel, has_vmix),
        grid=(S // R,),
        in_specs=in_specs,
        out_specs=[tile] * 7,
        out_shape=[jax.ShapeDtypeStruct((S, W), F32)] * 7,
        compiler_params=_cp("parallel"),
        name="rwkv_prep",
    )(*args)


def _rwkv_scan_kernel(r_ref, lw_ref, k_ref, v_ref, a_ref, b_ref, lt_ref, mask_ref, eye_ref, y_ref, st_ref):
    C = RWKV_CHUNK
    N = RWKV_HEAD_SIZE
    n_chunks = r_ref.shape[1] // C

    @pl.when(pl.program_id(0) == 0)
    def _():
        st_ref[...] = jnp.zeros_like(st_ref)

    ltri = lt_ref[...]
    mask = mask_ref[...]
    eye = eye_ref[...]

    def chunk(h, c):
        rows = pl.ds(pl.multiple_of(c * C, C), C)
        rr, lw, kk, vv, aa, bb = (ref[h, rows, :] for ref in (r_ref, lw_ref, k_ref, v_ref, a_ref, b_ref))
        cum = _dot_hi(ltri, lw)
        e_in = jnp.exp(cum)
        e_inv = jnp.exp(-cum)
        a_t = aa * jnp.exp(cum - lw)
        r_t = rr * e_in
        b_t = bb * e_inv
        k_t = kk * e_inv
        m_all = _dot_nt(jnp.concatenate([a_t, r_t], axis=0), jnp.concatenate([b_t, k_t], axis=0), HI) * mask
        m_ab, m_ak = m_all[:C, :C], m_all[:C, C:]
        m_rb, m_rk = m_all[C:, :C], m_all[C:, C:]
        t_inv = eye + m_ab
        x = _dot_hi(m_ab, m_ab)
        for _ in range(5):
            xt = _dot_hi(x, jnp.concatenate([x, t_inv], axis=1))
            x = xt[:, :C]
            t_inv = t_inv + xt[:, C:]
        mv = _dot_hi(jnp.concatenate([m_ak, m_rk], axis=0), vv)
        pq = _dot_hi(t_inv, jnp.concatenate([a_t, mv[:C]], axis=1))
        p2q2 = jnp.concatenate([r_t, mv[C:]], axis=1) + _dot_hi(m_rb, pq)
        g_c = e_in[C - 1:C, :]
        gh = _dot_tn(b_t * g_c, pq, HI)
        g_mat = gh[:, :N] + eye * g_c
        h_mat = gh[:, N:] + _dot_tn(k_t * g_c, vv, HI)
        st = st_ref[h]
        y_ref[h, rows, :] = _dot_hi(p2q2[:, :N], st) + p2q2[:, N:]
        st_ref[h] = _dot_hi(g_mat, st) + h_mat

    def body(c, carry):
        for h in range(RWKV_HEADS):
            chunk(h, c)
        return carry

    lax.fori_loop(0, n_chunks, body, 0)


def _rwkv_scan(r, lw, k, v, a, b):
    H, S, N = r.shape
    C = RWKV_CHUNK
    R = min(S, 256)
    ti = np.arange(C)
    ltri = jnp.asarray((ti[None, :] <= ti[:, None]).astype(np.float32))
    strict = (ti[None, :] < ti[:, None]).astype(np.float32)
    incl = (ti[None, :] <= ti[:, None]).astype(np.float32)
    mask = jnp.asarray(np.block([[strict, strict], [incl, incl]]))
    eye = jnp.eye(C, dtype=F32)
    tile = pl.BlockSpec((H, R, N), lambda i: (0, i, 0))
    const = lambda a_: pl.BlockSpec(a_.shape, lambda i: (0, 0))
    return pl.pallas_call(
        _rwkv_scan_kernel,
        grid=(S // R,),
        in_specs=[tile] * 6 + [const(ltri), const(mask), const(eye)],
        out_specs=tile,
        out_shape=jax.ShapeDtypeStruct((H, S, N), F32),
        scratch_shapes=[pltpu.VMEM((H, N, N), F32)],
        compiler_params=_cp("arbitrary"),
        name="rwkv_scan",
    )(r, lw, k, v, a, b, ltri, mask, eye)


def _rwkv_post_kernel(y_ref, r_ref, k_ref, v_ref, g_ref, rk_ref, lw_ref, lb_ref, hd_ref, o_ref):
    hd = hd_ref[...]
    inv_n = 1.0 / RWKV_HEAD_SIZE
    y = y_ref[...]
    yc = y - _dot_hi(y, hd) * inv_n
    var = _dot_hi(yc * yc, hd) * inv_n
    yn = yc * lax.rsqrt(var + RWKV_LNX_EPS) * lw_ref[...] + lb_ref[...]
    bonus = _dot_hi(r_ref[...] * k_ref[...] * rk_ref[...], hd) * v_ref[...]
    o_ref[...] = ((yn + bonus) * g_ref[...]).astype(o_ref.dtype)


def _rwkv_post(y, r, k, v, g, r_k, lnx_w, lnx_b):
    S, W = y.shape
    R = min(S, 512)
    tile = pl.BlockSpec((R, W), lambda i: (i, 0))
    row = pl.BlockSpec((1, W), lambda i: (0, 0))
    hd = _head_indicator()
    return pl.pallas_call(
        _rwkv_post_kernel,
        grid=(S // R,),
        in_specs=[tile] * 5 + [row] * 3 + [pl.BlockSpec((W, W), lambda i: (0, 0))],
        out_specs=tile,
        out_shape=jax.ShapeDtypeStruct((S, W), BF16),
        compiler_params=_cp("parallel"),
        name="rwkv_post",
    )(y, r, k, v, g, r_k[None, :], lnx_w[None, :], lnx_b[None, :], hd)


def _rwkv(p, mu, w0, w2, a0, a2, g2, k_k, k_a, r_k, lnx_w, lnx_b, v_first, v_mix):
    S = p.shape[0]
    r, lw, k, v, a, b, g = _rwkv_prep(p, mu, w0, w2, a0, a2, g2, k_k, k_a, v_first, v_mix)
    heads = lambda t: t.reshape(S, RWKV_HEADS, RWKV_HEAD_SIZE).transpose(1, 0, 2)
    y = _rwkv_scan(*(heads(t) for t in (r, lw, k, v, a, b)))
    y = y.transpose(1, 0, 2).reshape(S, RWKV_WIDTH)
    out = _rwkv_post(y, r, k, v, g, r_k, lnx_w, lnx_b)
    return out, (v if v_mix is None else v_first)


def _moba_prep_kernel(k_ref, v_ref, c_ref, s_ref, kr_ref, vb_ref, km_ref):
    cos_f, sin_f = c_ref[...], s_ref[...]
    parts = []
    for h in range(MOBA_HEADS):
        parts.append(_rope(k_ref[:, h * HEAD_DIM:(h + 1) * HEAD_DIM], cos_f, sin_f))
    kr = jnp.concatenate(parts, axis=1)
    kr_ref[...] = kr.astype(BF16)
    vb_ref[...] = v_ref[...].astype(BF16)
    km_ref[...] = jnp.broadcast_to(jnp.mean(kr, axis=0, keepdims=True), km_ref.shape)


def _moba_kernel(q_ref, c_ref, s_ref, k_ref, v_ref, km_ref, o_ref):
    cb = pl.program_id(1)
    blk = MOBA_BLOCK
    nb = km_ref.shape[0]
    q = _rope(q_ref[...], c_ref[...], s_ref[...]) * (HEAD_DIM ** -0.5)
    bi = lax.broadcasted_iota(jnp.int32, (blk, nb), 1)
    gate = jnp.where(bi < cb, _dot_nt(q, km_ref[...], HI), -jnp.inf)
    sel = jnp.zeros((blk, nb), F32)
    for _ in range(MOBA_TOPK):
        m = jnp.max(gate, axis=1, keepdims=True)
        idx = jnp.min(jnp.where(gate == m, bi, nb), axis=1, keepdims=True)
        hit = bi == idx
        sel = jnp.where(hit, jnp.where(m > -jnp.inf, 1.0, sel), sel)
        gate = jnp.where(hit, -jnp.inf, gate)
    qb = q.astype(BF16)
    own = pl.ds(pl.multiple_of(cb * blk, blk), blk)
    s = _dot_nt(qb, k_ref[own, :])
    qi = lax.broadcasted_iota(jnp.int32, s.shape, 0)
    ki = lax.broadcasted_iota(jnp.int32, s.shape, 1)
    s = jnp.where(ki <= qi, s, -jnp.inf)
    m0 = jnp.max(s, axis=1, keepdims=True)
    p0 = jnp.exp(s - m0)
    l0 = jnp.sum(p0, axis=1, keepdims=True)
    acc0 = _dot(p0.astype(BF16), v_ref[own, :])

    def body(j, carry):
        m, l, acc = carry
        rows = pl.ds(pl.multiple_of(j * blk, blk), blk)
        picked = jnp.sum(jnp.where(bi == j, sel, 0.0), axis=1, keepdims=True)
        s = jnp.where(picked > 0.5, _dot_nt(qb, k_ref[rows, :]), -jnp.inf)
        m_new = jnp.maximum(m, jnp.max(s, axis=1, keepdims=True))
        alpha = jnp.exp(m - m_new)
        pr = jnp.exp(s - m_new)
        l = alpha * l + jnp.sum(pr, axis=1, keepdims=True)
        acc = alpha * acc + _dot(pr.astype(BF16), v_ref[rows, :])
        return m_new, l, acc

    m, l, acc = lax.fori_loop(0, cb, body, (m0, l0, acc0))
    o_ref[...] = (acc / l).astype(o_ref.dtype)


def _moba(p, cos_f, sin_f):
    S = p.shape[0]
    blk = MOBA_BLOCK
    nb = S // blk
    HW = MOBA_HEADS * HEAD_DIM
    kr, vb, km = pl.pallas_call(
        _moba_prep_kernel,
        grid=(nb,),
        in_specs=[pl.BlockSpec((blk, HW), lambda i: (i, 1)), pl.BlockSpec((blk, HW), lambda i: (i, 2)),
                  pl.BlockSpec((blk, HEAD_DIM), lambda i: (i, 0)), pl.BlockSpec((blk, HEAD_DIM), lambda i: (i, 0))],
        out_specs=[pl.BlockSpec((blk, HW), lambda i: (i, 0)), pl.BlockSpec((blk, HW), lambda i: (i, 0)),
                   pl.BlockSpec((8, HW), lambda i: (i, 0))],
        out_shape=[jax.ShapeDtypeStruct((S, HW), BF16), jax.ShapeDtypeStruct((S, HW), BF16),
                   jax.ShapeDtypeStruct((nb * 8, HW), F32)],
        compiler_params=_cp("parallel"),
        name="moba_prep",
    )(p, p, cos_f, sin_f)
    k_mean = km.reshape(nb, 8, HW)[:, 0]
    tab = pl.BlockSpec((blk, HEAD_DIM), lambda h, cb: (cb, 0))
    return pl.pallas_call(
        _moba_kernel,
        grid=(MOBA_HEADS, nb),
        in_specs=[pl.BlockSpec((blk, HEAD_DIM), lambda h, cb: (cb, h)), tab, tab,
                  pl.BlockSpec((S, HEAD_DIM), lambda h, cb: (0, h)),
                  pl.BlockSpec((S, HEAD_DIM), lambda h, cb: (0, h)),
                  pl.BlockSpec((nb, HEAD_DIM), lambda h, cb: (0, h))],
        out_specs=pl.BlockSpec((blk, HEAD_DIM), lambda h, cb: (cb, h)),
        out_shape=jax.ShapeDtypeStruct((S, HW), BF16),
        compiler_params=_cp("parallel", "arbitrary"),
        name="moba",
    )(p, cos_f, sin_f, kr, vb, k_mean)


def _split_w_in(w_in_l):
    o = np.cumsum((N_BRANCH * D_MODEL, GLA_IN, DIL_IN, RWKV_IN, MOBA_IN))
    w_gate, w_gla, w_dil, w_rwkv, w_moba = (w_in_l[:, a:b] for a, b in zip((0,) + tuple(o[:-1]), o))
    w_gla = jnp.concatenate([w_gla, jnp.zeros((D_MODEL, GLA_IN_PAD - GLA_IN), w_gla.dtype)], axis=1)
    w_rwkv = _pad_rwkv_cols(w_rwkv)
    return tuple(w.astype(BF16) for w in (w_gate, w_gla, w_dil, w_rwkv, w_moba))


def kernel(x, c, positions, w_ada, b_ada, norm1, w_in, gla_w_a2, gla_b_a2, gla_gnorm, rwkv_mu, rwkv_w0, rwkv_w2, rwkv_a0, rwkv_a2, rwkv_g2, rwkv_k_k, rwkv_k_a, rwkv_r_k, rwkv_lnx_w, rwkv_lnx_b, rwkv_v0, rwkv_v1, rwkv_v2, w_branch_a, w_branch_b, w_branch_c, w_branch_d, w_out, norm2, w_ffn_in, w_ffn_out, norm_f):
    B, S, D = x.shape
    assert B == 1 and D == D_MODEL
    xs = x.reshape(S, D)
    cos_f, sin_f = _rope_tables(positions)
    c8 = jnp.broadcast_to(c, (8, D))
    v_first = None
    for l in range(DEPTH):
        mod = _adaln(c8, w_ada, b_ada, l)[0:1]
        shift1, scale1, gate1, shift2, scale2, gate2 = jnp.split(mod, 6, axis=-1)

        h = _normmod(xs, norm1[l][None, :], scale1, shift1, BF16)
        w_gate, w_gla, w_dil, w_rwkv, w_moba = _split_w_in(w_in[l])
        p_gla = _mm(h, w_gla, 512, GLA_IN_PAD, F32, "proj_gla")
        p_dil = _mm(h, w_dil, 512, DIL_IN, F32, "proj_dil")
        p_rwkv = _mm(h, w_rwkv, 512, RWKV_IN_PAD, F32, "proj_rwkv")
        p_moba = _mm(h, w_moba, 512, MOBA_IN, F32, "proj_moba")

        o_gla = _gla(p_gla, gla_w_a2[l], gla_b_a2[l], gla_gnorm[l])
        o_dil = _dilated(p_dil, cos_f, sin_f)
        v_mix = None if l == 0 else (rwkv_v0[l - 1], rwkv_v1[l - 1], rwkv_v2[l - 1])
        o_rwkv, v_first = _rwkv(p_rwkv, rwkv_mu[l], rwkv_w0[l], rwkv_w2[l], rwkv_a0[l], rwkv_a2[l], rwkv_g2[l],
                                rwkv_k_k[l], rwkv_k_a[l], rwkv_r_k[l], rwkv_lnx_w[l], rwkv_lnx_b[l], v_first, v_mix)
        o_moba = _moba(p_moba, cos_f, sin_f)

        merged = _merge(h, w_gate, (o_gla, o_dil, o_rwkv, o_moba),
                        tuple(w[l].astype(BF16) for w in (w_branch_a, w_branch_b, w_branch_c, w_branch_d)), 1024, 512)
        xs = _mm_resid(merged, w_out[l].astype(BF16), xs, gate1, 512, 512, "out_proj_resid")

        h2 = _normmod(xs, norm2[l][None, :], scale2, shift2, BF16)
        act = _mm_swiglu(h2, w_ffn_in[l].astype(BF16), 1024, 512)
        xs = _mm_resid(act, w_ffn_out[l].astype(BF16), xs, gate2, 512, 512, "ffn_out_resid")

    zero = jnp.zeros((1, D), F32)
    return _normmod(xs, norm_f[None, :], zero, zero, F32).reshape(B, S, D)
```

```python
import functools

import numpy as np
import jax
import jax.numpy as jnp
from jax import lax
from jax.experimental import pallas as pl
from jax.experimental.pallas import tpu as pltpu

F32 = jnp.float32
BF16 = jnp.bfloat16
HI = lax.Precision.HIGHEST

D_MODEL = 2048
DEPTH = 2
HEAD_DIM = 128
ROT_DIM = HEAD_DIM // 4
ROPE_THETA = 500000.0
N_BRANCH = 4
NORM_EPS = 1e-6

GLA_HEADS = 4
GLA_DK = 64
GLA_DV = 128
GLA_RANK = 16
GLA_NORMALIZER = 16.0
GLA_SUB = 16

DIL_PAIRS = ((128, 1), (512, 4), (2048, 16))
DIL_HEADS_PER_GROUP = 2
DIL_HEADS = DIL_HEADS_PER_GROUP * len(DIL_PAIRS)
DIL_BLK = 128

RWKV_HEAD_SIZE = 64
RWKV_HEADS = 8
RWKV_WIDTH = RWKV_HEADS * RWKV_HEAD_SIZE
RWKV_DECAY_LORA = 96
RWKV_AAA_LORA = 96
RWKV_MV_LORA = 64
RWKV_GATE_LORA = 256
RWKV_LNX_EPS = 64e-5
RWKV_CHUNK = 64
LORA_PAD = 128

MOBA_HEADS = 4
MOBA_BLOCK = 256
MOBA_TOPK = 3
MOBA_GROUP = 4

FFN_HIDDEN = -(-8 * D_MODEL // (3 * 256)) * 256

GLA_SIZES = (GLA_HEADS * GLA_DK, GLA_HEADS * GLA_DK, GLA_HEADS * GLA_DV, GLA_HEADS * GLA_DV, GLA_RANK)
RWKV_SIZES = (RWKV_WIDTH, RWKV_WIDTH, RWKV_WIDTH, RWKV_DECAY_LORA, RWKV_AAA_LORA, RWKV_GATE_LORA)
GLA_IN = sum(GLA_SIZES)
DIL_IN = 3 * DIL_HEADS * HEAD_DIM
RWKV_IN = sum(RWKV_SIZES)
MOBA_IN = 3 * MOBA_HEADS * HEAD_DIM
GLA_OUT = GLA_HEADS * GLA_DV
DIL_OUT = DIL_HEADS_PER_GROUP * HEAD_DIM
RWKV_OUT = RWKV_WIDTH
MOBA_OUT = MOBA_HEADS * HEAD_DIM

GLA_IN_PAD = GLA_IN - GLA_RANK + 128
RWKV_IN_PAD = 3 * RWKV_WIDTH + 2 * LORA_PAD + RWKV_GATE_LORA

VMEM_LIMIT_BYTES = 56 * 1024 * 1024
SLAB = 128


def _cp(*sem):
    return pltpu.CompilerParams(dimension_semantics=sem, vmem_limit_bytes=VMEM_LIMIT_BYTES)


def _dot(a, b):
    return jnp.dot(a, b, preferred_element_type=F32)


def _dot_hi(a, b):
    return jnp.dot(a, b, precision=HI, preferred_element_type=F32)


def _split3(x):
    x1 = x.astype(BF16)
    r1 = x - x1.astype(F32)
    x2 = r1.astype(BF16)
    x3 = (r1 - x2.astype(F32)).astype(BF16)
    return x1, x2, x3


def _dot_sel_l(sel16, x):
    x1, x2, x3 = _split3(x)
    return _dot(sel16, x1) + _dot(sel16, x2) + _dot(sel16, x3)


def _dot_sel_r(x, sel16):
    x1, x2, x3 = _split3(x)
    return _dot(x1, sel16) + _dot(x2, sel16) + _dot(x3, sel16)


def _group_sum_lanes(x, ind16):
    n = ind16.shape[0]
    return jnp.concatenate([_dot_sel_r(x[:, i:i + n], ind16) for i in range(0, x.shape[1], n)], axis=1)


def _group_rows(sel16, parts):
    n = sel16.shape[1]
    slabs = range(0, parts[0].shape[0], n)
    return jnp.concatenate([sum(_dot(sel16, p[i:i + n]) for p in parts) for i in slabs], axis=0)


def _dot_nt(a, b, precision=None):
    return lax.dot_general(a, b, (((1,), (1,)), ((), ())), precision=precision, preferred_element_type=F32)


def _dot_tn(a, b, precision=None):
    return lax.dot_general(a, b, (((0,), (0,)), ((), ())), precision=precision, preferred_element_type=F32)


def _log_sigmoid(z):
    return jnp.minimum(z, 0.0) - jnp.log1p(jnp.exp(-jnp.abs(z)))


def _sigmoid(z):
    return 1.0 / (1.0 + jnp.exp(-z))


def _rope(x, cos_f, sin_f):
    half = ROT_DIM // 2
    lane = lax.broadcasted_iota(jnp.int32, x.shape, 1)
    partner = jnp.where(lane < half, pltpu.roll(x, HEAD_DIM - half, axis=1), pltpu.roll(x, half, axis=1))
    return x * cos_f + partner * sin_f


def _rope_table_kernel(pos_ref, inv_ref, cos_ref, sin_ref):
    ang = pos_ref[...] * inv_ref[...]
    lane = lax.broadcasted_iota(jnp.int32, ang.shape, 1)
    half = ROT_DIM // 2
    c = jnp.cos(ang)
    s = jnp.sin(ang)
    cos_ref[...] = jnp.where(lane < ROT_DIM, c, 1.0)
    sin_ref[...] = jnp.where(lane < half, -s, jnp.where(lane < ROT_DIM, s, 0.0))


def _rope_tables(positions):
    S = positions.shape[1]
    inv = ROPE_THETA ** (-jnp.arange(0, ROT_DIM, 2, dtype=F32) / ROT_DIM)
    inv_full = jnp.concatenate([inv, inv, jnp.zeros((HEAD_DIM - ROT_DIM,), F32)])[None, :]
    pos = positions.astype(F32).reshape(S, 1)
    tm = min(S, 1024)
    return pl.pallas_call(
        _rope_table_kernel,
        grid=(S // tm,),
        in_specs=[pl.BlockSpec((tm, 1), lambda i: (i, 0)), pl.BlockSpec((1, HEAD_DIM), lambda i: (0, 0))],
        out_specs=[pl.BlockSpec((tm, HEAD_DIM), lambda i: (i, 0))] * 2,
        out_shape=[jax.ShapeDtypeStruct((S, HEAD_DIM), F32)] * 2,
        compiler_params=_cp("parallel"),
        name="rope_tables",
    )(pos, inv_full)


def _ada_kernel(c_ref, w_ref, b_ref, o_ref):
    c = c_ref[...]
    o_ref[...] = _dot_hi(c * _sigmoid(c), w_ref[...]) + b_ref[...]


def _adaln(c8, w_ada, b_ada, layer):
    D = D_MODEL
    tn = 1536
    return pl.pallas_call(
        _ada_kernel,
        grid=(6 * D // tn,),
        in_specs=[pl.BlockSpec((8, D), lambda n: (0, 0)),
                  pl.BlockSpec((None, D, tn), lambda n: (layer, 0, n)),
                  pl.BlockSpec((None, 1, tn), lambda n: (layer, 0, n))],
        out_specs=pl.BlockSpec((8, tn), lambda n: (0, n)),
        out_shape=jax.ShapeDtypeStruct((8, 6 * D), F32),
        compiler_params=_cp("parallel"),
        name="adaln",
    )(c8, w_ada, b_ada.reshape(DEPTH, 1, 6 * D))


def _normmod_kernel(x_ref, g_ref, sc_ref, sh_ref, o_ref):
    x = x_ref[...]
    y = x * lax.rsqrt(jnp.mean(x * x, axis=-1, keepdims=True) + NORM_EPS) * g_ref[...]
    o_ref[...] = (y * (1.0 + sc_ref[...]) + sh_ref[...]).astype(o_ref.dtype)


def _normmod(x, gain, scale, shift, out_dtype):
    S, D = x.shape
    tm = min(S, 512)
    row = pl.BlockSpec((1, D), lambda i: (0, 0))
    return pl.pallas_call(
        _normmod_kernel,
        grid=(S // tm,),
        in_specs=[pl.BlockSpec((tm, D), lambda i: (i, 0)), row, row, row],
        out_specs=pl.BlockSpec((tm, D), lambda i: (i, 0)),
        out_shape=jax.ShapeDtypeStruct((S, D), out_dtype),
        compiler_params=_cp("parallel"),
        name="normmod",
    )(x, gain, scale, shift)


def _mm_kernel(a_ref, b_ref, o_ref):
    o_ref[...] = _dot(a_ref[...], b_ref[...]).astype(o_ref.dtype)


def _mm(a, b, tm, tn, out_dtype, name):
    M, K = a.shape
    N = b.shape[1]
    tm, tn = min(tm, M), min(tn, N)
    return pl.pallas_call(
        _mm_kernel,
        grid=(N // tn, M // tm),
        in_specs=[pl.BlockSpec((tm, K), lambda n, m: (m, 0)), pl.BlockSpec((K, tn), lambda n, m: (0, n))],
        out_specs=pl.BlockSpec((tm, tn), lambda n, m: (m, n)),
        out_shape=jax.ShapeDtypeStruct((M, N), out_dtype),
        compiler_params=_cp("parallel", "parallel"),
        name=name,
    )(a, b)


def _swiglu_kernel(a_ref, bg_ref, bu_ref, o_ref):
    a = a_ref[...]
    g = _dot(a, bg_ref[...])
    u = _dot(a, bu_ref[...])
    o_ref[...] = (g * _sigmoid(g) * u).astype(o_ref.dtype)


def _mm_swiglu(a, w, tm, tn):
    M, K = a.shape
    H = w.shape[1] // 2
    tm = min(tm, M)
    nb = H // tn
    return pl.pallas_call(
        _swiglu_kernel,
        grid=(nb, M // tm),
        in_specs=[pl.BlockSpec((tm, K), lambda n, m: (m, 0)),
                  pl.BlockSpec((K, tn), lambda n, m: (0, n)),
                  pl.BlockSpec((K, tn), lambda n, m: (0, nb + n))],
        out_specs=pl.BlockSpec((tm, tn), lambda n, m: (m, n)),
        out_shape=jax.ShapeDtypeStruct((M, H), BF16),
        compiler_params=_cp("parallel", "parallel"),
        name="ffn_in_swiglu",
    )(a, w, w)


def _resid_kernel(a_ref, b_ref, x_ref, g_ref, o_ref):
    o_ref[...] = x_ref[...] + g_ref[...] * _dot(a_ref[...], b_ref[...])


def _mm_resid(a, b, x, gate, tm, tn, name):
    M, K = a.shape
    N = b.shape[1]
    tm = min(tm, M)
    return pl.pallas_call(
        _resid_kernel,
        grid=(N // tn, M // tm),
        in_specs=[pl.BlockSpec((tm, K), lambda n, m: (m, 0)),
                  pl.BlockSpec((K, tn), lambda n, m: (0, n)),
                  pl.BlockSpec((tm, tn), lambda n, m: (m, n)),
                  pl.BlockSpec((1, tn), lambda n, m: (0, n))],
        out_specs=pl.BlockSpec((tm, tn), lambda n, m: (m, n)),
        out_shape=jax.ShapeDtypeStruct((M, N), F32),
        compiler_params=_cp("parallel", "parallel"),
        name=name,
    )(a, b, x, gate)


def _merge_kernel(h_ref, wg0, wg1, wg2, wg3, oa, ob, oc, od, wa, wb, wc, wd, o_ref):
    h = h_ref[...]
    acc = None
    for wg, o_br, w_br in ((wg0, oa, wa), (wg1, ob, wb), (wg2, oc, wc), (wg3, od, wd)):
        term = _sigmoid(_dot(h, wg[...])) * _dot(o_br[...], w_br[...])
        acc = term if acc is None else acc + term
    o_ref[...] = acc.astype(o_ref.dtype)


def _merge(h, w_gate, branches, w_branches, tm, tn):
    M, D = h.shape
    tm = min(tm, M)
    nb = D // tn
    in_specs = [pl.BlockSpec((tm, D), lambda n, m: (m, 0))]
    for i in range(N_BRANCH):
        in_specs.append(pl.BlockSpec((D, tn), lambda n, m, i=i: (0, i * nb + n)))
    for br in branches:
        in_specs.append(pl.BlockSpec((tm, br.shape[1]), lambda n, m: (m, 0)))
    for wb in w_branches:
        in_specs.append(pl.BlockSpec((wb.shape[0], tn), lambda n, m: (0, n)))
    return pl.pallas_call(
        _merge_kernel,
        grid=(nb, M // tm),
        in_specs=in_specs,
        out_specs=pl.BlockSpec((tm, tn), lambda n, m: (m, n)),
        out_shape=jax.ShapeDtypeStruct((M, D), BF16),
        compiler_params=_cp("parallel", "parallel"),
        name="gated_merge",
    )(h, w_gate, w_gate, w_gate, w_gate, *branches, *w_branches)


def _gla_kernel(q_ref, k_ref, v_ref, g_ref, al_ref, wa_ref, ba_ref, gn_ref, lt_ref, blk_ref, e_ref, sel_ref,
                o_ref, st_ref, qe_ref, ke_ref, b_ref, dec_ref, acc_ref):
    C = GLA_SUB
    R = q_ref.shape[0]

    @pl.when(pl.program_id(0) == 0)
    def _():
        st_ref[...] = jnp.zeros_like(st_ref)

    gk = _log_sigmoid(_dot_hi(al_ref[...], wa_ref[...]) + ba_ref[...]) * (1.0 / GLA_NORMALIZER)
    gk_parts = _split3(gk)
    b = _group_rows(lt_ref[...], gk_parts)
    tot = _group_rows(blk_ref[...], gk_parts)
    scale = GLA_DK ** -0.5
    qe_ref[...] = (q_ref[...] * scale * jnp.exp(b)).astype(BF16)
    ke_ref[...] = (k_ref[...] * jnp.exp(tot - b)).astype(BF16)
    b_ref[...] = b
    dec_ref[...] = jnp.exp(tot)

    e_mat = e_ref[...]
    sel = sel_ref[...]
    s_idx = lax.broadcasted_iota(jnp.int32, (C, GLA_HEADS * GLA_DK), 0)

    def body(c, carry):
        r0 = pl.multiple_of(c * C, C)
        rows = pl.ds(r0, C)
        qe = qe_ref[rows, :]
        ke = ke_ref[rows, :]
        vs = v_ref[rows, :]
        v16 = vs.astype(BF16)
        dec = dec_ref[pl.ds(r0, 1), :]
        qs = q_ref[rows, :] * scale
        ks = k_ref[rows, :]
        bs = b_ref[rows, :]
        pairs = []
        for t in range(C):
            e = jnp.exp(jnp.where(s_idx <= t, bs[t:t + 1, :] - bs, -jnp.inf))
            pairs.append(qs[t:t + 1, :] * ks * e)
        p_st = jnp.concatenate(pairs, axis=0).astype(BF16)
        sc = _dot(p_st, e_mat)
        w = (sc * jnp.concatenate([vs] * C, axis=0)).astype(BF16)
        intra = _dot(sel, w)
        outs = []
        for h in range(GLA_HEADS):
            st = st_ref[h]
            kh = slice(h * GLA_DK, (h + 1) * GLA_DK)
            vh = slice(h * GLA_DV, (h + 1) * GLA_DV)
            outs.append(_dot_nt(qe[:, kh], st.astype(BF16)))
            st_ref[h] = st * dec[:, kh] + _dot_tn(v16[:, vh], ke[:, kh])
        acc_ref[rows, :] = jnp.concatenate(outs, axis=1) + intra
        return carry

    lax.fori_loop(0, R // C, body, 0, unroll=2)

    o = acc_ref[...]
    g = g_ref[...]
    gn = gn_ref[...]
    parts = []
    for h in range(GLA_HEADS):
        oh = o[:, h * GLA_DV:(h + 1) * GLA_DV]
        parts.append(oh * lax.rsqrt(jnp.mean(oh * oh, axis=-1, keepdims=True) + NORM_EPS) * gn)
    o_ref[...] = (jnp.concatenate(parts, axis=1) * (g * _sigmoid(g))).astype(o_ref.dtype)


def _gla(p, w_a2, b_a2, g_norm):
    S = p.shape[0]
    R = min(S, 512)
    C = GLA_SUB
    HK, HV = GLA_HEADS * GLA_DK, GLA_HEADS * GLA_DV
    wa = jnp.zeros((128, HK), F32).at[:GLA_RANK].set(w_a2)
    ri = np.arange(SLAB)
    same = (ri[:, None] // C) == (ri[None, :] // C)
    ltri = jnp.asarray((same & (ri[None, :] <= ri[:, None])).astype(np.float32), BF16)
    blk = jnp.asarray(same.astype(np.float32), BF16)
    e_mat =jnp.asarray((np.arange(HK)[:, None] // GLA_DK == np.arange(HV)[None, :] // GLA_DV).astype(np.float32), BF16)
    sel = jnp.asarray((np.arange(C)[:, None] == np.arange(C * C)[None, :] // C).astype(np.float32), BF16)
    const = lambda shape: pl.BlockSpec(shape, lambda i: (0, 0))
    return pl.pallas_call(
        _gla_kernel,
        grid=(S // R,),
        in_specs=[pl.BlockSpec((R, HK), lambda i: (i, 0)),
                  pl.BlockSpec((R, HK), lambda i: (i, 1)),
                  pl.BlockSpec((R, HV), lambda i: (i, 1)),
                  pl.BlockSpec((R, HV), lambda i: (i, 2)),
                  pl.BlockSpec((R, 128), lambda i: (i, (2 * HK + 2 * HV) // 128)),
                  const((128, HK)), const((1, HK)), const((1, GLA_DV)),
                  const((SLAB, SLAB)), const((SLAB, SLAB)), const((HK, HV)), const((C, C * C))],
        out_specs=pl.BlockSpec((R, HV), lambda i: (i, 0)),
        out_shape=jax.ShapeDtypeStruct((S, HV), BF16),
        scratch_shapes=[pltpu.VMEM((GLA_HEADS, GLA_DV, GLA_DK), F32),
                        pltpu.VMEM((R, HK), BF16), pltpu.VMEM((R, HK), BF16),
                        pltpu.VMEM((R, HK), F32), pltpu.VMEM((R, HK), F32),
                        pltpu.VMEM((R, HV), F32)],
        compiler_params=_cp("arbitrary"),
        name="gla",
    )(p, p, p, p, p, wa, b_a2[None, :], g_norm[None, :], ltri, blk, e_mat, sel)


def _band_kernel(q_ref, kp_ref, kc_ref, vp_ref, vc_ref, cc_ref, sc_ref, cp_ref, sp_ref, o_ref, l_ref):
    mb = pl.program_id(2)
    blk = DIL_BLK
    cos_c, sin_c = cc_ref[...], sc_ref[...]
    q = _rope(q_ref[...], cos_c, sin_c) * (HEAD_DIM ** -0.5)
    kc = _rope(kc_ref[...], cos_c, sin_c)
    kp = _rope(kp_ref[...], cp_ref[...], sp_ref[...])
    k2 = jnp.concatenate([kp, kc], axis=0).astype(BF16)
    s = _dot_nt(q.astype(BF16), k2)
    qi = lax.broadcasted_iota(jnp.int32, s.shape, 0)
    ki = lax.broadcasted_iota(jnp.int32, s.shape, 1)
    dist = blk + qi - ki
    ok = jnp.where(dist >= 0, jnp.where(dist <= blk, 1, 0), 0) * jnp.where(ki >= blk, 1, jnp.where(mb > 0, 1, 0))
    s = jnp.where(ok > 0, s, -jnp.inf)
    m = jnp.max(s, axis=-1, keepdims=True)
    pr = jnp.exp(s - m)
    den = jnp.sum(pr, axis=-1, keepdims=True)
    v2 = jnp.concatenate([vp_ref[...], vc_ref[...]], axis=0).astype(BF16)
    o_ref[...] = _dot(pr.astype(BF16), v2) / den
    l_ref[...] = jnp.broadcast_to(m + jnp.log(den), l_ref.shape)


def _banded_group(p_dil, cos_f, sin_f, group, dil):
    S = p_dil.shape[0]
    L = S // dil
    nb = L // DIL_BLK
    hp = DIL_HEADS_PER_GROUP
    ncol = DIL_IN // HEAD_DIM
    pf = p_dil.reshape(L, dil * DIL_IN)
    cf = cos_f.reshape(L, dil * HEAD_DIM)
    sf = sin_f.reshape(L, dil * HEAD_DIM)

    def col(part):
        return lambda h, r, mb: (mb, r * ncol + part * DIL_HEADS + group * hp + h)

    def col_prev(part):
        return lambda h, r, mb: (jnp.maximum(mb - 1, 0), r * ncol + part * DIL_HEADS + group * hp + h)

    blk = (DIL_BLK, HEAD_DIM)
    tab_c = lambda h, r, mb: (mb, r)
    tab_p = lambda h, r, mb: (jnp.maximum(mb - 1, 0), r)
    out_map = lambda h, r, mb: (mb, r * hp + h)
    o, l = pl.pallas_call(
        _band_kernel,
        grid=(hp, dil, nb),
        in_specs=[pl.BlockSpec(blk, col(0)), pl.BlockSpec(blk, col_prev(1)), pl.BlockSpec(blk, col(1)),
                  pl.BlockSpec(blk, col_prev(2)), pl.BlockSpec(blk, col(2)),
                  pl.BlockSpec(blk, tab_c), pl.BlockSpec(blk, tab_c), pl.BlockSpec(blk, tab_p), pl.BlockSpec(blk, tab_p)],
        out_specs=[pl.BlockSpec(blk, out_map)] * 2,
        out_shape=[jax.ShapeDtypeStruct((L, dil * DIL_OUT), F32)] * 2,
        compiler_params=_cp("parallel", "parallel", "parallel"),
        name=f"dilated_band_d{dil}",
    )(pf, pf, pf, pf, pf, cf, sf, cf, sf)
    return o.reshape(S, DIL_OUT), l.reshape(S, DIL_OUT)


def _dil_combine_kernel(o0, o1, o2, l0, l1, l2, out_ref):
    a, b, c = l0[...], l1[...], l2[...]
    m = jnp.maximum(jnp.maximum(a, b), c)
    ea, eb, ec = jnp.exp(a - m), jnp.exp(b - m), jnp.exp(c - m)
    out_ref[...] = ((ea * o0[...] + eb * o1[...] + ec * o2[...]) / (ea + eb + ec)).astype(out_ref.dtype)


def _dilated(p_dil, cos_f, sin_f):
    S = p_dil.shape[0]
    outs, lses = [], []
    for g, (window, dil) in enumerate(DIL_PAIRS):
        assert window // dil == DIL_BLK
        o, l = _banded_group(p_dil, cos_f, sin_f, g, dil)
        outs.append(o)
        lses.append(l)
    tm = min(S, 1024)
    spec = pl.BlockSpec((tm, DIL_OUT), lambda i: (i, 0))
    return pl.pallas_call(
        _dil_combine_kernel,
        grid=(S // tm,),
        in_specs=[spec] * 6,
        out_specs=spec,
        out_shape=jax.ShapeDtypeStruct((S, DIL_OUT), BF16),
        compiler_params=_cp("parallel"),
        name="dilated_combine",
    )(*outs, *lses)


def _rwkv_prep_kernel(has_vmix, *refs):
    if has_vmix:
        (p_ref, prev_ref, mu_ref, w0_ref, w2_ref, a0_ref, a2_ref, g2_ref, kk_ref, ka_ref, rk_ref, hd_ref,
         lt_ref, pick_ref, vf_ref, v0_ref, v1_ref, v2_ref,
         at_o, rt_o, bt_o, kt_o, v16_o, gc_o, bonus_o, g_o) = refs
        v_o = None
    else:
        (p_ref, prev_ref, mu_ref, w0_ref, w2_ref, a0_ref, a2_ref, g2_ref, kk_ref, ka_ref, rk_ref, hd_ref,
         lt_ref, pick_ref,
         at_o, rt_o, bt_o, kt_o, v16_o, gc_o, bonus_o, g_o, v_o) = refs
    W = RWKV_WIDTH
    p = p_ref[...]
    last = jnp.where(pl.program_id(0) == 0, 0.0, prev_ref[7:8, :])
    row = lax.broadcasted_iota(jnp.int32, p.shape, 0)
    p_prev = jnp.where(row == 0, last, pltpu.roll(p, 1, axis=0))
    xs = p + (p_prev - p) * mu_ref[...]
    r, k, v = xs[:, 0:W], xs[:, W:2 * W], xs[:, 2 * W:3 * W]
    w_low = xs[:, 3 * W:3 * W + LORA_PAD]
    a_low = xs[:, 3 * W + LORA_PAD:3 * W + 2 * LORA_PAD]
    g_low = xs[:, 3 * W + 2 * LORA_PAD:]
    log_w = _log_sigmoid(w0_ref[...] + _dot_hi(jnp.tanh(w_low), w2_ref[...])) - 0.5
    a = _sigmoid(a0_ref[...] + _dot_hi(a_low, a2_ref[...]))
    g = _dot_hi(_sigmoid(g_low), g2_ref[...])
    if has_vmix:
        mix = _sigmoid(v0_ref[...] + _dot_hi(_dot_hi(v, v1_ref[...]), v2_ref[...]))
        v = v + (vf_ref[...] - v) * mix
    kk = k * kk_ref[...]
    kk = kk / jnp.maximum(jnp.sqrt(_group_sum_lanes(kk * kk, hd_ref[...])), 1e-12)
    k = k * (1.0 + (a - 1.0) * ka_ref[...])
    lw = -jnp.exp(log_w)
    lw_parts = _split3(lw)
    cum = _group_rows(lt_ref[...], lw_parts)
    e_in = jnp.exp(cum)
    e_inv = jnp.exp(-cum)
    at_o[...] = (-kk * jnp.exp(cum - lw)).astype(at_o.dtype)
    rt_o[...] = (r * e_in).astype(rt_o.dtype)
    bt_o[...] = (kk * a * e_inv).astype(bt_o.dtype)
    kt_o[...] = (k * e_inv).astype(kt_o.dtype)
    v16_o[...] = v.astype(v16_o.dtype)
    gc_o[...] = jnp.exp(sum(_dot(pick_ref[...], part) for part in lw_parts))
    bonus_o[...] = _group_sum_lanes(r * k * rk_ref[...], hd_ref[...]) * v
    g_o[...] = g
    if v_o is not None:
        v_o[...] = v


def _pad_rows(w, rows):
    return jnp.zeros((rows, w.shape[1]), w.dtype).at[:w.shape[0]].set(w)


def _pad_rwkv_cols(t):
    W = RWKV_WIDTH
    z = jnp.zeros(t.shape[:-1] + (LORA_PAD - RWKV_DECAY_LORA,), t.dtype)
    o1, o2 = 3 * W + RWKV_DECAY_LORA, 3 * W + RWKV_DECAY_LORA + RWKV_AAA_LORA
    return jnp.concatenate([t[..., :o1], z, t[..., o1:o2], z, t[..., o2:]], axis=-1)


def _head_indicator():
    i = np.arange(SLAB) // RWKV_HEAD_SIZE
    return jnp.asarray((i[:, None] == i[None, :]).astype(np.float32), BF16)


def _rwkv_prep(p, mu, w0, w2, a0, a2, g2, k_k, k_a, r_k, v_first, v_mix):
    S = p.shape[0]
    W = RWKV_WIDTH
    C = RWKV_CHUNK
    R = min(S, 512)
    has_vmix = v_mix is not None
    row = lambda n: pl.BlockSpec((1, n), lambda i: (0, 0))
    full = lambda a: pl.BlockSpec(a.shape, lambda i: (0, 0))
    tile = pl.BlockSpec((R, W), lambda i: (i, 0))
    w2p, a2p = _pad_rows(w2, LORA_PAD), _pad_rows(a2, LORA_PAD)
    hd = _head_indicator()
    ri = np.arange(SLAB)
    same = (ri[:, None] // C) == (ri[None, :] // C)
    ltri = jnp.asarray((same & (ri[None, :] <= ri[:, None])).astype(np.float32), BF16)
    pick = jnp.asarray((np.arange(R // C)[:, None] == np.arange(R)[None, :] // C).astype(np.float32), BF16)
    args = [p, p, _pad_rwkv_cols(mu)[None, :], w0[None, :], w2p, a0[None, :], a2p, g2, k_k[None, :], k_a[None, :],
            r_k[None, :], hd, ltri, pick]
    in_specs = [pl.BlockSpec((R, RWKV_IN_PAD), lambda i: (i, 0)),
                pl.BlockSpec((8, RWKV_IN_PAD), lambda i: (jnp.maximum(i * (R // 8) - 1, 0), 0)),
                row(RWKV_IN_PAD), row(W), full(w2p), row(W), full(a2p), full(g2), row(W), row(W), row(W),
                full(hd), full(ltri), full(pick)]
    if has_vmix:
        v0, v1, v2 = v_mix
        args += [v_first, v0[None, :], v1, v2]
        in_specs += [tile, row(W), full(v1), full(v2)]
    gc_tile = pl.BlockSpec((R // C, W), lambda i: (i, 0))
    out_specs = [tile] * 5 + [gc_tile, tile, tile]
    out_shape = ([jax.ShapeDtypeStruct((S, W), BF16)] * 5
                 + [jax.ShapeDtypeStruct((S // C, W), F32)] + [jax.ShapeDtypeStruct((S, W), F32)] * 2)
    if not has_vmix:
        out_specs.append(tile)
        out_shape.append(jax.ShapeDtypeStruct((S, W), F32))
    return pl.pallas_call(
        functools.partial(_rwkv_prep_kernel, has_vmix),
        grid=(S // R,),
        in_specs=in_specs,
        out_specs=out_specs,
        out_shape=out_shape,
        compiler_params=_cp("parallel"),
        name="rwkv_prep",
    )(*args)


def _rwkv_scan_kernel(at_ref, rt_ref, bt_ref, kt_ref, v_ref, gc_ref, mask_ref, eye_ref, y_ref, st_ref):
    C = RWKV_CHUNK
    N = RWKV_HEAD_SIZE
    n_chunks = at_ref.shape[0] // C

    @pl.when(pl.program_id(0) == 0)
    def _():
        st_ref[...] = jnp.zeros_like(st_ref)

    mask = mask_ref[...]
    eye = eye_ref[...]

    HS = range(RWKV_HEADS)

    def body(c, carry):
        rows = pl.ds(pl.multiple_of(c * C, C), C)
        at, rt, bt, kt, vt = (ref[rows, :] for ref in (at_ref, rt_ref, bt_ref, kt_ref, v_ref))
        g_row = gc_ref[pl.ds(c, 1), :]
        hs = [slice(h * N, (h + 1) * N) for h in HS]
        a_t, r_t, b_t, k_t, vv = ([t[:, s] for s in hs] for t in (at, rt, bt, kt, vt))
        g_c = [g_row[:, s] for s in hs]
        m_all = [_dot_nt(jnp.concatenate([a_t[h], r_t[h]], axis=0), jnp.concatenate([b_t[h], k_t[h]], axis=0)) * mask
                 for h in HS]
        m16 = [m.astype(BF16) for m in m_all]
        t_inv = [eye + m[:C, :C] for m in m_all]
        x = [_dot(m[:C, :C], m[:C, :C]) for m in m16]
        for _ in range(5):
            xt = [_dot(x[h].astype(BF16), jnp.concatenate([x[h], t_inv[h]], axis=1).astype(BF16)) for h in HS]
            x = [v[:, :C] for v in xt]
            t_inv = [t_inv[h] + xt[h][:, C:] for h in HS]
        mv = [_dot(jnp.concatenate([m16[h][:C, C:], m16[h][C:, C:]], axis=0), vv[h]) for h in HS]
        pq16 = [_dot(t_inv[h].astype(BF16), jnp.concatenate([a_t[h], mv[h][:C].astype(BF16)], axis=1)).astype(BF16)
                for h in HS]
        p2q2 = [jnp.concatenate([r_t[h].astype(F32), mv[h][C:]], axis=1) + _dot(m16[h][C:, :C], pq16[h])
                for h in HS]
        gh = [_dot_tn((b_t[h] * g_c[h]).astype(BF16), pq16[h]) for h in HS]
        kv = [_dot_tn((k_t[h] * g_c[h]).astype(BF16), vv[h]) for h in HS]
        st16 = [st_ref[h].astype(BF16) for h in HS]
        ys = [_dot(p2q2[h][:, :N].astype(BF16), st16[h]) + p2q2[h][:, N:] for h in HS]
        for h in HS:
            st_ref[h] = _dot((gh[h][:, :N] + eye * g_c[h]).astype(BF16), st16[h]) + gh[h][:, N:] + kv[h]
        y_ref[rows, :] = jnp.concatenate(ys, axis=1)
        return carry

    lax.fori_loop(0, n_chunks, body, 0)


def _rwkv_scan(at, rt, bt, kt, v16, gc):
    S, W = at.shape
    C = RWKV_CHUNK
    R = min(S, 512)
    ti = np.arange(C)
    strict = (ti[None, :] < ti[:, None]).astype(np.float32)
    incl = (ti[None, :] <= ti[:, None]).astype(np.float32)
    mask = jnp.asarray(np.block([[strict, strict], [incl, incl]]))
    eye = jnp.eye(C, dtype=F32)
    tile = pl.BlockSpec((R, W), lambda i: (i, 0))
    const = lambda a_: pl.BlockSpec(a_.shape, lambda i: (0, 0))
    return pl.pallas_call(
        _rwkv_scan_kernel,
        grid=(S // R,),
        in_specs=[tile] * 5 + [pl.BlockSpec((R // C, W), lambda i: (i, 0)), const(mask), const(eye)],
        out_specs=tile,
        out_shape=jax.ShapeDtypeStruct((S, W), F32),
        scratch_shapes=[pltpu.VMEM((RWKV_HEADS, RWKV_HEAD_SIZE, RWKV_HEAD_SIZE), F32)],
        compiler_params=_cp("arbitrary"),
        name="rwkv_scan",
    )(at, rt, bt, kt, v16, gc, mask, eye)


def _rwkv_post_kernel(y_ref, bonus_ref, g_ref, lw_ref, lb_ref, hd_ref, o_ref):
    hd = hd_ref[...]
    inv_n = 1.0 / RWKV_HEAD_SIZE
    y = y_ref[...]
    yc = y - _group_sum_lanes(y, hd) * inv_n
    var = _group_sum_lanes(yc * yc, hd) * inv_n
    yn = yc * lax.rsqrt(var + RWKV_LNX_EPS) * lw_ref[...] + lb_ref[...]
    o_ref[...] = ((yn + bonus_ref[...]) * g_ref[...]).astype(o_ref.dtype)


def _rwkv_post(y, bonus, g, lnx_w, lnx_b):
    S, W = y.shape
    R = min(S, 512)
    tile = pl.BlockSpec((R, W), lambda i: (i, 0))
    row = pl.BlockSpec((1, W), lambda i: (0, 0))
    hd = _head_indicator()
    return pl.pallas_call(
        _rwkv_post_kernel,
        grid=(S // R,),
        in_specs=[tile] * 3 + [row] * 2 + [pl.BlockSpec((SLAB, SLAB), lambda i: (0, 0))],
        out_specs=tile,
        out_shape=jax.ShapeDtypeStruct((S, W), BF16),
        compiler_params=_cp("parallel"),
        name="rwkv_post",
    )(y, bonus, g, lnx_w[None, :], lnx_b[None, :], hd)


def _rwkv(p, mu, w0, w2, a0, a2, g2, k_k, k_a, r_k, lnx_w, lnx_b, v_first, v_mix):
    outs = _rwkv_prep(p, mu, w0, w2, a0, a2, g2, k_k, k_a, r_k, v_first, v_mix)
    at, rt, bt, kt, v16, gc, bonus, g = outs[:8]
    y = _rwkv_scan(at, rt, bt, kt, v16, gc)
    out = _rwkv_post(y, bonus, g, lnx_w, lnx_b)
    return out, (outs[8] if v_mix is None else v_first)


def _moba_prep_kernel(k_ref, v_ref, c_ref, s_ref, kr_ref, vt_ref, km_ref):
    cos_f, sin_f = c_ref[...], s_ref[...]
    parts = []
    for h in range(MOBA_HEADS):
        parts.append(_rope(k_ref[:, h * HEAD_DIM:(h + 1) * HEAD_DIM], cos_f, sin_f))
    kr = jnp.concatenate(parts, axis=1)
    kr_ref[...] = kr.astype(BF16)
    vt_ref[...] = v_ref[...].T.astype(BF16)
    km_ref[...] = jnp.broadcast_to(jnp.mean(kr, axis=0, keepdims=True), km_ref.shape)


def _moba_kernel(q_ref, c_ref, s_ref, k_ref, vt_ref, km_ref, o_ref, sel_ref):
    cb = pl.program_id(1)
    blk = MOBA_BLOCK
    nb = km_ref.shape[0]
    q = _rope(q_ref[...], c_ref[...], s_ref[...]) * (HEAD_DIM ** -0.5)
    qt = q.T
    bi = lax.broadcasted_iota(jnp.int32, (nb, blk), 0)
    gate = jnp.where(bi < cb, _dot_hi(km_ref[...], qt), -jnp.inf)
    sel = jnp.zeros((nb, blk), F32)
    for _ in range(MOBA_TOPK):
        m = jnp.max(gate, axis=0, keepdims=True)
        idx = jnp.min(jnp.where(gate == m, bi, nb), axis=0, keepdims=True)
        hit = bi == idx
        sel = jnp.where(hit, jnp.where(m > -jnp.inf, 1.0, sel), sel)
        gate = jnp.where(hit, -jnp.inf, gate)
    sel_ref[...] = sel
    qt16 = qt.astype(BF16)
    own = pl.ds(pl.multiple_of(cb * blk, blk), blk)
    s = _dot(k_ref[own, :], qt16)
    ki = lax.broadcasted_iota(jnp.int32, s.shape, 0)
    qi = lax.broadcasted_iota(jnp.int32, s.shape, 1)
    s = jnp.where(ki <= qi, s, -jnp.inf)
    m0 = jnp.max(s, axis=0, keepdims=True)
    p0 = jnp.exp(s - m0)
    l0 = jnp.sum(p0, axis=0, keepdims=True)
    acc0 = _dot(vt_ref[cb], p0.astype(BF16))

    G = MOBA_GROUP

    def body(jg, carry):
        m, l, acc = carry
        js = [jg * G + g for g in range(G)]
        ss = [jnp.where(sel_ref[pl.ds(j, 1), :] > 0.5,
                        _dot(k_ref[pl.ds(pl.multiple_of(j * blk, blk), blk), :], qt16), -jnp.inf) for j in js]
        m_new = m
        for s in ss:
            m_new = jnp.maximum(m_new, jnp.max(s, axis=0, keepdims=True))
        alpha = jnp.exp(m - m_new)
        l = alpha * l
        acc = alpha * acc
        for j, s in zip(js, ss):
            pr = jnp.exp(s - m_new)
            l = l + jnp.sum(pr, axis=0, keepdims=True)
            acc = acc + _dot(vt_ref[j], pr.astype(BF16))
        return m_new, l, acc

    m, l, acc = lax.fori_loop(0, (cb + G - 1) // G, body, (m0, l0, acc0))
    o_ref[...] = (acc / l).T.astype(o_ref.dtype)


def _moba(p, cos_f, sin_f):
    S = p.shape[0]
    blk = MOBA_BLOCK
    nb = S // blk
    HW = MOBA_HEADS * HEAD_DIM
    kr, vt, km = pl.pallas_call(
        _moba_prep_kernel,
        grid=(nb,),
        in_specs=[pl.BlockSpec((blk, HW), lambda i: (i, 1)), pl.BlockSpec((blk, HW), lambda i: (i, 2)),
                  pl.BlockSpec((blk, HEAD_DIM), lambda i: (i, 0)), pl.BlockSpec((blk, HEAD_DIM), lambda i: (i, 0))],
        out_specs=[pl.BlockSpec((blk, HW), lambda i: (i, 0)), pl.BlockSpec((None, HW, blk), lambda i: (i, 0, 0)),
                   pl.BlockSpec((8, HW), lambda i: (i, 0))],
        out_shape=[jax.ShapeDtypeStruct((S, HW), BF16), jax.ShapeDtypeStruct((nb, HW, blk), BF16),
                   jax.ShapeDtypeStruct((nb * 8, HW), F32)],
        compiler_params=_cp("parallel"),
        name="moba_prep",
    )(p, p, cos_f, sin_f)
    k_mean = km.reshape(nb, 8, HW)[:, 0]
    tab = pl.BlockSpec((blk, HEAD_DIM), lambda h, cb: (cb, 0))
    return pl.pallas_call(
        _moba_kernel,
        grid=(MOBA_HEADS, nb),
        in_specs=[pl.BlockSpec((blk, HEAD_DIM), lambda h, cb: (cb, h)), tab, tab,
                  pl.BlockSpec((S, HEAD_DIM), lambda h, cb: (0, h)),
                  pl.BlockSpec((nb, HEAD_DIM, blk), lambda h, cb: (0, h, 0)),
                  pl.BlockSpec((nb, HEAD_DIM), lambda h, cb: (0, h))],
        out_specs=pl.BlockSpec((blk, HEAD_DIM), lambda h, cb: (cb, h)),
        out_shape=jax.ShapeDtypeStruct((S, HW), BF16),
        scratch_shapes=[pltpu.VMEM((nb, blk), F32)],
        compiler_params=_cp("parallel", "arbitrary"),
        name="moba",
    )(p, cos_f, sin_f, kr, vt, k_mean)


def _split_w_in(w_in_l):
    o = np.cumsum((N_BRANCH * D_MODEL, GLA_IN, DIL_IN, RWKV_IN, MOBA_IN))
    w_gate, w_gla, w_dil, w_rwkv, w_moba = (w_in_l[:, a:b] for a, b in zip((0,) + tuple(o[:-1]), o))
    w_gla = jnp.concatenate([w_gla, jnp.zeros((D_MODEL, GLA_IN_PAD - GLA_IN), w_gla.dtype)], axis=1)
    w_rwkv = _pad_rwkv_cols(w_rwkv)
    return tuple(w.astype(BF16) for w in (w_gate, w_gla, w_dil, w_rwkv, w_moba))


def kernel(x, c, positions, w_ada, b_ada, norm1, w_in, gla_w_a2, gla_b_a2, gla_gnorm, rwkv_mu, rwkv_w0, rwkv_w2, rwkv_a0, rwkv_a2, rwkv_g2, rwkv_k_k, rwkv_k_a, rwkv_r_k, rwkv_lnx_w, rwkv_lnx_b, rwkv_v0, rwkv_v1, rwkv_v2, w_branch_a, w_branch_b, w_branch_c, w_branch_d, w_out, norm2, w_ffn_in, w_ffn_out, norm_f):
    B, S, D = x.shape
    assert B == 1 and D == D_MODEL
    xs = x.reshape(S, D)
    cos_f, sin_f = _rope_tables(positions)
    c8 = jnp.broadcast_to(c, (8, D))
    v_first = None
    for l in range(DEPTH):
        mod = _adaln(c8, w_ada, b_ada, l)[0:1]
        shift1, scale1, gate1, shift2, scale2, gate2 = jnp.split(mod, 6, axis=-1)

        h = _normmod(xs, norm1[l][None, :], scale1, shift1, BF16)
        w_gate, w_gla, w_dil, w_rwkv, w_moba = _split_w_in(w_in[l])
        p_gla = _mm(h, w_gla, 512, GLA_IN_PAD, F32, "proj_gla")
        p_dil = _mm(h, w_dil, 512, DIL_IN, F32, "proj_dil")
        p_rwkv = _mm(h, w_rwkv, 512, RWKV_IN_PAD, F32, "proj_rwkv")
        p_moba = _mm(h, w_moba, 512, MOBA_IN, F32, "proj_moba")

        o_gla = _gla(p_gla, gla_w_a2[l], gla_b_a2[l], gla_gnorm[l])
        o_dil = _dilated(p_dil, cos_f, sin_f)
        v_mix = None if l == 0 else (rwkv_v0[l - 1], rwkv_v1[l - 1], rwkv_v2[l - 1])
        o_rwkv, v_first = _rwkv(p_rwkv, rwkv_mu[l], rwkv_w0[l], rwkv_w2[l], rwkv_a0[l], rwkv_a2[l], rwkv_g2[l],
                                rwkv_k_k[l], rwkv_k_a[l], rwkv_r_k[l], rwkv_lnx_w[l], rwkv_lnx_b[l], v_first, v_mix)
        o_moba = _moba(p_moba, cos_f, sin_f)

        merged = _merge(h, w_gate, (o_gla, o_dil, o_rwkv, o_moba),
                        tuple(w[l].astype(BF16) for w in (w_branch_a, w_branch_b, w_branch_c, w_branch_d)), 1024, 512)
        xs = _mm_resid(merged, w_out[l].astype(BF16), xs, gate1, 512, 512, "out_proj_resid")

        h2 = _normmod(xs, norm2[l][None, :], scale2, shift2, BF16)
        act = _mm_swiglu(h2, w_ffn_in[l].astype(BF16), 1024, 512)
        xs = _mm_resid(act, w_ffn_out[l].astype(BF16), xs, gate2, 512, 512, "ffn_out_resid")

    zero = jnp.zeros((1, D), F32)
    return _normmod(xs, norm_f[None, :], zero, zero, F32).reshape(B, S, D)
```

```python
import functools

import numpy as np
import jax
import jax.numpy as jnp
from jax import lax
from jax.experimental import pallas as pl
from jax.experimental.pallas import tpu as pltpu

F32 = jnp.float32
BF16 = jnp.bfloat16
HI = lax.Precision.HIGHEST

D_MODEL = 2048
DEPTH = 2
HEAD_DIM = 128
ROT_DIM = HEAD_DIM // 4
ROPE_THETA = 500000.0
N_BRANCH = 4
NORM_EPS = 1e-6

GLA_HEADS = 4
GLA_DK = 64
GLA_DV = 128
GLA_RANK = 16
GLA_NORMALIZER = 16.0
GLA_SUB = 16

DIL_PAIRS = ((128, 1), (512, 4), (2048, 16))
DIL_HEADS_PER_GROUP = 2
DIL_HEADS = DIL_HEADS_PER_GROUP * len(DIL_PAIRS)
DIL_BLK = 128
DIL_TILE = DIL_BLK * max(d for _, d in DIL_PAIRS)

RWKV_HEAD_SIZE = 64
RWKV_HEADS = 8
RWKV_WIDTH = RWKV_HEADS * RWKV_HEAD_SIZE
RWKV_DECAY_LORA = 96
RWKV_AAA_LORA = 96
RWKV_MV_LORA = 64
RWKV_GATE_LORA = 256
RWKV_LNX_EPS = 64e-5
RWKV_CHUNK = 64
LORA_PAD = 128

MOBA_HEADS = 4
MOBA_BLOCK = 256
MOBA_TOPK = 3
MOBA_MASKED = -1e30
MOBA_VROWS = HEAD_DIM + 16
MOBA_GROUP = 4

FFN_HIDDEN = -(-8 * D_MODEL // (3 * 256)) * 256

GLA_SIZES = (GLA_HEADS * GLA_DK, GLA_HEADS * GLA_DK, GLA_HEADS * GLA_DV, GLA_HEADS * GLA_DV, GLA_RANK)
RWKV_SIZES = (RWKV_WIDTH, RWKV_WIDTH, RWKV_WIDTH, RWKV_DECAY_LORA, RWKV_AAA_LORA, RWKV_GATE_LORA)
GLA_IN = sum(GLA_SIZES)
DIL_IN = 3 * DIL_HEADS * HEAD_DIM
RWKV_IN = sum(RWKV_SIZES)
MOBA_IN = 3 * MOBA_HEADS * HEAD_DIM
GLA_OUT = GLA_HEADS * GLA_DV
DIL_OUT = DIL_HEADS_PER_GROUP * HEAD_DIM
RWKV_OUT = RWKV_WIDTH
MOBA_OUT = MOBA_HEADS * HEAD_DIM

GLA_IN_PAD = GLA_IN - GLA_RANK + 128
RWKV_IN_PAD = 3 * RWKV_WIDTH + 2 * LORA_PAD + RWKV_GATE_LORA

VMEM_LIMIT_BYTES = 56 * 1024 * 1024
SLAB = 128


def _cp(*sem):
    return pltpu.CompilerParams(dimension_semantics=sem, vmem_limit_bytes=VMEM_LIMIT_BYTES)


def _dot(a, b):
    return jnp.dot(a, b, preferred_element_type=F32)


def _dot_hi(a, b):
    return jnp.dot(a, b, precision=HI, preferred_element_type=F32)


def _split3(x):
    x1 = x.astype(BF16)
    r1 = x - x1.astype(F32)
    x2 = r1.astype(BF16)
    x3 = (r1 - x2.astype(F32)).astype(BF16)
    return x1, x2, x3


def _dot_sel_l(sel16, x):
    x1, x2, x3 = _split3(x)
    return _dot(sel16, x1) + _dot(sel16, x2) + _dot(sel16, x3)


def _dot_sel_r(x, sel16):
    x1, x2, x3 = _split3(x)
    return _dot(x1, sel16) + _dot(x2, sel16) + _dot(x3, sel16)


def _group_sum_lanes(x, ind16):
    n = ind16.shape[0]
    return jnp.concatenate([_dot_sel_r(x[:, i:i + n], ind16) for i in range(0, x.shape[1], n)], axis=1)


def _group_rows(sel16, parts):
    n = sel16.shape[1]
    slabs = range(0, parts[0].shape[0], n)
    return jnp.concatenate([sum(_dot(sel16, p[i:i + n]) for p in parts) for i in slabs], axis=0)


def _dot_nt(a, b, precision=None):
    return lax.dot_general(a, b, (((1,), (1,)), ((), ())), precision=precision, preferred_element_type=F32)


def _dot_tn(a, b, precision=None):
    return lax.dot_general(a, b, (((0,), (0,)), ((), ())), precision=precision, preferred_element_type=F32)


def _log_sigmoid(z):
    return jnp.minimum(z, 0.0) - jnp.log1p(jnp.exp(-jnp.abs(z)))


def _sigmoid(z):
    return 1.0 / (1.0 + jnp.exp(-z))


def _rope(x, cos_f, sin_f):
    half = ROT_DIM // 2
    lane = lax.broadcasted_iota(jnp.int32, x.shape, 1)
    partner = jnp.where(lane < half, pltpu.roll(x, HEAD_DIM - half, axis=1), pltpu.roll(x, half, axis=1))
    return x * cos_f + partner * sin_f


def _rope_table_kernel(pos_ref, inv_ref, cos_ref, sin_ref):
    ang = pos_ref[...] * inv_ref[...]
    lane = lax.broadcasted_iota(jnp.int32, ang.shape, 1)
    half = ROT_DIM // 2
    c = jnp.cos(ang)
    s = jnp.sin(ang)
    cos_ref[...] = jnp.where(lane < ROT_DIM, c, 1.0)
    sin_ref[...] = jnp.where(lane < half, -s, jnp.where(lane < ROT_DIM, s, 0.0))


def _rope_tables(positions):
    S = positions.shape[1]
    inv = ROPE_THETA ** (-jnp.arange(0, ROT_DIM, 2, dtype=F32) / ROT_DIM)
    inv_full = jnp.concatenate([inv, inv, jnp.zeros((HEAD_DIM - ROT_DIM,), F32)])[None, :]
    pos = positions.astype(F32).reshape(S, 1)
    tm = min(S, 1024)
    return pl.pallas_call(
        _rope_table_kernel,
        grid=(S // tm,),
        in_specs=[pl.BlockSpec((tm, 1), lambda i: (i, 0)), pl.BlockSpec((1, HEAD_DIM), lambda i: (0, 0))],
        out_specs=[pl.BlockSpec((tm, HEAD_DIM), lambda i: (i, 0))] * 2,
        out_shape=[jax.ShapeDtypeStruct((S, HEAD_DIM), F32)] * 2,
        compiler_params=_cp("parallel"),
        name="rope_tables",
    )(pos, inv_full)


def _ada_kernel(c_ref, w_ref, b_ref, o_ref):
    c = c_ref[...]
    o_ref[...] = _dot_hi(c * _sigmoid(c), w_ref[...]) + b_ref[...]


def _adaln(c8, w_ada, b_ada, layer):
    D = D_MODEL
    tn = 1536
    return pl.pallas_call(
        _ada_kernel,
        grid=(6 * D // tn,),
        in_specs=[pl.BlockSpec((8, D), lambda n: (0, 0)),
                  pl.BlockSpec((None, D, tn), lambda n: (layer, 0, n)),
                  pl.BlockSpec((None, 1, tn), lambda n: (layer, 0, n))],
        out_specs=pl.BlockSpec((8, tn), lambda n: (0, n)),
        out_shape=jax.ShapeDtypeStruct((8, 6 * D), F32),
        compiler_params=_cp("parallel"),
        name="adaln",
    )(c8, w_ada, b_ada.reshape(DEPTH, 1, 6 * D))


def _normmod_kernel(x_ref, g_ref, sc_ref, sh_ref, o_ref):
    x = x_ref[...]
    y = x * lax.rsqrt(jnp.mean(x * x, axis=-1, keepdims=True) + NORM_EPS) * g_ref[...]
    o_ref[...] = (y * (1.0 + sc_ref[...]) + sh_ref[...]).astype(o_ref.dtype)


def _normmod(x, gain, scale, shift, out_dtype):
    S, D = x.shape
    tm = min(S, 512)
    row = pl.BlockSpec((1, D), lambda i: (0, 0))
    return pl.pallas_call(
        _normmod_kernel,
        grid=(S // tm,),
        in_specs=[pl.BlockSpec((tm, D), lambda i: (i, 0)), row, row, row],
        out_specs=pl.BlockSpec((tm, D), lambda i: (i, 0)),
        out_shape=jax.ShapeDtypeStruct((S, D), out_dtype),
        compiler_params=_cp("parallel"),
        name="normmod",
    )(x, gain, scale, shift)


def _mm_kernel(a_ref, b_ref, o_ref):
    o_ref[...] = _dot(a_ref[...], b_ref[...]).astype(o_ref.dtype)


def _mm(a, b, tm, tn, out_dtype, name):
    M, K = a.shape
    N = b.shape[1]
    tm, tn = min(tm, M), min(tn, N)
    return pl.pallas_call(
        _mm_kernel,
        grid=(N // tn, M // tm),
        in_specs=[pl.BlockSpec((tm, K), lambda n, m: (m, 0)), pl.BlockSpec((K, tn), lambda n, m: (0, n))],
        out_specs=pl.BlockSpec((tm, tn), lambda n, m: (m, n)),
        out_shape=jax.ShapeDtypeStruct((M, N), out_dtype),
        compiler_params=_cp("parallel", "parallel"),
        name=name,
    )(a, b)


def _swiglu_kernel(a_ref, bg_ref, bu_ref, o_ref):
    a = a_ref[...]
    g = _dot(a, bg_ref[...])
    u = _dot(a, bu_ref[...])
    o_ref[...] = (g * _sigmoid(g) * u).astype(o_ref.dtype)


def _mm_swiglu(a, w, tm, tn):
    M, K = a.shape
    H = w.shape[1] // 2
    tm = min(tm, M)
    nb = H // tn
    return pl.pallas_call(
        _swiglu_kernel,
        grid=(nb, M // tm),
        in_specs=[pl.BlockSpec((tm, K), lambda n, m: (m, 0)),
                  pl.BlockSpec((K, tn), lambda n, m: (0, n)),
                  pl.BlockSpec((K, tn), lambda n, m: (0, nb + n))],
        out_specs=pl.BlockSpec((tm, tn), lambda n, m: (m, n)),
        out_shape=jax.ShapeDtypeStruct((M, H), BF16),
        compiler_params=_cp("parallel", "parallel"),
        name="ffn_in_swiglu",
    )(a, w, w)


def _resid_kernel(a_ref, b_ref, x_ref, g_ref, o_ref):
    o_ref[...] = x_ref[...] + g_ref[...] * _dot(a_ref[...], b_ref[...])


def _mm_resid(a, b, x, gate, tm, tn, name):
    M, K = a.shape
    N = b.shape[1]
    tm = min(tm, M)
    return pl.pallas_call(
        _resid_kernel,
        grid=(N // tn, M // tm),
        in_specs=[pl.BlockSpec((tm, K), lambda n, m: (m, 0)),
                  pl.BlockSpec((K, tn), lambda n, m: (0, n)),
                  pl.BlockSpec((tm, tn), lambda n, m: (m, n)),
                  pl.BlockSpec((1, tn), lambda n, m: (0, n))],
        out_specs=pl.BlockSpec((tm, tn), lambda n, m: (m, n)),
        out_shape=jax.ShapeDtypeStruct((M, N), F32),
        compiler_params=_cp("parallel", "parallel"),
        name=name,
    )(a, b, x, gate)


def _merge_kernel(h_ref, wg0, wg1, wg2, wg3, oa, ob, oc, od, wa, wb, wc, wd, o_ref):
    h = h_ref[...]
    acc = None
    for wg, o_br, w_br in ((wg0, oa, wa), (wg1, ob, wb), (wg2, oc, wc), (wg3, od, wd)):
        term = _sigmoid(_dot(h, wg[...])) * _dot(o_br[...], w_br[...])
        acc = term if acc is None else acc + term
    o_ref[...] = acc.astype(o_ref.dtype)


def _merge(h, w_gate, branches, w_branches, tm, tn):
    M, D = h.shape
    tm = min(tm, M)
    nb = D // tn
    in_specs = [pl.BlockSpec((tm, D), lambda n, m: (m, 0))]
    for i in range(N_BRANCH):
        in_specs.append(pl.BlockSpec((D, tn), lambda n, m, i=i: (0, i * nb + n)))
    for br in branches:
        in_specs.append(pl.BlockSpec((tm, br.shape[1]), lambda n, m: (m, 0)))
    for wb in w_branches:
        in_specs.append(pl.BlockSpec((wb.shape[0], tn), lambda n, m: (0, n)))
    return pl.pallas_call(
        _merge_kernel,
        grid=(nb, M // tm),
        in_specs=in_specs,
        out_specs=pl.BlockSpec((tm, tn), lambda n, m: (m, n)),
        out_shape=jax.ShapeDtypeStruct((M, D), BF16),
        compiler_params=_cp("parallel", "parallel"),
        name="gated_merge",
    )(h, w_gate, w_gate, w_gate, w_gate, *branches, *w_branches)


def _gla_kernel(q_ref, k_ref, v_ref, g_ref, al_ref, wa_ref, ba_ref, gn_ref, lt_ref, blk_ref, e_ref, sel_ref,
                o_ref, st_ref, qe_ref, ke_ref, b_ref, dec_ref, acc_ref):
    C = GLA_SUB
    R = q_ref.shape[0]

    @pl.when(pl.program_id(0) == 0)
    def _():
        st_ref[...] = jnp.zeros_like(st_ref)

    gk = _log_sigmoid(_dot_hi(al_ref[...], wa_ref[...]) + ba_ref[...]) * (1.0 / GLA_NORMALIZER)
    gk_parts = _split3(gk)
    b = _group_rows(lt_ref[...], gk_parts)
    tot = _group_rows(blk_ref[...], gk_parts)
    scale = GLA_DK ** -0.5
    qe_ref[...] = (q_ref[...] * scale * jnp.exp(b)).astype(BF16)
    ke_ref[...] = (k_ref[...] * jnp.exp(tot - b)).astype(BF16)
    b_ref[...] = b
    dec_ref[...] = jnp.exp(tot)

    e_mat = e_ref[...]
    sel = sel_ref[...]
    s_idx = lax.broadcasted_iota(jnp.int32, (C, GLA_HEADS * GLA_DK), 0)

    def body(c, carry):
        r0 = pl.multiple_of(c * C, C)
        rows = pl.ds(r0, C)
        qe = qe_ref[rows, :]
        ke = ke_ref[rows, :]
        vs = v_ref[rows, :]
        v16 = vs.astype(BF16)
        dec = dec_ref[pl.ds(r0, 1), :]
        qs = q_ref[rows, :] * scale
        ks = k_ref[rows, :]
        bs = b_ref[rows, :]
        pairs = []
        for t in range(C):
            e = jnp.exp(jnp.where(s_idx <= t, bs[t:t + 1, :] - bs, -jnp.inf))
            pairs.append(qs[t:t + 1, :] * ks * e)
        p_st = jnp.concatenate(pairs, axis=0).astype(BF16)
        sc = _dot(p_st, e_mat)
        w = (sc * jnp.concatenate([vs] * C, axis=0)).astype(BF16)
        intra = _dot(sel, w)
        outs = []
        for h in range(GLA_HEADS):
            st = st_ref[h]
            kh = slice(h * GLA_DK, (h + 1) * GLA_DK)
            vh = slice(h * GLA_DV, (h + 1) * GLA_DV)
            outs.append(_dot_nt(qe[:, kh], st.astype(BF16)))
            st_ref[h] = st * dec[:, kh] + _dot_tn(v16[:, vh], ke[:, kh])
        acc_ref[rows, :] = jnp.concatenate(outs, axis=1) + intra
        return carry

    lax.fori_loop(0, R // C, body, 0, unroll=4)

    o = acc_ref[...]
    g = g_ref[...]
    gn = gn_ref[...]
    parts = []
    for h in range(GLA_HEADS):
        oh = o[:, h * GLA_DV:(h + 1) * GLA_DV]
        parts.append(oh * lax.rsqrt(jnp.mean(oh * oh, axis=-1, keepdims=True) + NORM_EPS) * gn)
    o_ref[...] = (jnp.concatenate(parts, axis=1) * (g * _sigmoid(g))).astype(o_ref.dtype)


def _gla(p, w_a2, b_a2, g_norm):
    S = p.shape[0]
    R = min(S, 512)
    C = GLA_SUB
    HK, HV = GLA_HEADS * GLA_DK, GLA_HEADS * GLA_DV
    wa = jnp.zeros((128, HK), F32).at[:GLA_RANK].set(w_a2)
    ri = np.arange(SLAB)
    same = (ri[:, None] // C) == (ri[None, :] // C)
    ltri = jnp.asarray((same & (ri[None, :] <= ri[:, None])).astype(np.float32), BF16)
    blk = jnp.asarray(same.astype(np.float32), BF16)
    e_mat =jnp.asarray((np.arange(HK)[:, None] // GLA_DK == np.arange(HV)[None, :] // GLA_DV).astype(np.float32), BF16)
    sel = jnp.asarray((np.arange(C)[:, None] == np.arange(C * C)[None, :] // C).astype(np.float32), BF16)
    const = lambda shape: pl.BlockSpec(shape, lambda i: (0, 0))
    return pl.pallas_call(
        _gla_kernel,
        grid=(S // R,),
        in_specs=[pl.BlockSpec((R, HK), lambda i: (i, 0)),
                  pl.BlockSpec((R, HK), lambda i: (i, 1)),
                  pl.BlockSpec((R, HV), lambda i: (i, 1)),
                  pl.BlockSpec((R, HV), lambda i: (i, 2)),
                  pl.BlockSpec((R, 128), lambda i: (i, (2 * HK + 2 * HV) // 128)),
                  const((128, HK)), const((1, HK)), const((1, GLA_DV)),
                  const((SLAB, SLAB)), const((SLAB, SLAB)), const((HK, HV)), const((C, C * C))],
        out_specs=pl.BlockSpec((R, HV), lambda i: (i, 0)),
        out_shape=jax.ShapeDtypeStruct((S, HV), BF16),
        scratch_shapes=[pltpu.VMEM((GLA_HEADS, GLA_DV, GLA_DK), F32),
                        pltpu.VMEM((R, HK), BF16), pltpu.VMEM((R, HK), BF16),
                        pltpu.VMEM((R, HK), F32), pltpu.VMEM((R, HK), F32),
                        pltpu.VMEM((R, HV), F32)],
        compiler_params=_cp("arbitrary"),
        name="gla",
    )(p, p, p, p, p, wa, b_a2[None, :], g_norm[None, :], ltri, blk, e_mat, sel)


def _band_kernel(dil, q_ref, k_ref, v_ref, kp_ref, vp_ref, c_ref, s_ref, cp_ref, sp_ref, o_ref, l_ref):
    t = pl.program_id(1)
    blk = DIL_BLK
    span = blk * dil
    qi = lax.broadcasted_iota(jnp.int32, (blk, 2 * blk), 0)
    ki = lax.broadcasted_iota(jnp.int32, (blk, 2 * blk), 1)
    dist = blk + qi - ki
    band = jnp.where(dist >= 0, jnp.where(dist <= blk, 1, 0), 0)
    band_first = band * jnp.where(ki >= blk, 1, jnp.where(t > 0, 1, 0))

    def rows_of(start):
        return pl.ds(start, blk, stride=dil) if dil > 1 else pl.ds(start, blk)

    for r in range(dil):
        kp = vp = None
        for b in range(DIL_TILE // span):
            rows = rows_of(b * span + r)
            cos_c, sin_c = c_ref[rows, :], s_ref[rows, :]
            q = _rope(q_ref[rows, :], cos_c, sin_c) * (HEAD_DIM ** -0.5)
            kc = _rope(k_ref[rows, :], cos_c, sin_c).astype(BF16)
            vc = v_ref[rows, :].astype(BF16)
            if b == 0:
                prow = rows_of(r)
                kp = _rope(kp_ref[prow, :], cp_ref[prow, :], sp_ref[prow, :]).astype(BF16)
                vp = vp_ref[prow, :].astype(BF16)
            s = _dot_nt(q.astype(BF16), jnp.concatenate([kp, kc], axis=0))
            s = jnp.where((band_first if b == 0 else band) > 0, s, -jnp.inf)
            m = jnp.max(s, axis=-1, keepdims=True)
            pr = jnp.exp(s - m)
            den = jnp.sum(pr, axis=-1, keepdims=True)
            o_ref[rows, :] = _dot(pr.astype(BF16), jnp.concatenate([vp, vc], axis=0)) / den
            l_ref[rows, :] = jnp.broadcast_to(m + jnp.log(den), (blk, HEAD_DIM))
            kp, vp = kc, vc


def _banded_group(p_dil, cos_f, sin_f, group, dil):
    S = p_dil.shape[0]
    hp = DIL_HEADS_PER_GROUP
    span = DIL_BLK * dil
    assert S % DIL_TILE == 0 and DIL_TILE % span == 0
    per_tile = DIL_TILE // span

    def col(part):
        return lambda h, t: (t, part * DIL_HEADS + group * hp + h)

    def col_prev(part):
        return lambda h, t: (jnp.maximum(t * per_tile - 1, 0), part * DIL_HEADS + group * hp + h)

    cur = (DIL_TILE, HEAD_DIM)
    prev = (span, HEAD_DIM)
    tab_c = lambda h, t: (t, 0)
    tab_p = lambda h, t: (jnp.maximum(t * per_tile - 1, 0), 0)
    out_map = lambda h, t: (t, h)
    o, l = pl.pallas_call(
        functools.partial(_band_kernel, dil),
        grid=(hp, S // DIL_TILE),
        in_specs=[pl.BlockSpec(cur, col(0)), pl.BlockSpec(cur, col(1)), pl.BlockSpec(cur, col(2)),
                  pl.BlockSpec(prev, col_prev(1)), pl.BlockSpec(prev, col_prev(2)),
                  pl.BlockSpec(cur, tab_c), pl.BlockSpec(cur, tab_c), pl.BlockSpec(prev, tab_p), pl.BlockSpec(prev, tab_p)],
        out_specs=[pl.BlockSpec(cur, out_map)] * 2,
        out_shape=[jax.ShapeDtypeStruct((S, DIL_OUT), F32)] * 2,
        compiler_params=_cp("parallel", "parallel"),
        name=f"dilated_band_d{dil}",
    )(p_dil, p_dil, p_dil, p_dil, p_dil, cos_f, sin_f, cos_f, sin_f)
    return o, l


def _dil_combine_kernel(o0, o1, o2, l0, l1, l2, out_ref):
    a, b, c = l0[...], l1[...], l2[...]
    m = jnp.maximum(jnp.maximum(a, b), c)
    ea, eb, ec = jnp.exp(a - m), jnp.exp(b - m), jnp.exp(c - m)
    out_ref[...] = ((ea * o0[...] + eb * o1[...] + ec * o2[...]) / (ea + eb + ec)).astype(out_ref.dtype)


def _dilated(p_dil, cos_f, sin_f):
    S = p_dil.shape[0]
    outs, lses = [], []
    for g, (window, dil) in enumerate(DIL_PAIRS):
        assert window // dil == DIL_BLK
        o, l = _banded_group(p_dil, cos_f, sin_f, g, dil)
        outs.append(o)
        lses.append(l)
    tm = min(S, 1024)
    spec = pl.BlockSpec((tm, DIL_OUT), lambda i: (i, 0))
    return pl.pallas_call(
        _dil_combine_kernel,
        grid=(S // tm,),
        in_specs=[spec] * 6,
        out_specs=spec,
        out_shape=jax.ShapeDtypeStruct((S, DIL_OUT), BF16),
        compiler_params=_cp("parallel"),
        name="dilated_combine",
    )(*outs, *lses)


def _rwkv_prep_kernel(has_vmix, *refs):
    if has_vmix:
        (p_ref, prev_ref, mu_ref, w0_ref, w2_ref, a0_ref, a2_ref, g2_ref, kk_ref, ka_ref, rk_ref, hd_ref,
         lt_ref, pick_ref, vf_ref, v0_ref, v1_ref, v2_ref,
         at_o, rt_o, bt_o, kt_o, v16_o, gc_o, bonus_o, g_o) = refs
        v_o = None
    else:
        (p_ref, prev_ref, mu_ref, w0_ref, w2_ref, a0_ref, a2_ref, g2_ref, kk_ref, ka_ref, rk_ref, hd_ref,
         lt_ref, pick_ref,
         at_o, rt_o, bt_o, kt_o, v16_o, gc_o, bonus_o, g_o, v_o) = refs
    W = RWKV_WIDTH
    p = p_ref[...]
    last = jnp.where(pl.program_id(0) == 0, 0.0, prev_ref[7:8, :])
    row = lax.broadcasted_iota(jnp.int32, p.shape, 0)
    p_prev = jnp.where(row == 0, last, pltpu.roll(p, 1, axis=0))
    xs = p + (p_prev - p) * mu_ref[...]
    r, k, v = xs[:, 0:W], xs[:, W:2 * W], xs[:, 2 * W:3 * W]
    w_low = xs[:, 3 * W:3 * W + LORA_PAD]
    a_low = xs[:, 3 * W + LORA_PAD:3 * W + 2 * LORA_PAD]
    g_low = xs[:, 3 * W + 2 * LORA_PAD:]
    log_w = _log_sigmoid(w0_ref[...] + _dot_hi(jnp.tanh(w_low), w2_ref[...])) - 0.5
    a = _sigmoid(a0_ref[...] + _dot_hi(a_low, a2_ref[...]))
    g = _dot_hi(_sigmoid(g_low), g2_ref[...])
    if has_vmix:
        mix = _sigmoid(v0_ref[...] + _dot_hi(_dot_hi(v, v1_ref[...]), v2_ref[...]))
        v = v + (vf_ref[...] - v) * mix
    kk = k * kk_ref[...]
    kk = kk / jnp.maximum(jnp.sqrt(_group_sum_lanes(kk * kk, hd_ref[...])), 1e-12)
    k = k * (1.0 + (a - 1.0) * ka_ref[...])
    lw = -jnp.exp(log_w)
    lw_parts = _split3(lw)
    cum = _group_rows(lt_ref[...], lw_parts)
    e_in = jnp.exp(cum)
    e_inv = jnp.exp(-cum)
    at_o[...] = (-kk * jnp.exp(cum - lw)).astype(at_o.dtype)
    rt_o[...] = (r * e_in).astype(rt_o.dtype)
    bt_o[...] = (kk * a * e_inv).astype(bt_o.dtype)
    kt_o[...] = (k * e_inv).astype(kt_o.dtype)
    v16_o[...] = v.astype(v16_o.dtype)
    gc_o[...] = jnp.exp(sum(_dot(pick_ref[...], part) for part in lw_parts))
    bonus_o[...] = _group_sum_lanes(r * k * rk_ref[...], hd_ref[...]) * v
    g_o[...] = g
    if v_o is not None:
        v_o[...] = v


def _pad_rows(w, rows):
    return jnp.zeros((rows, w.shape[1]), w.dtype).at[:w.shape[0]].set(w)


def _pad_rwkv_cols(t):
    W = RWKV_WIDTH
    z = jnp.zeros(t.shape[:-1] + (LORA_PAD - RWKV_DECAY_LORA,), t.dtype)
    o1, o2 = 3 * W + RWKV_DECAY_LORA, 3 * W + RWKV_DECAY_LORA + RWKV_AAA_LORA
    return jnp.concatenate([t[..., :o1], z, t[..., o1:o2], z, t[..., o2:]], axis=-1)


def _head_indicator():
    i = np.arange(SLAB) // RWKV_HEAD_SIZE
    return jnp.asarray((i[:, None] == i[None, :]).astype(np.float32), BF16)


def _rwkv_prep(p, mu, w0, w2, a0, a2, g2, k_k, k_a, r_k, v_first, v_mix):
    S = p.shape[0]
    W = RWKV_WIDTH
    C = RWKV_CHUNK
    R = min(S, 512)
    has_vmix = v_mix is not None
    row = lambda n: pl.BlockSpec((1, n), lambda i: (0, 0))
    full = lambda a: pl.BlockSpec(a.shape, lambda i: (0, 0))
    tile = pl.BlockSpec((R, W), lambda i: (i, 0))
    w2p, a2p = _pad_rows(w2, LORA_PAD), _pad_rows(a2, LORA_PAD)
    hd = _head_indicator()
    ri = np.arange(SLAB)
    same = (ri[:, None] // C) == (ri[None, :] // C)
    ltri = jnp.asarray((same & (ri[None, :] <= ri[:, None])).astype(np.float32), BF16)
    pick = jnp.asarray((np.arange(R // C)[:, None] == np.arange(R)[None, :] // C).astype(np.float32), BF16)
    args = [p, p, _pad_rwkv_cols(mu)[None, :], w0[None, :], w2p, a0[None, :], a2p, g2, k_k[None, :], k_a[None, :],
            r_k[None, :], hd, ltri, pick]
    in_specs = [pl.BlockSpec((R, RWKV_IN_PAD), lambda i: (i, 0)),
                pl.BlockSpec((8, RWKV_IN_PAD), lambda i: (jnp.maximum(i * (R // 8) - 1, 0), 0)),
                row(RWKV_IN_PAD), row(W), full(w2p), row(W), full(a2p), full(g2), row(W), row(W), row(W),
                full(hd), full(ltri), full(pick)]
    if has_vmix:
        v0, v1, v2 = v_mix
        args += [v_first, v0[None, :], v1, v2]
        in_specs += [tile, row(W), full(v1), full(v2)]
    gc_tile = pl.BlockSpec((R // C, W), lambda i: (i, 0))
    out_specs = [tile] * 5 + [gc_tile, tile, tile]
    out_shape = ([jax.ShapeDtypeStruct((S, W), BF16)] * 5
                 + [jax.ShapeDtypeStruct((S // C, W), F32)] + [jax.ShapeDtypeStruct((S, W), F32)] * 2)
    if not has_vmix:
        out_specs.append(tile)
        out_shape.append(jax.ShapeDtypeStruct((S, W), F32))
    return pl.pallas_call(
        functools.partial(_rwkv_prep_kernel, has_vmix),
        grid=(S // R,),
        in_specs=in_specs,
        out_specs=out_specs,
        out_shape=out_shape,
        compiler_params=_cp("parallel"),
        name="rwkv_prep",
    )(*args)


def _rwkv_scan_kernel(at_ref, rt_ref, bt_ref, kt_ref, v_ref, gc_ref, mask_ref, eye_ref, y_ref, st_ref):
    C = RWKV_CHUNK
    N = RWKV_HEAD_SIZE
    n_chunks = at_ref.shape[0] // C

    @pl.when(pl.program_id(0) == 0)
    def _():
        st_ref[...] = jnp.zeros_like(st_ref)

    mask = mask_ref[...]
    eye = eye_ref[...]

    HS = range(RWKV_HEADS)

    def body(c, carry):
        rows = pl.ds(pl.multiple_of(c * C, C), C)
        at, rt, bt, kt, vt = (ref[rows, :] for ref in (at_ref, rt_ref, bt_ref, kt_ref, v_ref))
        g_row = gc_ref[pl.ds(c, 1), :]
        hs = [slice(h * N, (h + 1) * N) for h in HS]
        a_t, r_t, b_t, k_t, vv = ([t[:, s] for s in hs] for t in (at, rt, bt, kt, vt))
        g_c = [g_row[:, s] for s in hs]
        m_all = [_dot_nt(jnp.concatenate([a_t[h], r_t[h]], axis=0), jnp.concatenate([b_t[h], k_t[h]], axis=0)) * mask
                 for h in HS]
        m16 = [m.astype(BF16) for m in m_all]
        t_inv = [eye + m[:C, :C] for m in m_all]
        x = [_dot(m[:C, :C], m[:C, :C]) for m in m16]
        for _ in range(5):
            xt = [_dot(x[h].astype(BF16), jnp.concatenate([x[h], t_inv[h]], axis=1).astype(BF16)) for h in HS]
            x = [v[:, :C] for v in xt]
            t_inv = [t_inv[h] + xt[h][:, C:] for h in HS]
        mv = [_dot(jnp.concatenate([m16[h][:C, C:], m16[h][C:, C:]], axis=0), vv[h]) for h in HS]
        pq16 = [_dot(t_inv[h].astype(BF16), jnp.concatenate([a_t[h], mv[h][:C].astype(BF16)], axis=1)).astype(BF16)
                for h in HS]
        p2q2 = [jnp.concatenate([r_t[h].astype(F32), mv[h][C:]], axis=1) + _dot(m16[h][C:, :C], pq16[h])
                for h in HS]
        gh = [_dot_tn((b_t[h] * g_c[h]).astype(BF16), pq16[h]) for h in HS]
        kv = [_dot_tn((k_t[h] * g_c[h]).astype(BF16), vv[h]) for h in HS]
        st16 = [st_ref[h].astype(BF16) for h in HS]
        ys = [_dot(p2q2[h][:, :N].astype(BF16), st16[h]) + p2q2[h][:, N:] for h in HS]
        for h in HS:
            st_ref[h] = _dot((gh[h][:, :N] + eye * g_c[h]).astype(BF16), st16[h]) + gh[h][:, N:] + kv[h]
        y_ref[rows, :] = jnp.concatenate(ys, axis=1)
        return carry

    lax.fori_loop(0, n_chunks, body, 0)


def _rwkv_scan(at, rt, bt, kt, v16, gc):
    S, W = at.shape
    C = RWKV_CHUNK
    R = min(S, 512)
    ti = np.arange(C)
    strict = (ti[None, :] < ti[:, None]).astype(np.float32)
    incl = (ti[None, :] <= ti[:, None]).astype(np.float32)
    mask = jnp.asarray(np.block([[strict, strict], [incl, incl]]))
    eye = jnp.eye(C, dtype=F32)
    tile = pl.BlockSpec((R, W), lambda i: (i, 0))
    const = lambda a_: pl.BlockSpec(a_.shape, lambda i: (0, 0))
    return pl.pallas_call(
        _rwkv_scan_kernel,
        grid=(S // R,),
        in_specs=[tile] * 5 + [pl.BlockSpec((R // C, W), lambda i: (i, 0)), const(mask), const(eye)],
        out_specs=tile,
        out_shape=jax.ShapeDtypeStruct((S, W), F32),
        scratch_shapes=[pltpu.VMEM((RWKV_HEADS, RWKV_HEAD_SIZE, RWKV_HEAD_SIZE), F32)],
        compiler_params=_cp("arbitrary"),
        name="rwkv_scan",
    )(at, rt, bt, kt, v16, gc, mask, eye)


def _rwkv_post_kernel(y_ref, bonus_ref, g_ref, lw_ref, lb_ref, hd_ref, o_ref):
    hd = hd_ref[...]
    inv_n = 1.0 / RWKV_HEAD_SIZE
    y = y_ref[...]
    yc = y - _group_sum_lanes(y, hd) * inv_n
    var = _group_sum_lanes(yc * yc, hd) * inv_n
    yn = yc * lax.rsqrt(var + RWKV_LNX_EPS) * lw_ref[...] + lb_ref[...]
    o_ref[...] = ((yn + bonus_ref[...]) * g_ref[...]).astype(o_ref.dtype)


def _rwkv_post(y, bonus, g, lnx_w, lnx_b):
    S, W = y.shape
    R = min(S, 512)
    tile = pl.BlockSpec((R, W), lambda i: (i, 0))
    row = pl.BlockSpec((1, W), lambda i: (0, 0))
    hd = _head_indicator()
    return pl.pallas_call(
        _rwkv_post_kernel,
        grid=(S // R,),
        in_specs=[tile] * 3 + [row] * 2 + [pl.BlockSpec((SLAB, SLAB), lambda i: (0, 0))],
        out_specs=tile,
        out_shape=jax.ShapeDtypeStruct((S, W), BF16),
        compiler_params=_cp("parallel"),
        name="rwkv_post",
    )(y, bonus, g, lnx_w[None, :], lnx_b[None, :], hd)


def _rwkv(p, mu, w0, w2, a0, a2, g2, k_k, k_a, r_k, lnx_w, lnx_b, v_first, v_mix):
    outs = _rwkv_prep(p, mu, w0, w2, a0, a2, g2, k_k, k_a, r_k, v_first, v_mix)
    at, rt, bt, kt, v16, gc, bonus, g = outs[:8]
    y = _rwkv_scan(at, rt, bt, kt, v16, gc)
    out = _rwkv_post(y, bonus, g, lnx_w, lnx_b)
    return out, (outs[8] if v_mix is None else v_first)


def _moba_prep_kernel(k_ref, v_ref, c_ref, s_ref, kr_ref, vt_ref, km_ref):
    cos_f, sin_f = c_ref[...], s_ref[...]
    parts = []
    for h in range(MOBA_HEADS):
        parts.append(_rope(k_ref[:, h * HEAD_DIM:(h + 1) * HEAD_DIM], cos_f, sin_f))
    kr = jnp.concatenate(parts, axis=1)
    lane = lax.broadcasted_iota(jnp.int32, (k_ref.shape[0], HEAD_DIM), 1)
    tag = jnp.where(lane == pl.program_id(0), 1.0, 0.0).astype(BF16)
    kr_ref[...] = jnp.concatenate(
        [t for h in range(MOBA_HEADS) for t in (parts[h].astype(BF16), tag)], axis=1)
    vt = v_ref[...].T
    ones = jnp.where(lax.broadcasted_iota(jnp.int32, (MOBA_VROWS - HEAD_DIM, vt.shape[1]), 0) == 0, 1.0, 0.0)
    vt_ref[...] = jnp.concatenate(
        [t for h in range(MOBA_HEADS) for t in (vt[h * HEAD_DIM:(h + 1) * HEAD_DIM], ones)], axis=0).astype(BF16)
    km_ref[...] = jnp.broadcast_to(jnp.mean(kr, axis=0, keepdims=True), km_ref.shape)


def _moba_kernel(q_ref, c_ref, s_ref, k_ref, vt_ref, km_ref, o_ref):
    cb = pl.program_id(1)
    blk = MOBA_BLOCK
    nb = km_ref.shape[0]
    q = _rope(q_ref[...], c_ref[...], s_ref[...]) * (HEAD_DIM ** -0.5)
    qt = q.T
    bi = lax.broadcasted_iota(jnp.int32, (nb, blk), 0)
    gate = jnp.where(bi < cb, _dot_hi(km_ref[...], qt), -jnp.inf)
    sel = jnp.zeros((nb, blk), F32)
    for _ in range(MOBA_TOPK):
        m = jnp.max(gate, axis=0, keepdims=True)
        idx = jnp.min(jnp.where(gate == m, bi, nb), axis=0, keepdims=True)
        hit = bi == idx
        sel = jnp.where(hit, jnp.where(m > -jnp.inf, 1.0, sel), sel)
        gate = jnp.where(hit, -jnp.inf, gate)
    qt16 = qt.astype(BF16)
    bias = ((1.0 - sel) * MOBA_MASKED).astype(BF16)
    qa16 = jnp.concatenate([qt16, bias, jnp.zeros((HEAD_DIM - nb, blk), BF16)], axis=0)
    own = pl.ds(pl.multiple_of(cb * blk, blk), blk)
    s = _dot(k_ref[own, :HEAD_DIM], qt16)
    ki = lax.broadcasted_iota(jnp.int32, s.shape, 0)
    qi = lax.broadcasted_iota(jnp.int32, s.shape, 1)
    s = jnp.where(ki <= qi, s, -jnp.inf)
    m0 = jnp.max(s, axis=0, keepdims=True)
    acc0 = _dot(vt_ref[cb], jnp.exp(s - m0).astype(BF16))

    G = MOBA_GROUP

    def body(jg, carry):
        m, acc = carry
        js = [jg * G + g for g in range(G)]
        scores = lambda j: _dot(k_ref[pl.ds(pl.multiple_of(j * blk, blk), blk), :], qa16)
        s = scores(js[0])
        for g in range(G):
            s_next = scores(js[g + 1]) if g + 1 < G else None
            m_new = jnp.maximum(m, jnp.max(s, axis=0, keepdims=True))
            acc = jnp.exp(m - m_new) * acc + _dot(vt_ref[js[g]], jnp.exp(s - m_new).astype(BF16))
            m, s = m_new, s_next
        return m, acc

    m, acc = lax.fori_loop(0, (cb + G - 1) // G, body, (m0, acc0))
    o_ref[...] = (acc[:HEAD_DIM] / acc[HEAD_DIM:HEAD_DIM + 1]).T.astype(o_ref.dtype)


def _moba(p, cos_f, sin_f):
    S = p.shape[0]
    blk = MOBA_BLOCK
    nb = S // blk
    assert nb <= HEAD_DIM, "block ids are one-hot encoded in HEAD_DIM lanes"
    HW = MOBA_HEADS * HEAD_DIM
    kr, vt, km = pl.pallas_call(
        _moba_prep_kernel,
        grid=(nb,),
        in_specs=[pl.BlockSpec((blk, HW), lambda i: (i, 1)), pl.BlockSpec((blk, HW), lambda i: (i, 2)),
                  pl.BlockSpec((blk, HEAD_DIM), lambda i: (i, 0)), pl.BlockSpec((blk, HEAD_DIM), lambda i: (i, 0))],
        out_specs=[pl.BlockSpec((blk, 2 * HW), lambda i: (i, 0)),
                   pl.BlockSpec((None, MOBA_HEADS * MOBA_VROWS, blk), lambda i: (i, 0, 0)),
                   pl.BlockSpec((8, HW), lambda i: (i, 0))],
        out_shape=[jax.ShapeDtypeStruct((S, 2 * HW), BF16),
                   jax.ShapeDtypeStruct((nb, MOBA_HEADS * MOBA_VROWS, blk), BF16),
                   jax.ShapeDtypeStruct((nb * 8, HW), F32)],
        compiler_params=_cp("parallel"),
        name="moba_prep",
    )(p, p, cos_f, sin_f)
    k_mean = km.reshape(nb, 8, HW)[:, 0]
    tab = pl.BlockSpec((blk, HEAD_DIM), lambda h, cb: (cb, 0))
    return pl.pallas_call(
        _moba_kernel,
        grid=(MOBA_HEADS, nb),
        in_specs=[pl.BlockSpec((blk, HEAD_DIM), lambda h, cb: (cb, h)), tab, tab,
                  pl.BlockSpec((S, 2 * HEAD_DIM), lambda h, cb: (0, h)),
                  pl.BlockSpec((nb, MOBA_VROWS, blk), lambda h, cb: (0, h, 0)),
                  pl.BlockSpec((nb, HEAD_DIM), lambda h, cb: (0, h))],
        out_specs=pl.BlockSpec((blk, HEAD_DIM), lambda h, cb: (cb, h)),
        out_shape=jax.ShapeDtypeStruct((S, HW), BF16),
        compiler_params=_cp("parallel", "arbitrary"),
        name="moba",
    )(p, cos_f, sin_f, kr, vt, k_mean)


def _split_w_in(w_in_l):
    o = np.cumsum((N_BRANCH * D_MODEL, GLA_IN, DIL_IN, RWKV_IN, MOBA_IN))
    w_gate, w_gla, w_dil, w_rwkv, w_moba = (w_in_l[:, a:b] for a, b in zip((0,) + tuple(o[:-1]), o))
    w_gla = jnp.concatenate([w_gla, jnp.zeros((D_MODEL, GLA_IN_PAD - GLA_IN), w_gla.dtype)], axis=1)
    w_rwkv = _pad_rwkv_cols(w_rwkv)
    return tuple(w.astype(BF16) for w in (w_gate, w_gla, w_dil, w_rwkv, w_moba))


def kernel(x, c, positions, w_ada, b_ada, norm1, w_in, gla_w_a2, gla_b_a2, gla_gnorm, rwkv_mu, rwkv_w0, rwkv_w2, rwkv_a0, rwkv_a2, rwkv_g2, rwkv_k_k, rwkv_k_a, rwkv_r_k, rwkv_lnx_w, rwkv_lnx_b, rwkv_v0, rwkv_v1, rwkv_v2, w_branch_a, w_branch_b, w_branch_c, w_branch_d, w_out, norm2, w_ffn_in, w_ffn_out, norm_f):
    B, S, D = x.shape
    assert B == 1 and D == D_MODEL
    xs = x.reshape(S, D)
    cos_f, sin_f = _rope_tables(positions)
    c8 = jnp.broadcast_to(c, (8, D))
    v_first = None
    for l in range(DEPTH):
        mod = _adaln(c8, w_ada, b_ada, l)[0:1]
        shift1, scale1, gate1, shift2, scale2, gate2 = jnp.split(mod, 6, axis=-1)

        h = _normmod(xs, norm1[l][None, :], scale1, shift1, BF16)
        w_gate, w_gla, w_dil, w_rwkv, w_moba = _split_w_in(w_in[l])
        p_gla = _mm(h, w_gla, 512, GLA_IN_PAD, F32, "proj_gla")
        p_dil = _mm(h, w_dil, 512, DIL_IN, F32, "proj_dil")
        p_rwkv = _mm(h, w_rwkv, 512, RWKV_IN_PAD, F32, "proj_rwkv")
        p_moba = _mm(h, w_moba, 512, MOBA_IN, F32, "proj_moba")

        o_gla = _gla(p_gla, gla_w_a2[l], gla_b_a2[l], gla_gnorm[l])
        o_dil = _dilated(p_dil, cos_f, sin_f)
        v_mix = None if l == 0 else (rwkv_v0[l - 1], rwkv_v1[l - 1], rwkv_v2[l - 1])
        o_rwkv, v_first = _rwkv(p_rwkv, rwkv_mu[l], rwkv_w0[l], rwkv_w2[l], rwkv_a0[l], rwkv_a2[l], rwkv_g2[l],
                                rwkv_k_k[l], rwkv_k_a[l], rwkv_r_k[l], rwkv_lnx_w[l], rwkv_lnx_b[l], v_first, v_mix)
        o_moba = _moba(p_moba, cos_f, sin_f)

        merged = _merge(h, w_gate, (o_gla, o_dil, o_rwkv, o_moba),
                        tuple(w[l].astype(BF16) for w in (w_branch_a, w_branch_b, w_branch_c, w_branch_d)), 1024, 512)
        xs = _mm_resid(merged, w_out[l].astype(BF16), xs, gate1, 512, 512, "out_proj_resid")

        h2 = _normmod(xs, norm2[l][None, :], scale2, shift2, BF16)
        act = _mm_swiglu(h2, w_ffn_in[l].astype(BF16), 1024, 512)
        xs = _mm_resid(act, w_ffn_out[l].astype(BF16), xs, gate2, 512, 512, "ffn_out_resid")

    zero = jnp.zeros((1, D), F32)
    return _normmod(xs, norm_f[None, :], zero, zero, F32).reshape(B, S, D)
```

```python
import functools

import numpy as np
import jax
import jax.numpy as jnp
from jax import lax
from jax.experimental import pallas as pl
from jax.experimental.pallas import tpu as pltpu

F32 = jnp.float32
BF16 = jnp.bfloat16
HI = lax.Precision.HIGHEST

D_MODEL = 2048
DEPTH = 2
HEAD_DIM = 128
ROT_DIM = HEAD_DIM // 4
ROPE_THETA = 500000.0
N_BRANCH = 4
NORM_EPS = 1e-6

GLA_HEADS = 4
GLA_DK = 64
GLA_DV = 128
GLA_RANK = 16
GLA_NORMALIZER = 16.0
GLA_SUB = 16
GLA_SUBS_PER_TRIP = 8

DIL_PAIRS = ((128, 1), (512, 4), (2048, 16))
DIL_HEADS_PER_GROUP = 2
DIL_HEADS = DIL_HEADS_PER_GROUP * len(DIL_PAIRS)
DIL_BLK = 128
DIL_TILE = DIL_BLK * max(d for _, d in DIL_PAIRS)

RWKV_HEAD_SIZE = 64
RWKV_HEADS = 8
RWKV_WIDTH = RWKV_HEADS * RWKV_HEAD_SIZE
RWKV_DECAY_LORA = 96
RWKV_AAA_LORA = 96
RWKV_MV_LORA = 64
RWKV_GATE_LORA = 256
RWKV_LNX_EPS = 64e-5
RWKV_CHUNK = 64
RWKV_CHUNKS_PER_TRIP = 4
LORA_PAD = 128

MOBA_HEADS = 4
MOBA_BLOCK = 256
MOBA_TOPK = 3
MOBA_MASKED = -1e30
MOBA_VROWS = HEAD_DIM + 16
MOBA_HEADS_PER_STEP = 2
MOBA_GROUP = 4

FFN_HIDDEN = -(-8 * D_MODEL // (3 * 256)) * 256

GLA_SIZES = (GLA_HEADS * GLA_DK, GLA_HEADS * GLA_DK, GLA_HEADS * GLA_DV, GLA_HEADS * GLA_DV, GLA_RANK)
RWKV_SIZES = (RWKV_WIDTH, RWKV_WIDTH, RWKV_WIDTH, RWKV_DECAY_LORA, RWKV_AAA_LORA, RWKV_GATE_LORA)
GLA_IN = sum(GLA_SIZES)
DIL_IN = 3 * DIL_HEADS * HEAD_DIM
RWKV_IN = sum(RWKV_SIZES)
MOBA_IN = 3 * MOBA_HEADS * HEAD_DIM
GLA_OUT = GLA_HEADS * GLA_DV
DIL_OUT = DIL_HEADS_PER_GROUP * HEAD_DIM
RWKV_OUT = RWKV_WIDTH
MOBA_OUT = MOBA_HEADS * HEAD_DIM

GLA_IN_PAD = GLA_IN - GLA_RANK + 128
RWKV_IN_PAD = 3 * RWKV_WIDTH + 2 * LORA_PAD + RWKV_GATE_LORA

VMEM_LIMIT_BYTES = 56 * 1024 * 1024
SLAB = 128


def _cp(*sem):
    return pltpu.CompilerParams(dimension_semantics=sem, vmem_limit_bytes=VMEM_LIMIT_BYTES)


def _dot(a, b):
    return jnp.dot(a, b, preferred_element_type=F32)


def _dot_hi(a, b):
    return jnp.dot(a, b, precision=HI, preferred_element_type=F32)


def _split3(x):
    x1 = x.astype(BF16)
    r1 = x - x1.astype(F32)
    x2 = r1.astype(BF16)
    x3 = (r1 - x2.astype(F32)).astype(BF16)
    return x1, x2, x3


def _dot_sel_l(sel16, x):
    x1, x2, x3 = _split3(x)
    return _dot(sel16, x1) + _dot(sel16, x2) + _dot(sel16, x3)


def _dot_sel_r(x, sel16):
    x1, x2, x3 = _split3(x)
    return _dot(x1, sel16) + _dot(x2, sel16) + _dot(x3, sel16)


def _group_sum_lanes(x, ind16):
    n = ind16.shape[0]
    return jnp.concatenate([_dot_sel_r(x[:, i:i + n], ind16) for i in range(0, x.shape[1], n)], axis=1)


def _group_rows(sel16, parts):
    n = sel16.shape[1]
    slabs = range(0, parts[0].shape[0], n)
    return jnp.concatenate([sum(_dot(sel16, p[i:i + n]) for p in parts) for i in slabs], axis=0)


def _dot_nt(a, b, precision=None):
    return lax.dot_general(a, b, (((1,), (1,)), ((), ())), precision=precision, preferred_element_type=F32)


def _dot_tn(a, b, precision=None):
    return lax.dot_general(a, b, (((0,), (0,)), ((), ())), precision=precision, preferred_element_type=F32)


def _log_sigmoid(z):
    return jnp.minimum(z, 0.0) - jnp.log1p(jnp.exp(-jnp.abs(z)))


def _sigmoid(z):
    return 1.0 / (1.0 + jnp.exp(-z))


def _rope(x, cos_f, sin_f):
    half = ROT_DIM // 2
    lane = lax.broadcasted_iota(jnp.int32, x.shape, 1)
    partner = jnp.where(lane < half, pltpu.roll(x, HEAD_DIM - half, axis=1), pltpu.roll(x, half, axis=1))
    return x * cos_f + partner * sin_f


def _rope_table_kernel(pos_ref, inv_ref, cos_ref, sin_ref):
    ang = pos_ref[...] * inv_ref[...]
    lane = lax.broadcasted_iota(jnp.int32, ang.shape, 1)
    half = ROT_DIM // 2
    c = jnp.cos(ang)
    s = jnp.sin(ang)
    cos_ref[...] = jnp.where(lane < ROT_DIM, c, 1.0)
    sin_ref[...] = jnp.where(lane < half, -s, jnp.where(lane < ROT_DIM, s, 0.0))


def _rope_tables(positions):
    S = positions.shape[1]
    inv = ROPE_THETA ** (-jnp.arange(0, ROT_DIM, 2, dtype=F32) / ROT_DIM)
    inv_full = jnp.concatenate([inv, inv, jnp.zeros((HEAD_DIM - ROT_DIM,), F32)])[None, :]
    pos = positions.astype(F32).reshape(S, 1)
    tm = min(S, 1024)
    return pl.pallas_call(
        _rope_table_kernel,
        grid=(S // tm,),
        in_specs=[pl.BlockSpec((tm, 1), lambda i: (i, 0)), pl.BlockSpec((1, HEAD_DIM), lambda i: (0, 0))],
        out_specs=[pl.BlockSpec((tm, HEAD_DIM), lambda i: (i, 0))] * 2,
        out_shape=[jax.ShapeDtypeStruct((S, HEAD_DIM), F32)] * 2,
        compiler_params=_cp("parallel"),
        name="rope_tables",
    )(pos, inv_full)


def _ada_kernel(c_ref, w_ref, b_ref, o_ref):
    c = c_ref[...]
    o_ref[...] = _dot_hi(c * _sigmoid(c), w_ref[...]) + b_ref[...]


def _adaln(c8, w_ada, b_ada, layer):
    D = D_MODEL
    tn = 1536
    return pl.pallas_call(
        _ada_kernel,
        grid=(6 * D // tn,),
        in_specs=[pl.BlockSpec((8, D), lambda n: (0, 0)),
                  pl.BlockSpec((None, D, tn), lambda n: (layer, 0, n)),
                  pl.BlockSpec((None, 1, tn), lambda n: (layer, 0, n))],
        out_specs=pl.BlockSpec((8, tn), lambda n: (0, n)),
        out_shape=jax.ShapeDtypeStruct((8, 6 * D), F32),
        compiler_params=_cp("parallel"),
        name="adaln",
    )(c8, w_ada, b_ada.reshape(DEPTH, 1, 6 * D))


def _normmod_kernel(x_ref, g_ref, sc_ref, sh_ref, o_ref):
    x = x_ref[...]
    y = x * lax.rsqrt(jnp.mean(x * x, axis=-1, keepdims=True) + NORM_EPS) * g_ref[...]
    o_ref[...] = (y * (1.0 + sc_ref[...]) + sh_ref[...]).astype(o_ref.dtype)


def _normmod(x, gain, scale, shift, out_dtype):
    S, D = x.shape
    tm = min(S, 512)
    row = pl.BlockSpec((1, D), lambda i: (0, 0))
    return pl.pallas_call(
        _normmod_kernel,
        grid=(S // tm,),
        in_specs=[pl.BlockSpec((tm, D), lambda i: (i, 0)), row, row, row],
        out_specs=pl.BlockSpec((tm, D), lambda i: (i, 0)),
        out_shape=jax.ShapeDtypeStruct((S, D), out_dtype),
        compiler_params=_cp("parallel"),
        name="normmod",
    )(x, gain, scale, shift)


def _mm_kernel(a_ref, b_ref, o_ref):
    o_ref[...] = _dot(a_ref[...], b_ref[...]).astype(o_ref.dtype)


def _mm(a, b, tm, tn, out_dtype, name):
    M, K = a.shape
    N = b.shape[1]
    tm, tn = min(tm, M), min(tn, N)
    return pl.pallas_call(
        _mm_kernel,
        grid=(N // tn, M // tm),
        in_specs=[pl.BlockSpec((tm, K), lambda n, m: (m, 0)), pl.BlockSpec((K, tn), lambda n, m: (0, n))],
        out_specs=pl.BlockSpec((tm, tn), lambda n, m: (m, n)),
        out_shape=jax.ShapeDtypeStruct((M, N), out_dtype),
        compiler_params=_cp("parallel", "parallel"),
        name=name,
    )(a, b)


def _swiglu_kernel(a_ref, bg_ref, bu_ref, o_ref):
    a = a_ref[...]
    g = _dot(a, bg_ref[...])
    u = _dot(a, bu_ref[...])
    o_ref[...] = (g * _sigmoid(g) * u).astype(o_ref.dtype)


def _mm_swiglu(a, w, tm, tn):
    M, K = a.shape
    H = w.shape[1] // 2
    tm = min(tm, M)
    nb = H // tn
    return pl.pallas_call(
        _swiglu_kernel,
        grid=(nb, M // tm),
        in_specs=[pl.BlockSpec((tm, K), lambda n, m: (m, 0)),
                  pl.BlockSpec((K, tn), lambda n, m: (0, n)),
                  pl.BlockSpec((K, tn), lambda n, m: (0, nb + n))],
        out_specs=pl.BlockSpec((tm, tn), lambda n, m: (m, n)),
        out_shape=jax.ShapeDtypeStruct((M, H), BF16),
        compiler_params=_cp("parallel", "parallel"),
        name="ffn_in_swiglu",
    )(a, w, w)


def _resid_kernel(a_ref, b_ref, x_ref, g_ref, o_ref):
    o_ref[...] = x_ref[...] + g_ref[...] * _dot(a_ref[...], b_ref[...])


def _mm_resid(a, b, x, gate, tm, tn, name):
    M, K = a.shape
    N = b.shape[1]
    tm = min(tm, M)
    return pl.pallas_call(
        _resid_kernel,
        grid=(N // tn, M // tm),
        in_specs=[pl.BlockSpec((tm, K), lambda n, m: (m, 0)),
                  pl.BlockSpec((K, tn), lambda n, m: (0, n)),
                  pl.BlockSpec((tm, tn), lambda n, m: (m, n)),
                  pl.BlockSpec((1, tn), lambda n, m: (0, n))],
        out_specs=pl.BlockSpec((tm, tn), lambda n, m: (m, n)),
        out_shape=jax.ShapeDtypeStruct((M, N), F32),
        compiler_params=_cp("parallel", "parallel"),
        name=name,
    )(a, b, x, gate)


def _merge_kernel(h_ref, wg0, wg1, wg2, wg3, oa, ob, oc, od, wa, wb, wc, wd, o_ref):
    h = h_ref[...]
    acc = None
    for wg, o_br, w_br in ((wg0, oa, wa), (wg1, ob, wb), (wg2, oc, wc), (wg3, od, wd)):
        term = _sigmoid(_dot(h, wg[...])) * _dot(o_br[...], w_br[...])
        acc = term if acc is None else acc + term
    o_ref[...] = acc.astype(o_ref.dtype)


def _merge(h, w_gate, branches, w_branches, tm, tn):
    M, D = h.shape
    tm = min(tm, M)
    nb = D // tn
    in_specs = [pl.BlockSpec((tm, D), lambda n, m: (m, 0))]
    for i in range(N_BRANCH):
        in_specs.append(pl.BlockSpec((D, tn), lambda n, m, i=i: (0, i * nb + n)))
    for br in branches:
        in_specs.append(pl.BlockSpec((tm, br.shape[1]), lambda n, m: (m, 0)))
    for wb in w_branches:
        in_specs.append(pl.BlockSpec((wb.shape[0], tn), lambda n, m: (0, n)))
    return pl.pallas_call(
        _merge_kernel,
        grid=(nb, M // tm),
        in_specs=in_specs,
        out_specs=pl.BlockSpec((tm, tn), lambda n, m: (m, n)),
        out_shape=jax.ShapeDtypeStruct((M, D), BF16),
        compiler_params=_cp("parallel", "parallel"),
        name="gated_merge",
    )(h, w_gate, w_gate, w_gate, w_gate, *branches, *w_branches)


def _gla_kernel(q_ref, k_ref, v_ref, g_ref, al_ref, wa_ref, ba_ref, gn_ref, lt_ref, blk_ref, e_ref, sel_ref,
                o_ref, st_ref, qe_ref, ke_ref, b_ref, dec_ref, acc_ref):
    C = GLA_SUB
    R = q_ref.shape[0]

    @pl.when(pl.program_id(0) == 0)
    def _():
        st_ref[...] = jnp.zeros_like(st_ref)

    gk = _log_sigmoid(_dot_hi(al_ref[...], wa_ref[...]) + ba_ref[...]) * (1.0 / GLA_NORMALIZER)
    gk_parts = _split3(gk)
    b = _group_rows(lt_ref[...], gk_parts)
    tot = _group_rows(blk_ref[...], gk_parts)
    scale = GLA_DK ** -0.5
    qe_ref[...] = (q_ref[...] * scale * jnp.exp(b)).astype(BF16)
    ke_ref[...] = (k_ref[...] * jnp.exp(tot - b)).astype(BF16)
    b_ref[...] = b
    dec_ref[...] = jnp.exp(tot)

    e_mat = e_ref[...]
    sel = sel_ref[...]
    s_idx = lax.broadcasted_iota(jnp.int32, (C, GLA_HEADS * GLA_DK), 0)

    K = GLA_SUBS_PER_TRIP
    HS = range(GLA_HEADS)
    kh = [slice(h * GLA_DK, (h + 1) * GLA_DK) for h in HS]
    vh = [slice(h * GLA_DV, (h + 1) * GLA_DV) for h in HS]

    def body(trip, carry):
        r0 = [pl.multiple_of((trip * K + i) * C, C) for i in range(K)]
        rows = [pl.ds(r, C) for r in r0]
        vs = [v_ref[r, :] for r in rows]
        p_st = []
        for i in range(K):
            qs = q_ref[rows[i], :] * scale
            ks = k_ref[rows[i], :]
            bs = b_ref[rows[i], :]
            pairs = []
            for t in range(C):
                e = jnp.exp(jnp.where(s_idx <= t, bs[t:t + 1, :] - bs, -jnp.inf))
                pairs.append(qs[t:t + 1, :] * ks * e)
            p_st.append(jnp.concatenate(pairs, axis=0).astype(BF16))
        sc = [_dot(p, e_mat) for p in p_st]
        w = [(sc[i] * jnp.concatenate([vs[i]] * C, axis=0)).astype(BF16) for i in range(K)]
        intra = [_dot(sel, x) for x in w]
        qe = [qe_ref[r, :] for r in rows]
        ke = [ke_ref[r, :] for r in rows]
        kv = [[_dot_tn(vs[i].astype(BF16)[:, vh[h]], ke[i][:, kh[h]]) for h in HS] for i in range(K)]
        st = [st_ref[h] for h in HS]
        for i in range(K):
            dec = dec_ref[pl.ds(r0[i], 1), :]
            outs = []
            for h in HS:
                outs.append(_dot_nt(qe[i][:, kh[h]], st[h].astype(BF16)))
                st[h] = st[h] * dec[:, kh[h]] + kv[i][h]
            acc_ref[rows[i], :] = jnp.concatenate(outs, axis=1) + intra[i]
        for h in HS:
            st_ref[h] = st[h]
        return carry

    lax.fori_loop(0, R // (C * K), body, 0)

    o = acc_ref[...]
    g = g_ref[...]
    gn = gn_ref[...]
    parts = []
    for h in range(GLA_HEADS):
        oh = o[:, h * GLA_DV:(h + 1) * GLA_DV]
        parts.append(oh * lax.rsqrt(jnp.mean(oh * oh, axis=-1, keepdims=True) + NORM_EPS) * gn)
    o_ref[...] = (jnp.concatenate(parts, axis=1) * (g * _sigmoid(g))).astype(o_ref.dtype)


def _gla(p, w_a2, b_a2, g_norm):
    S = p.shape[0]
    R = min(S, 512)
    C = GLA_SUB
    HK, HV = GLA_HEADS * GLA_DK, GLA_HEADS * GLA_DV
    wa = jnp.zeros((128, HK), F32).at[:GLA_RANK].set(w_a2)
    ri = np.arange(SLAB)
    same = (ri[:, None] // C) == (ri[None, :] // C)
    ltri = jnp.asarray((same & (ri[None, :] <= ri[:, None])).astype(np.float32), BF16)
    blk = jnp.asarray(same.astype(np.float32), BF16)
    e_mat =jnp.asarray((np.arange(HK)[:, None] // GLA_DK == np.arange(HV)[None, :] // GLA_DV).astype(np.float32), BF16)
    sel = jnp.asarray((np.arange(C)[:, None] == np.arange(C * C)[None, :] // C).astype(np.float32), BF16)
    const = lambda shape: pl.BlockSpec(shape, lambda i: (0, 0))
    return pl.pallas_call(
        _gla_kernel,
        grid=(S // R,),
        in_specs=[pl.BlockSpec((R, HK), lambda i: (i, 0)),
                  pl.BlockSpec((R, HK), lambda i: (i, 1)),
                  pl.BlockSpec((R, HV), lambda i: (i, 1)),
                  pl.BlockSpec((R, HV), lambda i: (i, 2)),
                  pl.BlockSpec((R, 128), lambda i: (i, (2 * HK + 2 * HV) // 128)),
                  const((128, HK)), const((1, HK)), const((1, GLA_DV)),
                  const((SLAB, SLAB)), const((SLAB, SLAB)), const((HK, HV)), const((C, C * C))],
        out_specs=pl.BlockSpec((R, HV), lambda i: (i, 0)),
        out_shape=jax.ShapeDtypeStruct((S, HV), BF16),
        scratch_shapes=[pltpu.VMEM((GLA_HEADS, GLA_DV, GLA_DK), F32),
                        pltpu.VMEM((R, HK), BF16), pltpu.VMEM((R, HK), BF16),
                        pltpu.VMEM((R, HK), F32), pltpu.VMEM((R, HK), F32),
                        pltpu.VMEM((R, HV), F32)],
        compiler_params=_cp("arbitrary"),
        name="gla",
    )(p, p, p, p, p, wa, b_a2[None, :], g_norm[None, :], ltri, blk, e_mat, sel)


def _band_kernel(dil, q_ref, k_ref, v_ref, kp_ref, vp_ref, c_ref, s_ref, cp_ref, sp_ref, o_ref, l_ref):
    t = pl.program_id(1)
    blk = DIL_BLK
    span = blk * dil
    qi = lax.broadcasted_iota(jnp.int32, (blk, 2 * blk), 0)
    ki = lax.broadcasted_iota(jnp.int32, (blk, 2 * blk), 1)
    dist = blk + qi - ki
    band = jnp.where(dist >= 0, jnp.where(dist <= blk, 1, 0), 0)
    band_first = band * jnp.where(ki >= blk, 1, jnp.where(t > 0, 1, 0))

    def rows_of(start):
        return pl.ds(start, blk, stride=dil) if dil > 1 else pl.ds(start, blk)

    for r in range(dil):
        kp = vp = None
        for b in range(DIL_TILE // span):
            rows = rows_of(b * span + r)
            cos_c, sin_c = c_ref[rows, :], s_ref[rows, :]
            q = _rope(q_ref[rows, :], cos_c, sin_c) * (HEAD_DIM ** -0.5)
            kc = _rope(k_ref[rows, :], cos_c, sin_c).astype(BF16)
            vc = v_ref[rows, :].astype(BF16)
            if b == 0:
                prow = rows_of(r)
                kp = _rope(kp_ref[prow, :], cp_ref[prow, :], sp_ref[prow, :]).astype(BF16)
                vp = vp_ref[prow, :].astype(BF16)
            s = _dot_nt(q.astype(BF16), jnp.concatenate([kp, kc], axis=0))
            s = jnp.where((band_first if b == 0 else band) > 0, s, -jnp.inf)
            m = jnp.max(s, axis=-1, keepdims=True)
            pr = jnp.exp(s - m)
            den = jnp.sum(pr, axis=-1, keepdims=True)
            o_ref[rows, :] = _dot(pr.astype(BF16), jnp.concatenate([vp, vc], axis=0)) / den
            l_ref[rows, :] = jnp.broadcast_to(m + jnp.log(den), (blk, HEAD_DIM))
            kp, vp = kc, vc


def _banded_group(p_dil, cos_f, sin_f, group, dil):
    S = p_dil.shape[0]
    hp = DIL_HEADS_PER_GROUP
    span = DIL_BLK * dil
    assert S % DIL_TILE == 0 and DIL_TILE % span == 0
    per_tile = DIL_TILE // span

    def col(part):
        return lambda h, t: (t, part * DIL_HEADS + group * hp + h)

    def col_prev(part):
        return lambda h, t: (jnp.maximum(t * per_tile - 1, 0), part * DIL_HEADS + group * hp + h)

    cur = (DIL_TILE, HEAD_DIM)
    prev = (span, HEAD_DIM)
    tab_c = lambda h, t: (t, 0)
    tab_p = lambda h, t: (jnp.maximum(t * per_tile - 1, 0), 0)
    out_map = lambda h, t: (t, h)
    o, l = pl.pallas_call(
        functools.partial(_band_kernel, dil),
        grid=(hp, S // DIL_TILE),
        in_specs=[pl.BlockSpec(cur, col(0)), pl.BlockSpec(cur, col(1)), pl.BlockSpec(cur, col(2)),
                  pl.BlockSpec(prev, col_prev(1)), pl.BlockSpec(prev, col_prev(2)),
                  pl.BlockSpec(cur, tab_c), pl.BlockSpec(cur, tab_c), pl.BlockSpec(prev, tab_p), pl.BlockSpec(prev, tab_p)],
        out_specs=[pl.BlockSpec(cur, out_map)] * 2,
        out_shape=[jax.ShapeDtypeStruct((S, DIL_OUT), F32)] * 2,
        compiler_params=_cp("parallel", "parallel"),
        name=f"dilated_band_d{dil}",
    )(p_dil, p_dil, p_dil, p_dil, p_dil, cos_f, sin_f, cos_f, sin_f)
    return o, l


def _dil_combine_kernel(o0, o1, o2, l0, l1, l2, out_ref):
    a, b, c = l0[...], l1[...], l2[...]
    m = jnp.maximum(jnp.maximum(a, b), c)
    ea, eb, ec = jnp.exp(a - m), jnp.exp(b - m), jnp.exp(c - m)
    out_ref[...] = ((ea * o0[...] + eb * o1[...] + ec * o2[...]) / (ea + eb + ec)).astype(out_ref.dtype)


def _dilated(p_dil, cos_f, sin_f):
    S = p_dil.shape[0]
    outs, lses = [], []
    for g, (window, dil) in enumerate(DIL_PAIRS):
        assert window // dil == DIL_BLK
        o, l = _banded_group(p_dil, cos_f, sin_f, g, dil)
        outs.append(o)
        lses.append(l)
    tm = min(S, 1024)
    spec = pl.BlockSpec((tm, DIL_OUT), lambda i: (i, 0))
    return pl.pallas_call(
        _dil_combine_kernel,
        grid=(S // tm,),
        in_specs=[spec] * 6,
        out_specs=spec,
        out_shape=jax.ShapeDtypeStruct((S, DIL_OUT), BF16),
        compiler_params=_cp("parallel"),
        name="dilated_combine",
    )(*outs, *lses)


def _rwkv_prep_kernel(has_vmix, *refs):
    if has_vmix:
        (p_ref, prev_ref, mu_ref, w0_ref, w2_ref, a0_ref, a2_ref, g2_ref, kk_ref, ka_ref, rk_ref, hd_ref,
         lt_ref, pick_ref, vf_ref, v0_ref, v1_ref, v2_ref,
         at_o, rt_o, bt_o, kt_o, v16_o, gc_o, bonus_o, g_o) = refs
        v_o = None
    else:
        (p_ref, prev_ref, mu_ref, w0_ref, w2_ref, a0_ref, a2_ref, g2_ref, kk_ref, ka_ref, rk_ref, hd_ref,
         lt_ref, pick_ref,
         at_o, rt_o, bt_o, kt_o, v16_o, gc_o, bonus_o, g_o, v_o) = refs
    W = RWKV_WIDTH
    p = p_ref[...]
    last = jnp.where(pl.program_id(0) == 0, 0.0, prev_ref[7:8, :])
    row = lax.broadcasted_iota(jnp.int32, p.shape, 0)
    p_prev = jnp.where(row == 0, last, pltpu.roll(p, 1, axis=0))
    xs = p + (p_prev - p) * mu_ref[...]
    r, k, v = xs[:, 0:W], xs[:, W:2 * W], xs[:, 2 * W:3 * W]
    w_low = xs[:, 3 * W:3 * W + LORA_PAD]
    a_low = xs[:, 3 * W + LORA_PAD:3 * W + 2 * LORA_PAD]
    g_low = xs[:, 3 * W + 2 * LORA_PAD:]
    log_w = _log_sigmoid(w0_ref[...] + _dot_hi(jnp.tanh(w_low), w2_ref[...])) - 0.5
    a = _sigmoid(a0_ref[...] + _dot_hi(a_low, a2_ref[...]))
    g = _dot_hi(_sigmoid(g_low), g2_ref[...])
    if has_vmix:
        mix = _sigmoid(v0_ref[...] + _dot_hi(_dot_hi(v, v1_ref[...]), v2_ref[...]))
        v = v + (vf_ref[...] - v) * mix
    kk = k * kk_ref[...]
    kk = kk / jnp.maximum(jnp.sqrt(_group_sum_lanes(kk * kk, hd_ref[...])), 1e-12)
    k = k * (1.0 + (a - 1.0) * ka_ref[...])
    lw = -jnp.exp(log_w)
    lw_parts = _split3(lw)
    cum = _group_rows(lt_ref[...], lw_parts)
    e_in = jnp.exp(cum)
    e_inv = jnp.exp(-cum)
    at_o[...] = (-kk * jnp.exp(cum - lw)).astype(at_o.dtype)
    rt_o[...] = (r * e_in).astype(rt_o.dtype)
    bt_o[...] = (kk * a * e_inv).astype(bt_o.dtype)
    kt_o[...] = (k * e_inv).astype(kt_o.dtype)
    v16_o[...] = v.astype(v16_o.dtype)
    gc_o[...] = jnp.exp(sum(_dot(pick_ref[...], part) for part in lw_parts))
    bonus_o[...] = _group_sum_lanes(r * k * rk_ref[...], hd_ref[...]) * v
    g_o[...] = g
    if v_o is not None:
        v_o[...] = v


def _pad_rows(w, rows):
    return jnp.zeros((rows, w.shape[1]), w.dtype).at[:w.shape[0]].set(w)


def _pad_rwkv_cols(t):
    W = RWKV_WIDTH
    z = jnp.zeros(t.shape[:-1] + (LORA_PAD - RWKV_DECAY_LORA,), t.dtype)
    o1, o2 = 3 * W + RWKV_DECAY_LORA, 3 * W + RWKV_DECAY_LORA + RWKV_AAA_LORA
    return jnp.concatenate([t[..., :o1], z, t[..., o1:o2], z, t[..., o2:]], axis=-1)


def _head_indicator():
    i = np.arange(SLAB) // RWKV_HEAD_SIZE
    return jnp.asarray((i[:, None] == i[None, :]).astype(np.float32), BF16)


def _rwkv_prep(p, mu, w0, w2, a0, a2, g2, k_k, k_a, r_k, v_first, v_mix):
    S = p.shape[0]
    W = RWKV_WIDTH
    C = RWKV_CHUNK
    R = min(S, 512)
    has_vmix = v_mix is not None
    row = lambda n: pl.BlockSpec((1, n), lambda i: (0, 0))
    full = lambda a: pl.BlockSpec(a.shape, lambda i: (0, 0))
    tile = pl.BlockSpec((R, W), lambda i: (i, 0))
    w2p, a2p = _pad_rows(w2, LORA_PAD), _pad_rows(a2, LORA_PAD)
    hd = _head_indicator()
    ri = np.arange(SLAB)
    same = (ri[:, None] // C) == (ri[None, :] // C)
    ltri = jnp.asarray((same & (ri[None, :] <= ri[:, None])).astype(np.float32), BF16)
    pick = jnp.asarray((np.arange(R // C)[:, None] == np.arange(R)[None, :] // C).astype(np.float32), BF16)
    args = [p, p, _pad_rwkv_cols(mu)[None, :], w0[None, :], w2p, a0[None, :], a2p, g2, k_k[None, :], k_a[None, :],
            r_k[None, :], hd, ltri, pick]
    in_specs = [pl.BlockSpec((R, RWKV_IN_PAD), lambda i: (i, 0)),
                pl.BlockSpec((8, RWKV_IN_PAD), lambda i: (jnp.maximum(i * (R // 8) - 1, 0), 0)),
                row(RWKV_IN_PAD), row(W), full(w2p), row(W), full(a2p), full(g2), row(W), row(W), row(W),
                full(hd), full(ltri), full(pick)]
    if has_vmix:
        v0, v1, v2 = v_mix
        args += [v_first, v0[None, :], v1, v2]
        in_specs += [tile, row(W), full(v1), full(v2)]
    gc_tile = pl.BlockSpec((R // C, W), lambda i: (i, 0))
    out_specs = [tile] * 5 + [gc_tile, tile, tile]
    out_shape = ([jax.ShapeDtypeStruct((S, W), BF16)] * 5
                 + [jax.ShapeDtypeStruct((S // C, W), F32)] + [jax.ShapeDtypeStruct((S, W), F32)] * 2)
    if not has_vmix:
        out_specs.append(tile)
        out_shape.append(jax.ShapeDtypeStruct((S, W), F32))
    return pl.pallas_call(
        functools.partial(_rwkv_prep_kernel, has_vmix),
        grid=(S // R,),
        in_specs=in_specs,
        out_specs=out_specs,
        out_shape=out_shape,
        compiler_params=_cp("parallel"),
        name="rwkv_prep",
    )(*args)


def _rwkv_scan_kernel(at_ref, rt_ref, bt_ref, kt_ref, v_ref, gc_ref, mask_ref, eye_ref, y_ref, st_ref):
    C = RWKV_CHUNK
    N = RWKV_HEAD_SIZE
    n_chunks = at_ref.shape[0] // C

    @pl.when(pl.program_id(0) == 0)
    def _():
        st_ref[...] = jnp.zeros_like(st_ref)

    mask = mask_ref[...]
    eye = eye_ref[...]

    H = RWKV_HEADS
    K = RWKV_CHUNKS_PER_TRIP
    HS = range(H)
    zeros_cn = jnp.zeros((C, N), BF16)

    def body(trip, carry):
        units = []
        for kc in range(K):
            c = trip * K + kc
            rows = pl.ds(pl.multiple_of(c * C, C), C)
            tiles = [ref[rows, :] for ref in (at_ref, rt_ref, bt_ref, kt_ref, v_ref)]
            g_row = gc_ref[pl.ds(c, 1), :]
            for h in HS:
                s = slice(h * N, (h + 1) * N)
                units.append(tuple(t[:, s] for t in tiles) + (g_row[:, s],))
        U = range(len(units))
        a_t, r_t, b_t, k_t, vv, g_c = ([u[i] for u in units] for i in range(6))
        m_all = [_dot_nt(jnp.concatenate([a_t[u], r_t[u]], axis=0), jnp.concatenate([b_t[u], k_t[u]], axis=0)) * mask
                 for u in U]
        m16 = [m.astype(BF16) for m in m_all]
        m_ab16 = [m[:C, :C] for m in m16]
        t_inv = [eye + m[:C, :C] for m in m_all]
        x = [_dot(m, m) for m in m_ab16]
        for step in range(5):
            x16 = [v.astype(BF16) for v in x]
            t_inv = [t_inv[u] + _dot(x16[u], t_inv[u].astype(BF16)) for u in U]
            if step < 4:
                x = [_dot(v, v) for v in x16]
        t16 = [t.astype(BF16) for t in t_inv]
        mv = [_dot(m16[u], jnp.concatenate([zeros_cn, vv[u]], axis=0)) for u in U]
        p16 = [_dot(t16[u], a_t[u]).astype(BF16) for u in U]
        q16 = [_dot(t16[u], mv[u][:C].astype(BF16)).astype(BF16) for u in U]
        m_rb16 = [m[C:, :C] for m in m16]
        p2 = [r_t[u].astype(F32) + _dot(m_rb16[u], p16[u]) for u in U]
        q2 = [mv[u][C:] + _dot(m_rb16[u], q16[u]) for u in U]
        b_end = [(b_t[u] * g_c[u]).astype(BF16) for u in U]
        k_end = [(k_t[u] * g_c[u]).astype(BF16) for u in U]
        g_mat = [_dot_tn(b_end[u], p16[u]) + eye * g_c[u] for u in U]
        h_mat = [_dot_tn(jnp.concatenate([b_end[u], k_end[u]], axis=0), jnp.concatenate([q16[u], vv[u]], axis=0))
                 for u in U]
        pg16 = [jnp.concatenate([p2[u], g_mat[u]], axis=0).astype(BF16) for u in U]
        st = [st_ref[h] for h in HS]
        for kc in range(K):
            ys = []
            for h in HS:
                u = kc * H + h
                ys_st = _dot(pg16[u], st[h].astype(BF16))
                ys.append(ys_st[:C] + q2[u])
                st[h] = ys_st[C:] + h_mat[u]
            rows = pl.ds(pl.multiple_of((trip * K + kc) * C, C), C)
            y_ref[rows, :] = jnp.concatenate(ys, axis=1)
        for h in HS:
            st_ref[h] = st[h]
        return carry

    lax.fori_loop(0, n_chunks // K, body, 0)


def _rwkv_scan(at, rt, bt, kt, v16, gc):
    S, W = at.shape
    C = RWKV_CHUNK
    R = min(S, 512)
    ti = np.arange(C)
    strict = (ti[None, :] < ti[:, None]).astype(np.float32)
    incl = (ti[None, :] <= ti[:, None]).astype(np.float32)
    mask = jnp.asarray(np.block([[strict, strict], [incl, incl]]))
    eye = jnp.eye(C, dtype=F32)
    tile = pl.BlockSpec((R, W), lambda i: (i, 0))
    const = lambda a_: pl.BlockSpec(a_.shape, lambda i: (0, 0))
    return pl.pallas_call(
        _rwkv_scan_kernel,
        grid=(S // R,),
        in_specs=[tile] * 5 + [pl.BlockSpec((R // C, W), lambda i: (i, 0)), const(mask), const(eye)],
        out_specs=tile,
        out_shape=jax.ShapeDtypeStruct((S, W), F32),
        scratch_shapes=[pltpu.VMEM((RWKV_HEADS, RWKV_HEAD_SIZE, RWKV_HEAD_SIZE), F32)],
        compiler_params=_cp("arbitrary"),
        name="rwkv_scan",
    )(at, rt, bt, kt, v16, gc, mask, eye)


def _rwkv_post_kernel(y_ref, bonus_ref, g_ref, lw_ref, lb_ref, hd_ref, o_ref):
    hd = hd_ref[...]
    inv_n = 1.0 / RWKV_HEAD_SIZE
    y = y_ref[...]
    yc = y - _group_sum_lanes(y, hd) * inv_n
    var = _group_sum_lanes(yc * yc, hd) * inv_n
    yn = yc * lax.rsqrt(var + RWKV_LNX_EPS) * lw_ref[...] + lb_ref[...]
    o_ref[...] = ((yn + bonus_ref[...]) * g_ref[...]).astype(o_ref.dtype)


def _rwkv_post(y, bonus, g, lnx_w, lnx_b):
    S, W = y.shape
    R = min(S, 512)
    tile = pl.BlockSpec((R, W), lambda i: (i, 0))
    row = pl.BlockSpec((1, W), lambda i: (0, 0))
    hd = _head_indicator()
    return pl.pallas_call(
        _rwkv_post_kernel,
        grid=(S // R,),
        in_specs=[tile] * 3 + [row] * 2 + [pl.BlockSpec((SLAB, SLAB), lambda i: (0, 0))],
        out_specs=tile,
        out_shape=jax.ShapeDtypeStruct((S, W), BF16),
        compiler_params=_cp("parallel"),
        name="rwkv_post",
    )(y, bonus, g, lnx_w[None, :], lnx_b[None, :], hd)


def _rwkv(p, mu, w0, w2, a0, a2, g2, k_k, k_a, r_k, lnx_w, lnx_b, v_first, v_mix):
    outs = _rwkv_prep(p, mu, w0, w2, a0, a2, g2, k_k, k_a, r_k, v_first, v_mix)
    at, rt, bt, kt, v16, gc, bonus, g = outs[:8]
    y = _rwkv_scan(at, rt, bt, kt, v16, gc)
    out = _rwkv_post(y, bonus, g, lnx_w, lnx_b)
    return out, (outs[8] if v_mix is None else v_first)


def _moba_prep_kernel(k_ref, v_ref, c_ref, s_ref, kr_ref, vt_ref, km_ref):
    cos_f, sin_f = c_ref[...], s_ref[...]
    parts = []
    for h in range(MOBA_HEADS):
        parts.append(_rope(k_ref[:, h * HEAD_DIM:(h + 1) * HEAD_DIM], cos_f, sin_f))
    kr = jnp.concatenate(parts, axis=1)
    lane = lax.broadcasted_iota(jnp.int32, (k_ref.shape[0], HEAD_DIM), 1)
    tag = jnp.where(lane == pl.program_id(0), 1.0, 0.0).astype(BF16)
    kr_ref[...] = jnp.concatenate(
        [t for h in range(MOBA_HEADS) for t in (parts[h].astype(BF16), tag)], axis=1)
    vt = v_ref[...].T
    ones = jnp.where(lax.broadcasted_iota(jnp.int32, (MOBA_VROWS - HEAD_DIM, vt.shape[1]), 0) == 0, 1.0, 0.0)
    vt_ref[...] = jnp.concatenate(
        [t for h in range(MOBA_HEADS) for t in (vt[h * HEAD_DIM:(h + 1) * HEAD_DIM], ones)], axis=0).astype(BF16)
    km_ref[...] = jnp.broadcast_to(jnp.mean(kr, axis=0, keepdims=True), km_ref.shape)


def _moba_kernel(q_ref, c_ref, s_ref, k_ref, vt_ref, km_ref, o_ref):
    cb = pl.program_id(1)
    blk = MOBA_BLOCK
    nb = km_ref.shape[0]
    HP = range(MOBA_HEADS_PER_STEP)
    kcol = [slice(h * 2 * HEAD_DIM, (h + 1) * 2 * HEAD_DIM) for h in HP]
    vrow = [slice(h * MOBA_VROWS, (h + 1) * MOBA_VROWS) for h in HP]
    cos_f, sin_f = c_ref[...], s_ref[...]
    own = pl.ds(pl.multiple_of(cb * blk, blk), blk)
    bi = lax.broadcasted_iota(jnp.int32, (nb, blk), 0)
    ki = lax.broadcasted_iota(jnp.int32, (blk, blk), 0)
    qi = lax.broadcasted_iota(jnp.int32, (blk, blk), 1)
    qa16, m0, acc0 = [], [], []
    for h in HP:
        q = _rope(q_ref[:, h * HEAD_DIM:(h + 1) * HEAD_DIM], cos_f, sin_f) * (HEAD_DIM ** -0.5)
        qt = q.T
        gate = jnp.where(bi < cb, _dot_hi(km_ref[:, h * HEAD_DIM:(h + 1) * HEAD_DIM], qt), -jnp.inf)
        sel = jnp.zeros((nb, blk), F32)
        for _ in range(MOBA_TOPK):
            m = jnp.max(gate, axis=0, keepdims=True)
            idx = jnp.min(jnp.where(gate == m, bi, nb), axis=0, keepdims=True)
            hit = bi == idx
            sel = jnp.where(hit, jnp.where(m > -jnp.inf, 1.0, sel), sel)
            gate = jnp.where(hit, -jnp.inf, gate)
        qt16 = qt.astype(BF16)
        bias = ((1.0 - sel) * MOBA_MASKED).astype(BF16)
        qa16.append(jnp.concatenate([qt16, bias, jnp.zeros((HEAD_DIM - nb, blk), BF16)], axis=0))
        s = _dot(k_ref[own, h * 2 * HEAD_DIM:h * 2 * HEAD_DIM + HEAD_DIM], qt16)
        s = jnp.where(ki <= qi, s, -jnp.inf)
        m0.append(jnp.max(s, axis=0, keepdims=True))
        acc0.append(_dot(vt_ref[cb, vrow[h], :], jnp.exp(s - m0[h]).astype(BF16)))

    G = MOBA_GROUP

    def body(jg, carry):
        m, acc = list(carry[0]), list(carry[1])
        js = [jg * G + g for g in range(G)]

        def scores(j):
            rows = pl.ds(pl.multiple_of(j * blk, blk), blk)
            return [_dot(k_ref[rows, kcol[h]], qa16[h]) for h in HP]

        s = scores(js[0])
        for g in range(G):
            s_next = scores(js[g + 1]) if g + 1 < G else None
            m_new = [jnp.maximum(m[h], jnp.max(s[h], axis=0, keepdims=True)) for h in HP]
            pr = [jnp.exp(s[h] - m_new[h]).astype(BF16) for h in HP]
            acc = [jnp.exp(m[h] - m_new[h]) * acc[h] + _dot(vt_ref[js[g], vrow[h], :], pr[h]) for h in HP]
            m, s = m_new, s_next
        return tuple(m), tuple(acc)

    m, acc = lax.fori_loop(0, (cb + G - 1) // G, body, (tuple(m0), tuple(acc0)))
    o_ref[...] = jnp.concatenate(
        [(acc[h][:HEAD_DIM] / acc[h][HEAD_DIM:HEAD_DIM + 1]).T for h in HP], axis=1).astype(o_ref.dtype)


def _moba(p, cos_f, sin_f):
    S = p.shape[0]
    blk = MOBA_BLOCK
    nb = S // blk
    assert nb <= HEAD_DIM, "block ids are one-hot encoded in HEAD_DIM lanes"
    HW = MOBA_HEADS * HEAD_DIM
    kr, vt, km = pl.pallas_call(
        _moba_prep_kernel,
        grid=(nb,),
        in_specs=[pl.BlockSpec((blk, HW), lambda i: (i, 1)), pl.BlockSpec((blk, HW), lambda i: (i, 2)),
                  pl.BlockSpec((blk, HEAD_DIM), lambda i: (i, 0)), pl.BlockSpec((blk, HEAD_DIM), lambda i: (i, 0))],
        out_specs=[pl.BlockSpec((blk, 2 * HW), lambda i: (i, 0)),
                   pl.BlockSpec((None, MOBA_HEADS * MOBA_VROWS, blk), lambda i: (i, 0, 0)),
                   pl.BlockSpec((8, HW), lambda i: (i, 0))],
        out_shape=[jax.ShapeDtypeStruct((S, 2 * HW), BF16),
                   jax.ShapeDtypeStruct((nb, MOBA_HEADS * MOBA_VROWS, blk), BF16),
                   jax.ShapeDtypeStruct((nb * 8, HW), F32)],
        compiler_params=_cp("parallel"),
        name="moba_prep",
    )(p, p, cos_f, sin_f)
    k_mean = km.reshape(nb, 8, HW)[:, 0]
    tab = pl.BlockSpec((blk, HEAD_DIM), lambda h, cb: (cb, 0))
    P = MOBA_HEADS_PER_STEP
    once = pl.Buffered(1)
    return pl.pallas_call(
        _moba_kernel,
        grid=(MOBA_HEADS // P, nb),
        in_specs=[pl.BlockSpec((blk, P * HEAD_DIM), lambda h, cb: (cb, h)), tab, tab,
                  pl.BlockSpec((S, P * 2 * HEAD_DIM), lambda h, cb: (0, h), pipeline_mode=once),
                  pl.BlockSpec((nb, P * MOBA_VROWS, blk), lambda h, cb: (0, h, 0), pipeline_mode=once),
                  pl.BlockSpec((nb, P * HEAD_DIM), lambda h, cb: (0, h))],
        out_specs=pl.BlockSpec((blk, P * HEAD_DIM), lambda h, cb: (cb, h)),
        out_shape=jax.ShapeDtypeStruct((S, HW), BF16),
        compiler_params=_cp("parallel", "arbitrary"),
        name="moba",
    )(p, cos_f, sin_f, kr, vt, k_mean)


def _split_w_in(w_in_l):
    o = np.cumsum((N_BRANCH * D_MODEL, GLA_IN, DIL_IN, RWKV_IN, MOBA_IN))
    w_gate, w_gla, w_dil, w_rwkv, w_moba = (w_in_l[:, a:b] for a, b in zip((0,) + tuple(o[:-1]), o))
    w_gla = jnp.concatenate([w_gla, jnp.zeros((D_MODEL, GLA_IN_PAD - GLA_IN), w_gla.dtype)], axis=1)
    w_rwkv = _pad_rwkv_cols(w_rwkv)
    return tuple(w.astype(BF16) for w in (w_gate, w_gla, w_dil, w_rwkv, w_moba))


def kernel(x, c, positions, w_ada, b_ada, norm1, w_in, gla_w_a2, gla_b_a2, gla_gnorm, rwkv_mu, rwkv_w0, rwkv_w2, rwkv_a0, rwkv_a2, rwkv_g2, rwkv_k_k, rwkv_k_a, rwkv_r_k, rwkv_lnx_w, rwkv_lnx_b, rwkv_v0, rwkv_v1, rwkv_v2, w_branch_a, w_branch_b, w_branch_c, w_branch_d, w_out, norm2, w_ffn_in, w_ffn_out, norm_f):
    B, S, D = x.shape
    assert B == 1 and D == D_MODEL
    xs = x.reshape(S, D)
    cos_f, sin_f = _rope_tables(positions)
    c8 = jnp.broadcast_to(c, (8, D))
    v_first = None
    for l in range(DEPTH):
        mod = _adaln(c8, w_ada, b_ada, l)[0:1]
        shift1, scale1, gate1, shift2, scale2, gate2 = jnp.split(mod, 6, axis=-1)

        h = _normmod(xs, norm1[l][None, :], scale1, shift1, BF16)
        w_gate, w_gla, w_dil, w_rwkv, w_moba = _split_w_in(w_in[l])
        p_gla = _mm(h, w_gla, 512, GLA_IN_PAD, F32, "proj_gla")
        p_dil = _mm(h, w_dil, 512, DIL_IN, F32, "proj_dil")
        p_rwkv = _mm(h, w_rwkv, 512, RWKV_IN_PAD, F32, "proj_rwkv")
        p_moba = _mm(h, w_moba, 512, MOBA_IN, F32, "proj_moba")

        o_gla = _gla(p_gla, gla_w_a2[l], gla_b_a2[l], gla_gnorm[l])
        o_dil = _dilated(p_dil, cos_f, sin_f)
        v_mix = None if l == 0 else (rwkv_v0[l - 1], rwkv_v1[l - 1], rwkv_v2[l - 1])
        o_rwkv, v_first = _rwkv(p_rwkv, rwkv_mu[l], rwkv_w0[l], rwkv_w2[l], rwkv_a0[l], rwkv_a2[l], rwkv_g2[l],
                                rwkv_k_k[l], rwkv_k_a[l], rwkv_r_k[l], rwkv_lnx_w[l], rwkv_lnx_b[l], v_first, v_mix)
        o_moba = _moba(p_moba, cos_f, sin_f)

        merged = _merge(h, w_gate, (o_gla, o_dil, o_rwkv, o_moba),
                        tuple(w[l].astype(BF16) for w in (w_branch_a, w_branch_b, w_branch_c, w_branch_d)), 1024, 512)
        xs = _mm_resid(merged, w_out[l].astype(BF16), xs, gate1, 512, 512, "out_proj_resid")

        h2 = _normmod(xs, norm2[l][None, :], scale2, shift2, BF16)
        act = _mm_swiglu(h2, w_ffn_in[l].astype(BF16), 1024, 512)
        xs = _mm_resid(act, w_ffn_out[l].astype(BF16), xs, gate2, 512, 512, "ffn_out_resid")

    zero = jnp.zeros((1, D), F32)
    return _normmod(xs, norm_f[None, :], zero, zero, F32).reshape(B, S, D)
```

```python
import functools

import numpy as np
import jax
import jax.numpy as jnp
from jax import lax
from jax.experimental import pallas as pl
from jax.experimental.pallas import tpu as pltpu

F32 = jnp.float32
BF16 = jnp.bfloat16
HI = lax.Precision.HIGHEST

D_MODEL = 2048
DEPTH = 2
HEAD_DIM = 128
ROT_DIM = HEAD_DIM // 4
ROPE_THETA = 500000.0
N_BRANCH = 4
NORM_EPS = 1e-6

GLA_HEADS = 4
GLA_DK = 64
GLA_DV = 128
GLA_RANK = 16
GLA_NORMALIZER = 16.0
GLA_SUB = 16
GLA_SUBS_PER_TRIP = 8

DIL_PAIRS = ((128, 1), (512, 4), (2048, 16))
DIL_HEADS_PER_GROUP = 2
DIL_HEADS = DIL_HEADS_PER_GROUP * len(DIL_PAIRS)
DIL_BLK = 128
DIL_TILE = DIL_BLK * max(d for _, d in DIL_PAIRS)

RWKV_HEAD_SIZE = 64
RWKV_HEADS = 8
RWKV_WIDTH = RWKV_HEADS * RWKV_HEAD_SIZE
RWKV_DECAY_LORA = 96
RWKV_AAA_LORA = 96
RWKV_MV_LORA = 64
RWKV_GATE_LORA = 256
RWKV_LNX_EPS = 64e-5
RWKV_CHUNK = 64
RWKV_CHUNKS_PER_TRIP = 4
LORA_PAD = 128

MOBA_HEADS = 4
MOBA_BLOCK = 256
MOBA_TOPK = 3
MOBA_MASKED = -1e30
MOBA_VROWS = HEAD_DIM + 16
MOBA_HEADS_PER_STEP = 2
MOBA_GROUP = 8

FFN_HIDDEN = -(-8 * D_MODEL // (3 * 256)) * 256

GLA_SIZES = (GLA_HEADS * GLA_DK, GLA_HEADS * GLA_DK, GLA_HEADS * GLA_DV, GLA_HEADS * GLA_DV, GLA_RANK)
RWKV_SIZES = (RWKV_WIDTH, RWKV_WIDTH, RWKV_WIDTH, RWKV_DECAY_LORA, RWKV_AAA_LORA, RWKV_GATE_LORA)
GLA_IN = sum(GLA_SIZES)
DIL_IN = 3 * DIL_HEADS * HEAD_DIM
RWKV_IN = sum(RWKV_SIZES)
MOBA_IN = 3 * MOBA_HEADS * HEAD_DIM
GLA_OUT = GLA_HEADS * GLA_DV
DIL_OUT = DIL_HEADS_PER_GROUP * HEAD_DIM
RWKV_OUT = RWKV_WIDTH
MOBA_OUT = MOBA_HEADS * HEAD_DIM

GLA_IN_PAD = GLA_IN - GLA_RANK + 128
RWKV_IN_PAD = 3 * RWKV_WIDTH + 2 * LORA_PAD + RWKV_GATE_LORA

VMEM_LIMIT_BYTES = 56 * 1024 * 1024
SLAB = 128


def _cp(*sem):
    return pltpu.CompilerParams(dimension_semantics=sem, vmem_limit_bytes=VMEM_LIMIT_BYTES)


def _dot(a, b):
    return jnp.dot(a, b, preferred_element_type=F32)


def _dot_hi(a, b):
    return jnp.dot(a, b, precision=HI, preferred_element_type=F32)


def _split3(x):
    x1 = x.astype(BF16)
    r1 = x - x1.astype(F32)
    x2 = r1.astype(BF16)
    x3 = (r1 - x2.astype(F32)).astype(BF16)
    return x1, x2, x3


def _dot_sel_l(sel16, x):
    x1, x2, x3 = _split3(x)
    return _dot(sel16, x1) + _dot(sel16, x2) + _dot(sel16, x3)


def _dot_sel_r(x, sel16):
    x1, x2, x3 = _split3(x)
    return _dot(x1, sel16) + _dot(x2, sel16) + _dot(x3, sel16)


def _group_sum_lanes(x, ind16):
    n = ind16.shape[0]
    return jnp.concatenate([_dot_sel_r(x[:, i:i + n], ind16) for i in range(0, x.shape[1], n)], axis=1)


def _group_rows(sel16, parts):
    n = sel16.shape[1]
    slabs = range(0, parts[0].shape[0], n)
    return jnp.concatenate([sum(_dot(sel16, p[i:i + n]) for p in parts) for i in slabs], axis=0)


def _dot_nt(a, b, precision=None):
    return lax.dot_general(a, b, (((1,), (1,)), ((), ())), precision=precision, preferred_element_type=F32)


def _dot_tn(a, b, precision=None):
    return lax.dot_general(a, b, (((0,), (0,)), ((), ())), precision=precision, preferred_element_type=F32)


def _log_sigmoid(z):
    return jnp.minimum(z, 0.0) - jnp.log1p(jnp.exp(-jnp.abs(z)))


def _sigmoid(z):
    return 1.0 / (1.0 + jnp.exp(-z))


def _rope(x, cos_f, sin_f):
    half = ROT_DIM // 2
    lane = lax.broadcasted_iota(jnp.int32, x.shape, 1)
    partner = jnp.where(lane < half, pltpu.roll(x, HEAD_DIM - half, axis=1), pltpu.roll(x, half, axis=1))
    return x * cos_f + partner * sin_f


def _rope_table_kernel(pos_ref, inv_ref, cos_ref, sin_ref):
    ang = pos_ref[...] * inv_ref[...]
    lane = lax.broadcasted_iota(jnp.int32, ang.shape, 1)
    half = ROT_DIM // 2
    c = jnp.cos(ang)
    s = jnp.sin(ang)
    cos_ref[...] = jnp.where(lane < ROT_DIM, c, 1.0)
    sin_ref[...] = jnp.where(lane < half, -s, jnp.where(lane < ROT_DIM, s, 0.0))


def _rope_tables(positions):
    S = positions.shape[1]
    inv = ROPE_THETA ** (-jnp.arange(0, ROT_DIM, 2, dtype=F32) / ROT_DIM)
    inv_full = jnp.concatenate([inv, inv, jnp.zeros((HEAD_DIM - ROT_DIM,), F32)])[None, :]
    pos = positions.astype(F32).reshape(S, 1)
    tm = min(S, 1024)
    return pl.pallas_call(
        _rope_table_kernel,
        grid=(S // tm,),
        in_specs=[pl.BlockSpec((tm, 1), lambda i: (i, 0)), pl.BlockSpec((1, HEAD_DIM), lambda i: (0, 0))],
        out_specs=[pl.BlockSpec((tm, HEAD_DIM), lambda i: (i, 0))] * 2,
        out_shape=[jax.ShapeDtypeStruct((S, HEAD_DIM), F32)] * 2,
        compiler_params=_cp("parallel"),
        name="rope_tables",
    )(pos, inv_full)


def _ada_kernel(c_ref, w_ref, b_ref, o_ref):
    c = c_ref[...]
    w16 = w_ref[...].astype(BF16)
    o_ref[...] = sum(_dot(part, w16) for part in _split3(c * _sigmoid(c))) + b_ref[...]


def _adaln(c8, w_ada, b_ada, layer):
    D = D_MODEL
    tn = 1536
    return pl.pallas_call(
        _ada_kernel,
        grid=(6 * D // tn,),
        in_specs=[pl.BlockSpec((8, D), lambda n: (0, 0)),
                  pl.BlockSpec((None, D, tn), lambda n: (layer, 0, n)),
                  pl.BlockSpec((None, 1, tn), lambda n: (layer, 0, n))],
        out_specs=pl.BlockSpec((8, tn), lambda n: (0, n)),
        out_shape=jax.ShapeDtypeStruct((8, 6 * D), F32),
        compiler_params=_cp("parallel"),
        name="adaln",
    )(c8, w_ada, b_ada.reshape(DEPTH, 1, 6 * D))


def _normmod_kernel(x_ref, g_ref, sc_ref, sh_ref, o_ref):
    x = x_ref[...]
    y = x * lax.rsqrt(jnp.mean(x * x, axis=-1, keepdims=True) + NORM_EPS) * g_ref[...]
    o_ref[...] = (y * (1.0 + sc_ref[...]) + sh_ref[...]).astype(o_ref.dtype)


def _normmod(x, gain, scale, shift, out_dtype):
    S, D = x.shape
    tm = min(S, 512)
    row = pl.BlockSpec((1, D), lambda i: (0, 0))
    return pl.pallas_call(
        _normmod_kernel,
        grid=(S // tm,),
        in_specs=[pl.BlockSpec((tm, D), lambda i: (i, 0)), row, row, row],
        out_specs=pl.BlockSpec((tm, D), lambda i: (i, 0)),
        out_shape=jax.ShapeDtypeStruct((S, D), out_dtype),
        compiler_params=_cp("parallel"),
        name="normmod",
    )(x, gain, scale, shift)


def _mm_kernel(a_ref, b_ref, o_ref):
    o_ref[...] = _dot(a_ref[...], b_ref[...]).astype(o_ref.dtype)


def _mm(a, b, tm, tn, out_dtype, name):
    M, K = a.shape
    N = b.shape[1]
    tm, tn = min(tm, M), min(tn, N)
    return pl.pallas_call(
        _mm_kernel,
        grid=(N // tn, M // tm),
        in_specs=[pl.BlockSpec((tm, K), lambda n, m: (m, 0)), pl.BlockSpec((K, tn), lambda n, m: (0, n))],
        out_specs=pl.BlockSpec((tm, tn), lambda n, m: (m, n)),
        out_shape=jax.ShapeDtypeStruct((M, N), out_dtype),
        compiler_params=_cp("parallel", "parallel"),
        name=name,
    )(a, b)


def _norm_swiglu_kernel(x_ref, gn_ref, sc_ref, sh_ref, bg_ref, bu_ref, o_ref, h_ref):
    @pl.when(pl.program_id(1) == 0)
    def _():
        x = x_ref[...]
        y = x * lax.rsqrt(jnp.mean(x * x, axis=-1, keepdims=True) + NORM_EPS) * gn_ref[...]
        h_ref[...] = (y * (1.0 + sc_ref[...]) + sh_ref[...]).astype(h_ref.dtype)

    a = h_ref[...]
    g = _dot(a, bg_ref[...])
    u = _dot(a, bu_ref[...])
    o_ref[...] = (g * _sigmoid(g) * u).astype(o_ref.dtype)


def _norm_mm_swiglu(x, gain, scale, shift, w, tm, tn):
    M, K = x.shape
    H = w.shape[1] // 2
    tm = min(tm, M)
    nb = H // tn
    row = pl.BlockSpec((1, K), lambda m, n: (0, 0))
    return pl.pallas_call(
        _norm_swiglu_kernel,
        grid=(M // tm, nb),
        in_specs=[pl.BlockSpec((tm, K), lambda m, n: (m, 0)), row, row, row,
                  pl.BlockSpec((K, tn), lambda m, n: (0, n)),
                  pl.BlockSpec((K, tn), lambda m, n: (0, nb + n))],
        out_specs=pl.BlockSpec((tm, tn), lambda m, n: (m, n)),
        out_shape=jax.ShapeDtypeStruct((M, H), BF16),
        scratch_shapes=[pltpu.VMEM((tm, K), BF16)],
        compiler_params=_cp("parallel", "arbitrary"),
        name="ffn_in_swiglu",
    )(x, gain, scale, shift, w, w)


def _resid_kernel(a_ref, b_ref, x_ref, g_ref, o_ref):
    o_ref[...] = x_ref[...] + g_ref[...] * _dot(a_ref[...], b_ref[...])


def _mm_resid(a, b, x, gate, tm, tn, name):
    M, K = a.shape
    N = b.shape[1]
    tm = min(tm, M)
    return pl.pallas_call(
        _resid_kernel,
        grid=(N // tn, M // tm),
        in_specs=[pl.BlockSpec((tm, K), lambda n, m: (m, 0)),
                  pl.BlockSpec((K, tn), lambda n, m: (0, n)),
                  pl.BlockSpec((tm, tn), lambda n, m: (m, n)),
                  pl.BlockSpec((1, tn), lambda n, m: (0, n))],
        out_specs=pl.BlockSpec((tm, tn), lambda n, m: (m, n)),
        out_shape=jax.ShapeDtypeStruct((M, N), F32),
        compiler_params=_cp("parallel", "parallel"),
        name=name,
    )(a, b, x, gate)


def _merge_kernel(h_ref, wg0, wg1, wg2, wg3, oa, ob, oc, od, wa, wb, wc, wd, o_ref):
    h = h_ref[...]
    acc = None
    for wg, o_br, w_br in ((wg0, oa, wa), (wg1, ob, wb), (wg2, oc, wc), (wg3, od, wd)):
        term = _sigmoid(_dot(h, wg[...])) * _dot(o_br[...], w_br[...])
        acc = term if acc is None else acc + term
    o_ref[...] = acc.astype(o_ref.dtype)


def _merge(h, w_gate, branches, w_branches, tm, tn):
    M, D = h.shape
    tm = min(tm, M)
    nb = D // tn
    in_specs = [pl.BlockSpec((tm, D), lambda n, m: (m, 0))]
    for i in range(N_BRANCH):
        in_specs.append(pl.BlockSpec((D, tn), lambda n, m, i=i: (0, i * nb + n)))
    for br in branches:
        in_specs.append(pl.BlockSpec((tm, br.shape[1]), lambda n, m: (m, 0)))
    for wb in w_branches:
        in_specs.append(pl.BlockSpec((wb.shape[0], tn), lambda n, m: (0, n)))
    return pl.pallas_call(
        _merge_kernel,
        grid=(nb, M // tm),
        in_specs=in_specs,
        out_specs=pl.BlockSpec((tm, tn), lambda n, m: (m, n)),
        out_shape=jax.ShapeDtypeStruct((M, D), BF16),
        compiler_params=_cp("parallel", "parallel"),
        name="gated_merge",
    )(h, w_gate, w_gate, w_gate, w_gate, *branches, *w_branches)


def _gla_kernel(q_ref, k_ref, v_ref, g_ref, al_ref, wa_ref, ba_ref, gn_ref, lt_ref, blk_ref, e_ref, sel_ref,
                o_ref, st_ref, qe_ref, ke_ref, b_ref, dec_ref, acc_ref):
    C = GLA_SUB
    R = q_ref.shape[0]

    @pl.when(pl.program_id(0) == 0)
    def _():
        st_ref[...] = jnp.zeros_like(st_ref)

    gk = _log_sigmoid(_dot_hi(al_ref[...], wa_ref[...]) + ba_ref[...]) * (1.0 / GLA_NORMALIZER)
    gk_parts = _split3(gk)
    b = _group_rows(lt_ref[...], gk_parts)
    tot = _group_rows(blk_ref[...], gk_parts)
    scale = GLA_DK ** -0.5
    qe_ref[...] = (q_ref[...] * scale * jnp.exp(b)).astype(BF16)
    ke_ref[...] = (k_ref[...] * jnp.exp(tot - b)).astype(BF16)
    b_ref[...] = b
    dec_ref[...] = jnp.exp(tot)

    e_mat = e_ref[...]
    sel = sel_ref[...]
    s_idx = lax.broadcasted_iota(jnp.int32, (C, GLA_HEADS * GLA_DK), 0)

    K = GLA_SUBS_PER_TRIP
    HS = range(GLA_HEADS)
    kh = [slice(h * GLA_DK, (h + 1) * GLA_DK) for h in HS]
    vh = [slice(h * GLA_DV, (h + 1) * GLA_DV) for h in HS]

    def body(trip, carry):
        r0 = [pl.multiple_of((trip * K + i) * C, C) for i in range(K)]
        rows = [pl.ds(r, C) for r in r0]
        vs = [v_ref[r, :] for r in rows]
        p_st = []
        for i in range(K):
            qs = q_ref[rows[i], :] * scale
            ks = k_ref[rows[i], :]
            bs = b_ref[rows[i], :]
            pairs = []
            for t in range(C):
                e = jnp.exp(jnp.where(s_idx <= t, bs[t:t + 1, :] - bs, -jnp.inf))
                pairs.append(qs[t:t + 1, :] * ks * e)
            p_st.append(jnp.concatenate(pairs, axis=0).astype(BF16))
        sc = [_dot(p, e_mat) for p in p_st]
        w = [(sc[i] * jnp.concatenate([vs[i]] * C, axis=0)).astype(BF16) for i in range(K)]
        intra = [_dot(sel, x) for x in w]
        qe = [qe_ref[r, :] for r in rows]
        ke = [ke_ref[r, :] for r in rows]
        kv = [[_dot_tn(vs[i].astype(BF16)[:, vh[h]], ke[i][:, kh[h]]) for h in HS] for i in range(K)]
        st = [st_ref[h] for h in HS]
        for i in range(K):
            dec = dec_ref[pl.ds(r0[i], 1), :]
            outs = []
            for h in HS:
                outs.append(_dot_nt(qe[i][:, kh[h]], st[h].astype(BF16)))
                st[h] = st[h] * dec[:, kh[h]] + kv[i][h]
            acc_ref[rows[i], :] = jnp.concatenate(outs, axis=1) + intra[i]
        for h in HS:
            st_ref[h] = st[h]
        return carry

    lax.fori_loop(0, R // (C * K), body, 0)

    o = acc_ref[...]
    g = g_ref[...]
    gn = gn_ref[...]
    parts = []
    for h in range(GLA_HEADS):
        oh = o[:, h * GLA_DV:(h + 1) * GLA_DV]
        parts.append(oh * lax.rsqrt(jnp.mean(oh * oh, axis=-1, keepdims=True) + NORM_EPS) * gn)
    o_ref[...] = (jnp.concatenate(parts, axis=1) * (g * _sigmoid(g))).astype(o_ref.dtype)


def _gla(p, w_a2, b_a2, g_norm):
    S = p.shape[0]
    R = min(S, 512)
    C = GLA_SUB
    HK, HV = GLA_HEADS * GLA_DK, GLA_HEADS * GLA_DV
    wa = jnp.zeros((128, HK), F32).at[:GLA_RANK].set(w_a2)
    ri = np.arange(SLAB)
    same = (ri[:, None] // C) == (ri[None, :] // C)
    ltri = jnp.asarray((same & (ri[None, :] <= ri[:, None])).astype(np.float32), BF16)
    blk = jnp.asarray(same.astype(np.float32), BF16)
    e_mat =jnp.asarray((np.arange(HK)[:, None] // GLA_DK == np.arange(HV)[None, :] // GLA_DV).astype(np.float32), BF16)
    sel = jnp.asarray((np.arange(C)[:, None] == np.arange(C * C)[None, :] // C).astype(np.float32), BF16)
    const = lambda shape: pl.BlockSpec(shape, lambda i: (0, 0))
    return pl.pallas_call(
        _gla_kernel,
        grid=(S // R,),
        in_specs=[pl.BlockSpec((R, HK), lambda i: (i, 0)),
                  pl.BlockSpec((R, HK), lambda i: (i, 1)),
                  pl.BlockSpec((R, HV), lambda i: (i, 1)),
                  pl.BlockSpec((R, HV), lambda i: (i, 2)),
                  pl.BlockSpec((R, 128), lambda i: (i, (2 * HK + 2 * HV) // 128)),
                  const((128, HK)), const((1, HK)), const((1, GLA_DV)),
                  const((SLAB, SLAB)), const((SLAB, SLAB)), const((HK, HV)), const((C, C * C))],
        out_specs=pl.BlockSpec((R, HV), lambda i: (i, 0)),
        out_shape=jax.ShapeDtypeStruct((S, HV), BF16),
        scratch_shapes=[pltpu.VMEM((GLA_HEADS, GLA_DV, GLA_DK), F32),
                        pltpu.VMEM((R, HK), BF16), pltpu.VMEM((R, HK), BF16),
                        pltpu.VMEM((R, HK), F32), pltpu.VMEM((R, HK), F32),
                        pltpu.VMEM((R, HV), F32)],
        compiler_params=_cp("arbitrary"),
        name="gla",
    )(p, p, p, p, p, wa, b_a2[None, :], g_norm[None, :], ltri, blk, e_mat, sel)


def _band_kernel(dil, q_ref, k_ref, v_ref, kp_ref, vp_ref, c_ref, s_ref, cp_ref, sp_ref, o_ref, l_ref):
    t = pl.program_id(1)
    blk = DIL_BLK
    span = blk * dil
    qi = lax.broadcasted_iota(jnp.int32, (blk, 2 * blk), 0)
    ki = lax.broadcasted_iota(jnp.int32, (blk, 2 * blk), 1)
    dist = blk + qi - ki
    band = jnp.where(dist >= 0, jnp.where(dist <= blk, 1, 0), 0)
    band_first = band * jnp.where(ki >= blk, 1, jnp.where(t > 0, 1, 0))

    def rows_of(start):
        return pl.ds(start, blk, stride=dil) if dil > 1 else pl.ds(start, blk)

    for r in range(dil):
        kp = vp = None
        for b in range(DIL_TILE // span):
            rows = rows_of(b * span + r)
            cos_c, sin_c = c_ref[rows, :], s_ref[rows, :]
            q = _rope(q_ref[rows, :], cos_c, sin_c) * (HEAD_DIM ** -0.5)
            kc = _rope(k_ref[rows, :], cos_c, sin_c).astype(BF16)
            vc = v_ref[rows, :].astype(BF16)
            if b == 0:
                prow = rows_of(r)
                kp = _rope(kp_ref[prow, :], cp_ref[prow, :], sp_ref[prow, :]).astype(BF16)
                vp = vp_ref[prow, :].astype(BF16)
            s = _dot_nt(q.astype(BF16), jnp.concatenate([kp, kc], axis=0))
            s = jnp.where((band_first if b == 0 else band) > 0, s, -jnp.inf)
            m = jnp.max(s, axis=-1, keepdims=True)
            pr = jnp.exp(s - m)
            den = jnp.sum(pr, axis=-1, keepdims=True)
            o_ref[rows, :] = _dot(pr.astype(BF16), jnp.concatenate([vp, vc], axis=0)) / den
            l_ref[rows, :] = jnp.broadcast_to(m + jnp.log(den), (blk, HEAD_DIM))
            kp, vp = kc, vc


def _banded_group(p_dil, cos_f, sin_f, group, dil):
    S = p_dil.shape[0]
    hp = DIL_HEADS_PER_GROUP
    span = DIL_BLK * dil
    assert S % DIL_TILE == 0 and DIL_TILE % span == 0
    per_tile = DIL_TILE // span

    def col(part):
        return lambda h, t: (t, part * DIL_HEADS + group * hp + h)

    def col_prev(part):
        return lambda h, t: (jnp.maximum(t * per_tile - 1, 0), part * DIL_HEADS + group * hp + h)

    cur = (DIL_TILE, HEAD_DIM)
    prev = (span, HEAD_DIM)
    tab_c = lambda h, t: (t, 0)
    tab_p = lambda h, t: (jnp.maximum(t * per_tile - 1, 0), 0)
    out_map = lambda h, t: (t, h)
    o, l = pl.pallas_call(
        functools.partial(_band_kernel, dil),
        grid=(hp, S // DIL_TILE),
        in_specs=[pl.BlockSpec(cur, col(0)), pl.BlockSpec(cur, col(1)), pl.BlockSpec(cur, col(2)),
                  pl.BlockSpec(prev, col_prev(1)), pl.BlockSpec(prev, col_prev(2)),
                  pl.BlockSpec(cur, tab_c), pl.BlockSpec(cur, tab_c), pl.BlockSpec(prev, tab_p), pl.BlockSpec(prev, tab_p)],
        out_specs=[pl.BlockSpec(cur, out_map)] * 2,
        out_shape=[jax.ShapeDtypeStruct((S, DIL_OUT), F32)] * 2,
        compiler_params=_cp("parallel", "parallel"),
        name=f"dilated_band_d{dil}",
    )(p_dil, p_dil, p_dil, p_dil, p_dil, cos_f, sin_f, cos_f, sin_f)
    return o, l


def _dil_combine_kernel(o0, o1, o2, l0, l1, l2, out_ref):
    a, b, c = l0[...], l1[...], l2[...]
    m = jnp.maximum(jnp.maximum(a, b), c)
    ea, eb, ec = jnp.exp(a - m), jnp.exp(b - m), jnp.exp(c - m)
    out_ref[...] = ((ea * o0[...] + eb * o1[...] + ec * o2[...]) / (ea + eb + ec)).astype(out_ref.dtype)


def _dilated(p_dil, cos_f, sin_f):
    S = p_dil.shape[0]
    outs, lses = [], []
    for g, (window, dil) in enumerate(DIL_PAIRS):
        assert window // dil == DIL_BLK
        o, l = _banded_group(p_dil, cos_f, sin_f, g, dil)
        outs.append(o)
        lses.append(l)
    tm = min(S, 1024)
    spec = pl.BlockSpec((tm, DIL_OUT), lambda i: (i, 0))
    return pl.pallas_call(
        _dil_combine_kernel,
        grid=(S // tm,),
        in_specs=[spec] * 6,
        out_specs=spec,
        out_shape=jax.ShapeDtypeStruct((S, DIL_OUT), BF16),
        compiler_params=_cp("parallel"),
        name="dilated_combine",
    )(*outs, *lses)


def _rwkv_prep_kernel(has_vmix, *refs):
    if has_vmix:
        (p_ref, prev_ref, mu_ref, w0_ref, w2_ref, a0_ref, a2_ref, g2_ref, kk_ref, ka_ref, rk_ref, hd_ref,
         lt_ref, pick_ref, vf_ref, v0_ref, v1_ref, v2_ref,
         at_o, rt_o, bt_o, kt_o, v16_o, gc_o, bonus_o, g_o) = refs
        v_o = None
    else:
        (p_ref, prev_ref, mu_ref, w0_ref, w2_ref, a0_ref, a2_ref, g2_ref, kk_ref, ka_ref, rk_ref, hd_ref,
         lt_ref, pick_ref,
         at_o, rt_o, bt_o, kt_o, v16_o, gc_o, bonus_o, g_o, v_o) = refs
    W = RWKV_WIDTH
    p = p_ref[...]
    last = jnp.where(pl.program_id(0) == 0, 0.0, prev_ref[7:8, :])
    row = lax.broadcasted_iota(jnp.int32, p.shape, 0)
    p_prev = jnp.where(row == 0, last, pltpu.roll(p, 1, axis=0))
    xs = p + (p_prev - p) * mu_ref[...]
    r, k, v = xs[:, 0:W], xs[:, W:2 * W], xs[:, 2 * W:3 * W]
    w_low = xs[:, 3 * W:3 * W + LORA_PAD]
    a_low = xs[:, 3 * W + LORA_PAD:3 * W + 2 * LORA_PAD]
    g_low = xs[:, 3 * W + 2 * LORA_PAD:]
    lora = lambda t, w_ref: _dot(t.astype(BF16), w_ref[...].astype(BF16))
    log_w = _log_sigmoid(w0_ref[...] + lora(jnp.tanh(w_low), w2_ref)) - 0.5
    a = _sigmoid(a0_ref[...] + lora(a_low, a2_ref))
    g = lora(_sigmoid(g_low), g2_ref)
    if has_vmix:
        mix = _sigmoid(v0_ref[...] + lora(lora(v, v1_ref), v2_ref))
        v = v + (vf_ref[...] - v) * mix
    kk = k * kk_ref[...]
    kk = kk / jnp.maximum(jnp.sqrt(_group_sum_lanes(kk * kk, hd_ref[...])), 1e-12)
    k = k * (1.0 + (a - 1.0) * ka_ref[...])
    lw = -jnp.exp(log_w)
    lw_parts = _split3(lw)
    cum = _group_rows(lt_ref[...], lw_parts)
    e_in = jnp.exp(cum)
    e_inv = jnp.exp(-cum)
    at_o[...] = (-kk * jnp.exp(cum - lw)).astype(at_o.dtype)
    rt_o[...] = (r * e_in).astype(rt_o.dtype)
    bt_o[...] = (kk * a * e_inv).astype(bt_o.dtype)
    kt_o[...] = (k * e_inv).astype(kt_o.dtype)
    v16_o[...] = v.astype(v16_o.dtype)
    gc_o[...] = jnp.exp(sum(_dot(pick_ref[...], part) for part in lw_parts))
    bonus_o[...] = _group_sum_lanes(r * k * rk_ref[...], hd_ref[...]) * v
    g_o[...] = g
    if v_o is not None:
        v_o[...] = v


def _pad_rows(w, rows):
    return jnp.zeros((rows, w.shape[1]), w.dtype).at[:w.shape[0]].set(w)


def _pad_rwkv_cols(t):
    W = RWKV_WIDTH
    z = jnp.zeros(t.shape[:-1] + (LORA_PAD - RWKV_DECAY_LORA,), t.dtype)
    o1, o2 = 3 * W + RWKV_DECAY_LORA, 3 * W + RWKV_DECAY_LORA + RWKV_AAA_LORA
    return jnp.concatenate([t[..., :o1], z, t[..., o1:o2], z, t[..., o2:]], axis=-1)


def _head_indicator():
    i = np.arange(SLAB) // RWKV_HEAD_SIZE
    return jnp.asarray((i[:, None] == i[None, :]).astype(np.float32), BF16)


def _rwkv_prep(p, mu, w0, w2, a0, a2, g2, k_k, k_a, r_k, v_first, v_mix):
    S = p.shape[0]
    W = RWKV_WIDTH
    C = RWKV_CHUNK
    R = min(S, 512)
    has_vmix = v_mix is not None
    row = lambda n: pl.BlockSpec((1, n), lambda i: (0, 0))
    full = lambda a: pl.BlockSpec(a.shape, lambda i: (0, 0))
    tile = pl.BlockSpec((R, W), lambda i: (i, 0))
    w2p, a2p = _pad_rows(w2, LORA_PAD), _pad_rows(a2, LORA_PAD)
    hd = _head_indicator()
    ri = np.arange(SLAB)
    same = (ri[:, None] // C) == (ri[None, :] // C)
    ltri = jnp.asarray((same & (ri[None, :] <= ri[:, None])).astype(np.float32), BF16)
    pick = jnp.asarray((np.arange(R // C)[:, None] == np.arange(R)[None, :] // C).astype(np.float32), BF16)
    args = [p, p, _pad_rwkv_cols(mu)[None, :], w0[None, :], w2p, a0[None, :], a2p, g2, k_k[None, :], k_a[None, :],
            r_k[None, :], hd, ltri, pick]
    in_specs = [pl.BlockSpec((R, RWKV_IN_PAD), lambda i: (i, 0)),
                pl.BlockSpec((8, RWKV_IN_PAD), lambda i: (jnp.maximum(i * (R // 8) - 1, 0), 0)),
                row(RWKV_IN_PAD), row(W), full(w2p), row(W), full(a2p), full(g2), row(W), row(W), row(W),
                full(hd), full(ltri), full(pick)]
    if has_vmix:
        v0, v1, v2 = v_mix
        args += [v_first, v0[None, :], v1, v2]
        in_specs += [tile, row(W), full(v1), full(v2)]
    gc_tile = pl.BlockSpec((R // C, W), lambda i: (i, 0))
    out_specs = [tile] * 5 + [gc_tile, tile, tile]
    out_shape = ([jax.ShapeDtypeStruct((S, W), BF16)] * 5
                 + [jax.ShapeDtypeStruct((S // C, W), F32)] + [jax.ShapeDtypeStruct((S, W), F32)] * 2)
    if not has_vmix:
        out_specs.append(tile)
        out_shape.append(jax.ShapeDtypeStruct((S, W), F32))
    return pl.pallas_call(
        functools.partial(_rwkv_prep_kernel, has_vmix),
        grid=(S // R,),
        in_specs=in_specs,
        out_specs=out_specs,
        out_shape=out_shape,
        compiler_params=_cp("parallel"),
        name="rwkv_prep",
    )(*args)


def _rwkv_scan_kernel(at_ref, rt_ref, bt_ref, kt_ref, v_ref, gc_ref, mask_ref, eye_ref, y_ref, st_ref):
    C = RWKV_CHUNK
    N = RWKV_HEAD_SIZE
    n_chunks = at_ref.shape[0] // C

    @pl.when(pl.program_id(0) == 0)
    def _():
        st_ref[...] = jnp.zeros_like(st_ref)

    mask = mask_ref[...]
    eye = eye_ref[...]

    H = RWKV_HEADS
    K = RWKV_CHUNKS_PER_TRIP
    HS = range(H)
    zeros_cn = jnp.zeros((C, N), BF16)

    def body(trip, carry):
        units = []
        for kc in range(K):
            c = trip * K + kc
            rows = pl.ds(pl.multiple_of(c * C, C), C)
            tiles = [ref[rows, :] for ref in (at_ref, rt_ref, bt_ref, kt_ref, v_ref)]
            g_row = gc_ref[pl.ds(c, 1), :]
            for h in HS:
                s = slice(h * N, (h + 1) * N)
                units.append(tuple(t[:, s] for t in tiles) + (g_row[:, s],))
        U = range(len(units))
        a_t, r_t, b_t, k_t, vv, g_c = ([u[i] for u in units] for i in range(6))
        m_all = [_dot_nt(jnp.concatenate([a_t[u], r_t[u]], axis=0), jnp.concatenate([b_t[u], k_t[u]], axis=0)) * mask
                 for u in U]
        m16 = [m.astype(BF16) for m in m_all]
        m_ab16 = [m[:C, :C] for m in m16]
        t_inv = [eye + m[:C, :C] for m in m_all]
        x = [_dot(m, m) for m in m_ab16]
        for step in range(5):
            x16 = [v.astype(BF16) for v in x]
            t_inv = [t_inv[u] + _dot(x16[u], t_inv[u].astype(BF16)) for u in U]
            if step < 4:
                x = [_dot(v, v) for v in x16]
        t16 = [t.astype(BF16) for t in t_inv]
        mv = [_dot(m16[u], jnp.concatenate([zeros_cn, vv[u]], axis=0)) for u in U]
        p16 = [_dot(t16[u], a_t[u]).astype(BF16) for u in U]
        q16 = [_dot(t16[u], mv[u][:C].astype(BF16)).astype(BF16) for u in U]
        m_rb16 = [m[C:, :C] for m in m16]
        p2 = [r_t[u].astype(F32) + _dot(m_rb16[u], p16[u]) for u in U]
        q2 = [mv[u][C:] + _dot(m_rb16[u], q16[u]) for u in U]
        b_end = [(b_t[u] * g_c[u]).astype(BF16) for u in U]
        k_end = [(k_t[u] * g_c[u]).astype(BF16) for u in U]
        g_mat = [_dot_tn(b_end[u], p16[u]) + eye * g_c[u] for u in U]
        h_mat = [_dot_tn(jnp.concatenate([b_end[u], k_end[u]], axis=0), jnp.concatenate([q16[u], vv[u]], axis=0))
                 for u in U]
        pg16 = [jnp.concatenate([p2[u], g_mat[u]], axis=0).astype(BF16) for u in U]
        st = [st_ref[h] for h in HS]
        for kc in range(K):
            ys = []
            for h in HS:
                u = kc * H + h
                ys_st = _dot(pg16[u], st[h].astype(BF16))
                ys.append(ys_st[:C] + q2[u])
                st[h] = ys_st[C:] + h_mat[u]
            rows = pl.ds(pl.multiple_of((trip * K + kc) * C, C), C)
            y_ref[rows, :] = jnp.concatenate(ys, axis=1)
        for h in HS:
            st_ref[h] = st[h]
        return carry

    lax.fori_loop(0, n_chunks // K, body, 0)


def _rwkv_scan(at, rt, bt, kt, v16, gc):
    S, W = at.shape
    C = RWKV_CHUNK
    R = min(S, 512)
    ti = np.arange(C)
    strict = (ti[None, :] < ti[:, None]).astype(np.float32)
    incl = (ti[None, :] <= ti[:, None]).astype(np.float32)
    mask = jnp.asarray(np.block([[strict, strict], [incl, incl]]))
    eye = jnp.eye(C, dtype=F32)
    tile = pl.BlockSpec((R, W), lambda i: (i, 0))
    const = lambda a_: pl.BlockSpec(a_.shape, lambda i: (0, 0))
    return pl.pallas_call(
        _rwkv_scan_kernel,
        grid=(S // R,),
        in_specs=[tile] * 5 + [pl.BlockSpec((R // C, W), lambda i: (i, 0)), const(mask), const(eye)],
        out_specs=tile,
        out_shape=jax.ShapeDtypeStruct((S, W), F32),
        scratch_shapes=[pltpu.VMEM((RWKV_HEADS, RWKV_HEAD_SIZE, RWKV_HEAD_SIZE), F32)],
        compiler_params=_cp("arbitrary"),
        name="rwkv_scan",
    )(at, rt, bt, kt, v16, gc, mask, eye)


def _rwkv_post_kernel(y_ref, bonus_ref, g_ref, lw_ref, lb_ref, hd_ref, o_ref):
    hd = hd_ref[...]
    inv_n = 1.0 / RWKV_HEAD_SIZE
    y = y_ref[...]
    yc = y - _group_sum_lanes(y, hd) * inv_n
    var = _group_sum_lanes(yc * yc, hd) * inv_n
    yn = yc * lax.rsqrt(var + RWKV_LNX_EPS) * lw_ref[...] + lb_ref[...]
    o_ref[...] = ((yn + bonus_ref[...]) * g_ref[...]).astype(o_ref.dtype)


def _rwkv_post(y, bonus, g, lnx_w, lnx_b):
    S, W = y.shape
    R = min(S, 512)
    tile = pl.BlockSpec((R, W), lambda i: (i, 0))
    row = pl.BlockSpec((1, W), lambda i: (0, 0))
    hd = _head_indicator()
    return pl.pallas_call(
        _rwkv_post_kernel,
        grid=(S // R,),
        in_specs=[tile] * 3 + [row] * 2 + [pl.BlockSpec((SLAB, SLAB), lambda i: (0, 0))],
        out_specs=tile,
        out_shape=jax.ShapeDtypeStruct((S, W), BF16),
        compiler_params=_cp("parallel"),
        name="rwkv_post",
    )(y, bonus, g, lnx_w[None, :], lnx_b[None, :], hd)


def _rwkv(p, mu, w0, w2, a0, a2, g2, k_k, k_a, r_k, lnx_w, lnx_b, v_first, v_mix):
    outs = _rwkv_prep(p, mu, w0, w2, a0, a2, g2, k_k, k_a, r_k, v_first, v_mix)
    at, rt, bt, kt, v16, gc, bonus, g = outs[:8]
    y = _rwkv_scan(at, rt, bt, kt, v16, gc)
    out = _rwkv_post(y, bonus, g, lnx_w, lnx_b)
    return out, (outs[8] if v_mix is None else v_first)


def _moba_prep_kernel(k_ref, v_ref, c_ref, s_ref, kr_ref, vt_ref, km_ref):
    cos_f, sin_f = c_ref[...], s_ref[...]
    parts = []
    for h in range(MOBA_HEADS):
        parts.append(_rope(k_ref[:, h * HEAD_DIM:(h + 1) * HEAD_DIM], cos_f, sin_f))
    kr = jnp.concatenate(parts, axis=1)
    lane = lax.broadcasted_iota(jnp.int32, (k_ref.shape[0], HEAD_DIM), 1)
    tag = jnp.where(lane == pl.program_id(0), 1.0, 0.0).astype(BF16)
    kr_ref[...] = jnp.concatenate(
        [t for h in range(MOBA_HEADS) for t in (parts[h].astype(BF16), tag)], axis=1)
    vt = v_ref[...].T
    ones = jnp.where(lax.broadcasted_iota(jnp.int32, (MOBA_VROWS - HEAD_DIM, vt.shape[1]), 0) == 0, 1.0, 0.0)
    vt_ref[...] = jnp.concatenate(
        [t for h in range(MOBA_HEADS) for t in (vt[h * HEAD_DIM:(h + 1) * HEAD_DIM], ones)], axis=0).astype(BF16)
    km_ref[...] = jnp.broadcast_to(jnp.mean(kr, axis=0, keepdims=True), km_ref.shape)


def _moba_kernel(q_ref, c_ref, s_ref, k_ref, vt_ref, km_ref, o_ref):
    cb = pl.program_id(1)
    blk = MOBA_BLOCK
    nb = km_ref.shape[0]
    HP = range(MOBA_HEADS_PER_STEP)
    kcol = [slice(h * 2 * HEAD_DIM, (h + 1) * 2 * HEAD_DIM) for h in HP]
    vrow = [slice(h * MOBA_VROWS, (h + 1) * MOBA_VROWS) for h in HP]
    cos_f, sin_f = c_ref[...], s_ref[...]
    own = pl.ds(pl.multiple_of(cb * blk, blk), blk)
    bi = lax.broadcasted_iota(jnp.int32, (nb, blk), 0)
    ki = lax.broadcasted_iota(jnp.int32, (blk, blk), 0)
    qi = lax.broadcasted_iota(jnp.int32, (blk, blk), 1)
    qa16, m0, acc0 = [], [], []
    for h in HP:
        q = _rope(q_ref[:, h * HEAD_DIM:(h + 1) * HEAD_DIM], cos_f, sin_f) * (HEAD_DIM ** -0.5)
        qt = q.T
        gate = jnp.where(bi < cb, _dot_hi(km_ref[:, h * HEAD_DIM:(h + 1) * HEAD_DIM], qt), -jnp.inf)
        sel = jnp.zeros((nb, blk), F32)
        for _ in range(MOBA_TOPK):
            m = jnp.max(gate, axis=0, keepdims=True)
            idx = jnp.min(jnp.where(gate == m, bi, nb), axis=0, keepdims=True)
            hit = bi == idx
            sel = jnp.where(hit, jnp.where(m > -jnp.inf, 1.0, sel), sel)
            gate = jnp.where(hit, -jnp.inf, gate)
        qt16 = qt.astype(BF16)
        bias = ((1.0 - sel) * MOBA_MASKED).astype(BF16)
        qa16.append(jnp.concatenate([qt16, bias, jnp.zeros((HEAD_DIM - nb, blk), BF16)], axis=0))
        s = _dot(k_ref[own, h * 2 * HEAD_DIM:h * 2 * HEAD_DIM + HEAD_DIM], qt16)
        s = jnp.where(ki <= qi, s, -jnp.inf)
        m0.append(jnp.max(s, axis=0, keepdims=True))
        acc0.append(_dot(vt_ref[cb, vrow[h], :], jnp.exp(s - m0[h]).astype(BF16)))

    G = MOBA_GROUP

    def body(jg, carry):
        m, acc = list(carry[0]), list(carry[1])
        js = [jg * G + g for g in range(G)]

        def scores(j):
            rows = pl.ds(pl.multiple_of(j * blk, blk), blk)
            return [_dot(k_ref[rows, kcol[h]], qa16[h]) for h in HP]

        s = scores(js[0])
        for g in range(G):
            s_next = scores(js[g + 1]) if g + 1 < G else None
            m_new = [jnp.maximum(m[h], jnp.max(s[h], axis=0, keepdims=True)) for h in HP]
            pr = [jnp.exp(s[h] - m_new[h]).astype(BF16) for h in HP]
            acc = [jnp.exp(m[h] - m_new[h]) * acc[h] + _dot(vt_ref[js[g], vrow[h], :], pr[h]) for h in HP]
            m, s = m_new, s_next
        return tuple(m), tuple(acc)

    m, acc = lax.fori_loop(0, (cb + G - 1) // G, body, (tuple(m0), tuple(acc0)))
    o_ref[...] = jnp.concatenate(
        [(acc[h][:HEAD_DIM] / acc[h][HEAD_DIM:HEAD_DIM + 1]).T for h in HP], axis=1).astype(o_ref.dtype)


def _moba(p, cos_f, sin_f):
    S = p.shape[0]
    blk = MOBA_BLOCK
    nb = S // blk
    assert nb <= HEAD_DIM, "block ids are one-hot encoded in HEAD_DIM lanes"
    assert nb % MOBA_GROUP == 0, "the block loop over-runs to a multiple of MOBA_GROUP"
    HW = MOBA_HEADS * HEAD_DIM
    kr, vt, km = pl.pallas_call(
        _moba_prep_kernel,
        grid=(nb,),
        in_specs=[pl.BlockSpec((blk, HW), lambda i: (i, 1)), pl.BlockSpec((blk, HW), lambda i: (i, 2)),
                  pl.BlockSpec((blk, HEAD_DIM), lambda i: (i, 0)), pl.BlockSpec((blk, HEAD_DIM), lambda i: (i, 0))],
        out_specs=[pl.BlockSpec((blk, 2 * HW), lambda i: (i, 0)),
                   pl.BlockSpec((None, MOBA_HEADS * MOBA_VROWS, blk), lambda i: (i, 0, 0)),
                   pl.BlockSpec((8, HW), lambda i: (i, 0))],
        out_shape=[jax.ShapeDtypeStruct((S, 2 * HW), BF16),
                   jax.ShapeDtypeStruct((nb, MOBA_HEADS * MOBA_VROWS, blk), BF16),
                   jax.ShapeDtypeStruct((nb * 8, HW), F32)],
        compiler_params=_cp("parallel"),
        name="moba_prep",
    )(p, p, cos_f, sin_f)
    k_mean = km.reshape(nb, 8, HW)[:, 0]
    tab = pl.BlockSpec((blk, HEAD_DIM), lambda h, cb: (cb, 0))
    P = MOBA_HEADS_PER_STEP
    once = pl.Buffered(1)
    return pl.pallas_call(
        _moba_kernel,
        grid=(MOBA_HEADS // P, nb),
        in_specs=[pl.BlockSpec((blk, P * HEAD_DIM), lambda h, cb: (cb, h)), tab, tab,
                  pl.BlockSpec((S, P * 2 * HEAD_DIM), lambda h, cb: (0, h), pipeline_mode=once),
                  pl.BlockSpec((nb, P * MOBA_VROWS, blk), lambda h, cb: (0, h, 0), pipeline_mode=once),
                  pl.BlockSpec((nb, P * HEAD_DIM), lambda h, cb: (0, h))],
        out_specs=pl.BlockSpec((blk, P * HEAD_DIM), lambda h, cb: (cb, h)),
        out_shape=jax.ShapeDtypeStruct((S, HW), BF16),
        compiler_params=_cp("parallel", "arbitrary"),
        name="moba",
    )(p, cos_f, sin_f, kr, vt, k_mean)


def _split_w_in(w_in_l):
    o = np.cumsum((N_BRANCH * D_MODEL, GLA_IN, DIL_IN, RWKV_IN, MOBA_IN))
    w_gate, w_gla, w_dil, w_rwkv, w_moba = (w_in_l[:, a:b] for a, b in zip((0,) + tuple(o[:-1]), o))
    w_gla = jnp.concatenate([w_gla, jnp.zeros((D_MODEL, GLA_IN_PAD - GLA_IN), w_gla.dtype)], axis=1)
    w_rwkv = _pad_rwkv_cols(w_rwkv)
    return tuple(w.astype(BF16) for w in (w_gate, w_gla, w_dil, w_rwkv, w_moba))


def kernel(x, c, positions, w_ada, b_ada, norm1, w_in, gla_w_a2, gla_b_a2, gla_gnorm, rwkv_mu, rwkv_w0, rwkv_w2, rwkv_a0, rwkv_a2, rwkv_g2, rwkv_k_k, rwkv_k_a, rwkv_r_k, rwkv_lnx_w, rwkv_lnx_b, rwkv_v0, rwkv_v1, rwkv_v2, w_branch_a, w_branch_b, w_branch_c, w_branch_d, w_out, norm2, w_ffn_in, w_ffn_out, norm_f):
    B, S, D = x.shape
    assert B == 1 and D == D_MODEL
    xs = x.reshape(S, D)
    cos_f, sin_f = _rope_tables(positions)
    c8 = jnp.broadcast_to(c, (8, D))
    v_first = None
    for l in range(DEPTH):
        mod = _adaln(c8, w_ada, b_ada, l)[0:1]
        shift1, scale1, gate1, shift2, scale2, gate2 = jnp.split(mod, 6, axis=-1)

        h = _normmod(xs, norm1[l][None, :], scale1, shift1, BF16)
        w_gate, w_gla, w_dil, w_rwkv, w_moba = _split_w_in(w_in[l])
        p_gla = _mm(h, w_gla, 512, GLA_IN_PAD, F32, "proj_gla")
        p_dil = _mm(h, w_dil, 512, DIL_IN, F32, "proj_dil")
        p_rwkv = _mm(h, w_rwkv, 512, RWKV_IN_PAD, F32, "proj_rwkv")
        p_moba = _mm(h, w_moba, 512, MOBA_IN, F32, "proj_moba")

        o_gla = _gla(p_gla, gla_w_a2[l], gla_b_a2[l], gla_gnorm[l])
        o_dil = _dilated(p_dil, cos_f, sin_f)
        v_mix = None if l == 0 else (rwkv_v0[l - 1], rwkv_v1[l - 1], rwkv_v2[l - 1])
        o_rwkv, v_first = _rwkv(p_rwkv, rwkv_mu[l], rwkv_w0[l], rwkv_w2[l], rwkv_a0[l], rwkv_a2[l], rwkv_g2[l],
                                rwkv_k_k[l], rwkv_k_a[l], rwkv_r_k[l], rwkv_lnx_w[l], rwkv_lnx_b[l], v_first, v_mix)
        o_moba = _moba(p_moba, cos_f, sin_f)

        merged = _merge(h, w_gate, (o_gla, o_dil, o_rwkv, o_moba),
                        tuple(w[l].astype(BF16) for w in (w_branch_a, w_branch_b, w_branch_c, w_branch_d)), 1024, 512)
        xs = _mm_resid(merged, w_out[l].astype(BF16), xs, gate1, 1024, 1024, "out_proj_resid")

        act = _norm_mm_swiglu(xs, norm2[l][None, :], scale2, shift2, w_ffn_in[l].astype(BF16), 1024, 512)
        xs = _mm_resid(act, w_ffn_out[l].astype(BF16), xs, gate2, 1024, 512, "ffn_out_resid")

    zero = jnp.zeros((1, D), F32)
    return _normmod(xs, norm_f[None, :], zero, zero, F32).reshape(B, S, D)
```

```python
import functools

import numpy as np
import jax
import jax.numpy as jnp
from jax import lax
from jax.experimental import pallas as pl
from jax.experimental.pallas import tpu as pltpu

F32 = jnp.float32
BF16 = jnp.bfloat16
HI = lax.Precision.HIGHEST

D_MODEL = 2048
DEPTH = 2
HEAD_DIM = 128
ROT_DIM = HEAD_DIM // 4
ROPE_THETA = 500000.0
N_BRANCH = 4
NORM_EPS = 1e-6

GLA_HEADS = 4
GLA_DK = 64
GLA_DV = 128
GLA_RANK = 16
GLA_NORMALIZER = 16.0
GLA_SUB = 16
GLA_SUBS_PER_TRIP = 8

DIL_PAIRS = ((128, 1), (512, 4), (2048, 16))
DIL_HEADS_PER_GROUP = 2
DIL_HEADS = DIL_HEADS_PER_GROUP * len(DIL_PAIRS)
DIL_BLK = 128
DIL_TILE = DIL_BLK * max(d for _, d in DIL_PAIRS)

RWKV_HEAD_SIZE = 64
RWKV_HEADS = 8
RWKV_WIDTH = RWKV_HEADS * RWKV_HEAD_SIZE
RWKV_DECAY_LORA = 96
RWKV_AAA_LORA = 96
RWKV_MV_LORA = 64
RWKV_GATE_LORA = 256
RWKV_LNX_EPS = 64e-5
RWKV_CHUNK = 64
RWKV_CHUNKS_PER_TRIP = 4
LORA_PAD = 128

MOBA_HEADS = 4
MOBA_BLOCK = 256
MOBA_TOPK = 3
MOBA_MASKED = -1e30
MOBA_VROWS = HEAD_DIM + 16
MOBA_HEADS_PER_STEP = 2
MOBA_TILES_PER_STEP = 2
MOBA_GROUP = 4

FFN_HIDDEN = -(-8 * D_MODEL // (3 * 256)) * 256

GLA_SIZES = (GLA_HEADS * GLA_DK, GLA_HEADS * GLA_DK, GLA_HEADS * GLA_DV, GLA_HEADS * GLA_DV, GLA_RANK)
RWKV_SIZES = (RWKV_WIDTH, RWKV_WIDTH, RWKV_WIDTH, RWKV_DECAY_LORA, RWKV_AAA_LORA, RWKV_GATE_LORA)
GLA_IN = sum(GLA_SIZES)
DIL_IN = 3 * DIL_HEADS * HEAD_DIM
RWKV_IN = sum(RWKV_SIZES)
MOBA_IN = 3 * MOBA_HEADS * HEAD_DIM
GLA_OUT = GLA_HEADS * GLA_DV
DIL_OUT = DIL_HEADS_PER_GROUP * HEAD_DIM
RWKV_OUT = RWKV_WIDTH
MOBA_OUT = MOBA_HEADS * HEAD_DIM

GLA_IN_PAD = GLA_IN - GLA_RANK + 128
RWKV_IN_PAD = 3 * RWKV_WIDTH + 2 * LORA_PAD + RWKV_GATE_LORA

VMEM_LIMIT_BYTES = 56 * 1024 * 1024
SLAB = 128


def _cp(*sem):
    return pltpu.CompilerParams(dimension_semantics=sem, vmem_limit_bytes=VMEM_LIMIT_BYTES)


def _dot(a, b):
    return jnp.dot(a, b, preferred_element_type=F32)


def _dot_hi(a, b):
    return jnp.dot(a, b, precision=HI, preferred_element_type=F32)


def _split3(x):
    x1 = x.astype(BF16)
    r1 = x - x1.astype(F32)
    x2 = r1.astype(BF16)
    x3 = (r1 - x2.astype(F32)).astype(BF16)
    return x1, x2, x3


def _dot_sel_l(sel16, x):
    x1, x2, x3 = _split3(x)
    return _dot(sel16, x1) + _dot(sel16, x2) + _dot(sel16, x3)


def _dot_sel_r(x, sel16):
    x1, x2, x3 = _split3(x)
    return _dot(x1, sel16) + _dot(x2, sel16) + _dot(x3, sel16)


def _group_sum_lanes(x, ind16):
    n = ind16.shape[0]
    return jnp.concatenate([_dot_sel_r(x[:, i:i + n], ind16) for i in range(0, x.shape[1], n)], axis=1)


def _group_rows(sel16, parts):
    n = sel16.shape[1]
    slabs = range(0, parts[0].shape[0], n)
    return jnp.concatenate([sum(_dot(sel16, p[i:i + n]) for p in parts) for i in slabs], axis=0)


def _dot_nt(a, b, precision=None):
    return lax.dot_general(a, b, (((1,), (1,)), ((), ())), precision=precision, preferred_element_type=F32)


def _dot_tn(a, b, precision=None):
    return lax.dot_general(a, b, (((0,), (0,)), ((), ())), precision=precision, preferred_element_type=F32)


def _log_sigmoid(z):
    return jnp.minimum(z, 0.0) - jnp.log1p(jnp.exp(-jnp.abs(z)))


def _sigmoid(z):
    return 1.0 / (1.0 + jnp.exp(-z))


def _rope(x, cos_f, sin_f):
    half = ROT_DIM // 2
    lane = lax.broadcasted_iota(jnp.int32, x.shape, 1)
    partner = jnp.where(lane < half, pltpu.roll(x, HEAD_DIM - half, axis=1), pltpu.roll(x, half, axis=1))
    return x * cos_f + partner * sin_f


def _rope_table_kernel(pos_ref, inv_ref, cos_ref, sin_ref):
    ang = pos_ref[...] * inv_ref[...]
    lane = lax.broadcasted_iota(jnp.int32, ang.shape, 1)
    half = ROT_DIM // 2
    c = jnp.cos(ang)
    s = jnp.sin(ang)
    cos_ref[...] = jnp.where(lane < ROT_DIM, c, 1.0)
    sin_ref[...] = jnp.where(lane < half, -s, jnp.where(lane < ROT_DIM, s, 0.0))


def _rope_tables(positions):
    S = positions.shape[1]
    inv = ROPE_THETA ** (-jnp.arange(0, ROT_DIM, 2, dtype=F32) / ROT_DIM)
    inv_full = jnp.concatenate([inv, inv, jnp.zeros((HEAD_DIM - ROT_DIM,), F32)])[None, :]
    pos = positions.astype(F32).reshape(S, 1)
    tm = min(S, 1024)
    return pl.pallas_call(
        _rope_table_kernel,
        grid=(S // tm,),
        in_specs=[pl.BlockSpec((tm, 1), lambda i: (i, 0)), pl.BlockSpec((1, HEAD_DIM), lambda i: (0, 0))],
        out_specs=[pl.BlockSpec((tm, HEAD_DIM), lambda i: (i, 0))] * 2,
        out_shape=[jax.ShapeDtypeStruct((S, HEAD_DIM), F32)] * 2,
        compiler_params=_cp("parallel"),
        name="rope_tables",
    )(pos, inv_full)


def _ada_kernel(c_ref, w_ref, b_ref, o_ref):
    c = c_ref[...]
    w16 = w_ref[...].astype(BF16)
    o_ref[...] = sum(_dot(part, w16) for part in _split3(c * _sigmoid(c))) + b_ref[...]


def _adaln(c8, w_ada, b_ada, layer):
    D = D_MODEL
    tn = 1536
    return pl.pallas_call(
        _ada_kernel,
        grid=(6 * D // tn,),
        in_specs=[pl.BlockSpec((8, D), lambda n: (0, 0)),
                  pl.BlockSpec((None, D, tn), lambda n: (layer, 0, n)),
                  pl.BlockSpec((None, 1, tn), lambda n: (layer, 0, n))],
        out_specs=pl.BlockSpec((8, tn), lambda n: (0, n)),
        out_shape=jax.ShapeDtypeStruct((8, 6 * D), F32),
        compiler_params=_cp("parallel"),
        name="adaln",
    )(c8, w_ada, b_ada.reshape(DEPTH, 1, 6 * D))


def _normmod_kernel(x_ref, g_ref, sc_ref, sh_ref, o_ref):
    x = x_ref[...]
    y = x * lax.rsqrt(jnp.mean(x * x, axis=-1, keepdims=True) + NORM_EPS) * g_ref[...]
    o_ref[...] = (y * (1.0 + sc_ref[...]) + sh_ref[...]).astype(o_ref.dtype)


def _normmod(x, gain, scale, shift, out_dtype):
    S, D = x.shape
    tm = min(S, 512)
    row = pl.BlockSpec((1, D), lambda i: (0, 0))
    return pl.pallas_call(
        _normmod_kernel,
        grid=(S // tm,),
        in_specs=[pl.BlockSpec((tm, D), lambda i: (i, 0)), row, row, row],
        out_specs=pl.BlockSpec((tm, D), lambda i: (i, 0)),
        out_shape=jax.ShapeDtypeStruct((S, D), out_dtype),
        compiler_params=_cp("parallel"),
        name="normmod",
    )(x, gain, scale, shift)


def _mm_kernel(a_ref, b_ref, o_ref):
    o_ref[...] = _dot(a_ref[...], b_ref[...]).astype(o_ref.dtype)


def _mm(a, b, tm, tn, out_dtype, name):
    M, K = a.shape
    N = b.shape[1]
    tm, tn = min(tm, M), min(tn, N)
    return pl.pallas_call(
        _mm_kernel,
        grid=(N // tn, M // tm),
        in_specs=[pl.BlockSpec((tm, K), lambda n, m: (m, 0)), pl.BlockSpec((K, tn), lambda n, m: (0, n))],
        out_specs=pl.BlockSpec((tm, tn), lambda n, m: (m, n)),
        out_shape=jax.ShapeDtypeStruct((M, N), out_dtype),
        compiler_params=_cp("parallel", "parallel"),
        name=name,
    )(a, b)


def _norm_swiglu_kernel(x_ref, gn_ref, sc_ref, sh_ref, bg_ref, bu_ref, o_ref, h_ref):
    @pl.when(pl.program_id(1) == 0)
    def _():
        x = x_ref[...]
        y = x * lax.rsqrt(jnp.mean(x * x, axis=-1, keepdims=True) + NORM_EPS) * gn_ref[...]
        h_ref[...] = (y * (1.0 + sc_ref[...]) + sh_ref[...]).astype(h_ref.dtype)

    a = h_ref[...]
    g = _dot(a, bg_ref[...].astype(BF16))
    u = _dot(a, bu_ref[...].astype(BF16))
    o_ref[...] = (g * _sigmoid(g) * u).astype(o_ref.dtype)


def _norm_mm_swiglu(x, gain, scale, shift, w_stack, layer, tm, tn):
    M, K = x.shape
    H = w_stack.shape[2] // 2
    tm = min(tm, M)
    nb = H // tn
    row = pl.BlockSpec((1, K), lambda m, n: (0, 0))
    w = w_stack
    return pl.pallas_call(
        _norm_swiglu_kernel,
        grid=(M // tm, nb),
        in_specs=[pl.BlockSpec((tm, K), lambda m, n: (m, 0)), row, row, row,
                  pl.BlockSpec((None, K, tn), lambda m, n: (layer, 0, n)),
                  pl.BlockSpec((None, K, tn), lambda m, n: (layer, 0, nb + n))],
        out_specs=pl.BlockSpec((tm, tn), lambda m, n: (m, n)),
        out_shape=jax.ShapeDtypeStruct((M, H), BF16),
        scratch_shapes=[pltpu.VMEM((tm, K), BF16)],
        compiler_params=_cp("parallel", "arbitrary"),
        name="ffn_in_swiglu",
    )(x, gain, scale, shift, w, w)


def _resid_kernel(a_ref, b_ref, x_ref, g_ref, o_ref):
    o_ref[...] = x_ref[...] + g_ref[...] * _dot(a_ref[...], b_ref[...])


def _mm_resid(a, b, x, gate, tm, tn, name):
    M, K = a.shape
    N = b.shape[1]
    tm = min(tm, M)
    return pl.pallas_call(
        _resid_kernel,
        grid=(N // tn, M // tm),
        in_specs=[pl.BlockSpec((tm, K), lambda n, m: (m, 0)),
                  pl.BlockSpec((K, tn), lambda n, m: (0, n)),
                  pl.BlockSpec((tm, tn), lambda n, m: (m, n)),
                  pl.BlockSpec((1, tn), lambda n, m: (0, n))],
        out_specs=pl.BlockSpec((tm, tn), lambda n, m: (m, n)),
        out_shape=jax.ShapeDtypeStruct((M, N), F32),
        compiler_params=_cp("parallel", "parallel"),
        name=name,
    )(a, b, x, gate)


def _merge_kernel(h_ref, wg0, wg1, wg2, wg3, oa, ob, oc, od, wa, wb, wc, wd, o_ref):
    h = h_ref[...]
    acc = None
    for wg, o_br, w_br in ((wg0, oa, wa), (wg1, ob, wb), (wg2, oc, wc), (wg3, od, wd)):
        term = _sigmoid(_dot(h, wg[...].astype(BF16))) * _dot(o_br[...], w_br[...])
        acc = term if acc is None else acc + term
    o_ref[...] = acc.astype(o_ref.dtype)


def _merge(h, w_in, layer, branches, w_branches, tm, tn):
    M, D = h.shape
    tm = min(tm, M)
    nb = D // tn
    w_gate = w_in
    in_specs = [pl.BlockSpec((tm, D), lambda n, m: (m, 0))]
    for i in range(N_BRANCH):
        in_specs.append(pl.BlockSpec((None, D, tn), lambda n, m, i=i: (layer, 0, i * nb + n)))
    for br in branches:
        in_specs.append(pl.BlockSpec((tm, br.shape[1]), lambda n, m: (m, 0)))
    for wb in w_branches:
        in_specs.append(pl.BlockSpec((wb.shape[0], tn), lambda n, m: (0, n)))
    return pl.pallas_call(
        _merge_kernel,
        grid=(nb, M // tm),
        in_specs=in_specs,
        out_specs=pl.BlockSpec((tm, tn), lambda n, m: (m, n)),
        out_shape=jax.ShapeDtypeStruct((M, D), BF16),
        compiler_params=_cp("parallel", "parallel"),
        name="gated_merge",
    )(h, w_gate, w_gate, w_gate, w_gate, *branches, *w_branches)


def _gla_kernel(q_ref, k_ref, v_ref, g_ref, al_ref, wa_ref, ba_ref, gn_ref, lt_ref, blk_ref, e_ref, sel_ref,
                o_ref, st_ref, qe_ref, ke_ref, b_ref, dec_ref, acc_ref):
    C = GLA_SUB
    R = q_ref.shape[0]

    @pl.when(pl.program_id(0) == 0)
    def _():
        st_ref[...] = jnp.zeros_like(st_ref)

    gk = _log_sigmoid(_dot_hi(al_ref[...], wa_ref[...]) + ba_ref[...]) * (1.0 / GLA_NORMALIZER)
    gk_parts = _split3(gk)
    b = _group_rows(lt_ref[...], gk_parts)
    tot = _group_rows(blk_ref[...], gk_parts)
    scale = GLA_DK ** -0.5
    qe_ref[...] = (q_ref[...] * scale * jnp.exp(b)).astype(BF16)
    ke_ref[...] = (k_ref[...] * jnp.exp(tot - b)).astype(BF16)
    b_ref[...] = b
    dec_ref[...] = jnp.exp(tot)

    e_mat = e_ref[...]
    sel = sel_ref[...]
    s_idx = lax.broadcasted_iota(jnp.int32, (C, GLA_HEADS * GLA_DK), 0)

    K = GLA_SUBS_PER_TRIP
    HS = range(GLA_HEADS)
    kh = [slice(h * GLA_DK, (h + 1) * GLA_DK) for h in HS]
    vh = [slice(h * GLA_DV, (h + 1) * GLA_DV) for h in HS]

    def body(trip, carry):
        r0 = [pl.multiple_of((trip * K + i) * C, C) for i in range(K)]
        rows = [pl.ds(r, C) for r in r0]
        vs = [v_ref[r, :] for r in rows]
        p_st = []
        for i in range(K):
            qs = q_ref[rows[i], :] * scale
            ks = k_ref[rows[i], :]
            bs = b_ref[rows[i], :]
            pairs = []
            for t in range(C):
                e = jnp.exp(jnp.where(s_idx <= t, bs[t:t + 1, :] - bs, -jnp.inf))
                pairs.append(qs[t:t + 1, :] * ks * e)
            p_st.append(jnp.concatenate(pairs, axis=0).astype(BF16))
        sc = [_dot(p, e_mat) for p in p_st]
        w = [(sc[i] * jnp.concatenate([vs[i]] * C, axis=0)).astype(BF16) for i in range(K)]
        intra = [_dot(sel, x) for x in w]
        qe = [qe_ref[r, :] for r in rows]
        ke = [ke_ref[r, :] for r in rows]
        kv = [[_dot_tn(vs[i].astype(BF16)[:, vh[h]], ke[i][:, kh[h]]) for h in HS] for i in range(K)]
        st = [st_ref[h] for h in HS]
        for i in range(K):
            dec = dec_ref[pl.ds(r0[i], 1), :]
            outs = []
            for h in HS:
                outs.append(_dot_nt(qe[i][:, kh[h]], st[h].astype(BF16)))
                st[h] = st[h] * dec[:, kh[h]] + kv[i][h]
            acc_ref[rows[i], :] = jnp.concatenate(outs, axis=1) + intra[i]
        for h in HS:
            st_ref[h] = st[h]
        return carry

    lax.fori_loop(0, R // (C * K), body, 0)

    o = acc_ref[...]
    g = g_ref[...]
    gn = gn_ref[...]
    parts = []
    for h in range(GLA_HEADS):
        oh = o[:, h * GLA_DV:(h + 1) * GLA_DV]
        parts.append(oh * lax.rsqrt(jnp.mean(oh * oh, axis=-1, keepdims=True) + NORM_EPS) * gn)
    o_ref[...] = (jnp.concatenate(parts, axis=1) * (g * _sigmoid(g))).astype(o_ref.dtype)


def _gla(p, w_a2, b_a2, g_norm):
    S = p.shape[0]
    R = min(S, 512)
    C = GLA_SUB
    HK, HV = GLA_HEADS * GLA_DK, GLA_HEADS * GLA_DV
    wa = jnp.zeros((128, HK), F32).at[:GLA_RANK].set(w_a2)
    ri = np.arange(SLAB)
    same = (ri[:, None] // C) == (ri[None, :] // C)
    ltri = jnp.asarray((same & (ri[None, :] <= ri[:, None])).astype(np.float32), BF16)
    blk = jnp.asarray(same.astype(np.float32), BF16)
    e_mat =jnp.asarray((np.arange(HK)[:, None] // GLA_DK == np.arange(HV)[None, :] // GLA_DV).astype(np.float32), BF16)
    sel = jnp.asarray((np.arange(C)[:, None] == np.arange(C * C)[None, :] // C).astype(np.float32), BF16)
    const = lambda shape: pl.BlockSpec(shape, lambda i: (0, 0))
    return pl.pallas_call(
        _gla_kernel,
        grid=(S // R,),
        in_specs=[pl.BlockSpec((R, HK), lambda i: (i, 0)),
                  pl.BlockSpec((R, HK), lambda i: (i, 1)),
                  pl.BlockSpec((R, HV), lambda i: (i, 1)),
                  pl.BlockSpec((R, HV), lambda i: (i, 2)),
                  pl.BlockSpec((R, 128), lambda i: (i, (2 * HK + 2 * HV) // 128)),
                  const((128, HK)), const((1, HK)), const((1, GLA_DV)),
                  const((SLAB, SLAB)), const((SLAB, SLAB)), const((HK, HV)), const((C, C * C))],
        out_specs=pl.BlockSpec((R, HV), lambda i: (i, 0)),
        out_shape=jax.ShapeDtypeStruct((S, HV), BF16),
        scratch_shapes=[pltpu.VMEM((GLA_HEADS, GLA_DV, GLA_DK), F32),
                        pltpu.VMEM((R, HK), BF16), pltpu.VMEM((R, HK), BF16),
                        pltpu.VMEM((R, HK), F32), pltpu.VMEM((R, HK), F32),
                        pltpu.VMEM((R, HV), F32)],
        compiler_params=_cp("arbitrary"),
        name="gla",
    )(p, p, p, p, p, wa, b_a2[None, :], g_norm[None, :], ltri, blk, e_mat, sel)


def _band_kernel(dil, q_ref, k_ref, v_ref, kp_ref, vp_ref, c_ref, s_ref, cp_ref, sp_ref, o_ref, l_ref):
    t = pl.program_id(1)
    blk = DIL_BLK
    span = blk * dil
    qi = lax.broadcasted_iota(jnp.int32, (blk, 2 * blk), 0)
    ki = lax.broadcasted_iota(jnp.int32, (blk, 2 * blk), 1)
    dist = blk + qi - ki
    band = jnp.where(dist >= 0, jnp.where(dist <= blk, 1, 0), 0)
    band_first = band * jnp.where(ki >= blk, 1, jnp.where(t > 0, 1, 0))

    def rows_of(start):
        return pl.ds(start, blk, stride=dil) if dil > 1 else pl.ds(start, blk)

    for r in range(dil):
        kp = vp = None
        for b in range(DIL_TILE // span):
            rows = rows_of(b * span + r)
            cos_c, sin_c = c_ref[rows, :], s_ref[rows, :]
            q = _rope(q_ref[rows, :], cos_c, sin_c) * (HEAD_DIM ** -0.5)
            kc = _rope(k_ref[rows, :], cos_c, sin_c).astype(BF16)
            vc = v_ref[rows, :].astype(BF16)
            if b == 0:
                prow = rows_of(r)
                kp = _rope(kp_ref[prow, :], cp_ref[prow, :], sp_ref[prow, :]).astype(BF16)
                vp = vp_ref[prow, :].astype(BF16)
            s = _dot_nt(q.astype(BF16), jnp.concatenate([kp, kc], axis=0))
            s = jnp.where((band_first if b == 0 else band) > 0, s, -jnp.inf)
            m = jnp.max(s, axis=-1, keepdims=True)
            pr = jnp.exp(s - m)
            den = jnp.sum(pr, axis=-1, keepdims=True)
            o_ref[rows, :] = _dot(pr.astype(BF16), jnp.concatenate([vp, vc], axis=0)) / den
            l_ref[rows, :] = jnp.broadcast_to(m + jnp.log(den), (blk, HEAD_DIM))
            kp, vp = kc, vc


def _banded_group(p_dil, cos_f, sin_f, group, dil):
    S = p_dil.shape[0]
    hp = DIL_HEADS_PER_GROUP
    span = DIL_BLK * dil
    assert S % DIL_TILE == 0 and DIL_TILE % span == 0
    per_tile = DIL_TILE // span

    def col(part):
        return lambda h, t: (t, part * DIL_HEADS + group * hp + h)

    def col_prev(part):
        return lambda h, t: (jnp.maximum(t * per_tile - 1, 0), part * DIL_HEADS + group * hp + h)

    cur = (DIL_TILE, HEAD_DIM)
    prev = (span, HEAD_DIM)
    tab_c = lambda h, t: (t, 0)
    tab_p = lambda h, t: (jnp.maximum(t * per_tile - 1, 0), 0)
    out_map = lambda h, t: (t, h)
    o, l = pl.pallas_call(
        functools.partial(_band_kernel, dil),
        grid=(hp, S // DIL_TILE),
        in_specs=[pl.BlockSpec(cur, col(0)), pl.BlockSpec(cur, col(1)), pl.BlockSpec(cur, col(2)),
                  pl.BlockSpec(prev, col_prev(1)), pl.BlockSpec(prev, col_prev(2)),
                  pl.BlockSpec(cur, tab_c), pl.BlockSpec(cur, tab_c), pl.BlockSpec(prev, tab_p), pl.BlockSpec(prev, tab_p)],
        out_specs=[pl.BlockSpec(cur, out_map)] * 2,
        out_shape=[jax.ShapeDtypeStruct((S, DIL_OUT), F32)] * 2,
        compiler_params=_cp("parallel", "parallel"),
        name=f"dilated_band_d{dil}",
    )(p_dil, p_dil, p_dil, p_dil, p_dil, cos_f, sin_f, cos_f, sin_f)
    return o, l


def _dil_combine_kernel(o0, o1, o2, l0, l1, l2, out_ref):
    a, b, c = l0[...], l1[...], l2[...]
    m = jnp.maximum(jnp.maximum(a, b), c)
    ea, eb, ec = jnp.exp(a - m), jnp.exp(b - m), jnp.exp(c - m)
    out_ref[...] = ((ea * o0[...] + eb * o1[...] + ec * o2[...]) / (ea + eb + ec)).astype(out_ref.dtype)


def _dilated(p_dil, cos_f, sin_f):
    S = p_dil.shape[0]
    outs, lses = [], []
    for g, (window, dil) in enumerate(DIL_PAIRS):
        assert window // dil == DIL_BLK
        o, l = _banded_group(p_dil, cos_f, sin_f, g, dil)
        outs.append(o)
        lses.append(l)
    tm = min(S, 1024)
    spec = pl.BlockSpec((tm, DIL_OUT), lambda i: (i, 0))
    return pl.pallas_call(
        _dil_combine_kernel,
        grid=(S // tm,),
        in_specs=[spec] * 6,
        out_specs=spec,
        out_shape=jax.ShapeDtypeStruct((S, DIL_OUT), BF16),
        compiler_params=_cp("parallel"),
        name="dilated_combine",
    )(*outs, *lses)


def _rwkv_prep_kernel(has_vmix, *refs):
    if has_vmix:
        (p_ref, prev_ref, mu_ref, w0_ref, w2_ref, a0_ref, a2_ref, g2_ref, kk_ref, ka_ref, rk_ref, hd_ref,
         lt_ref, pick_ref, vf_ref, v0_ref, v1_ref, v2_ref,
         at_o, rt_o, bt_o, kt_o, v16_o, gc_o, bonus_o, g_o) = refs
        v_o = None
    else:
        (p_ref, prev_ref, mu_ref, w0_ref, w2_ref, a0_ref, a2_ref, g2_ref, kk_ref, ka_ref, rk_ref, hd_ref,
         lt_ref, pick_ref,
         at_o, rt_o, bt_o, kt_o, v16_o, gc_o, bonus_o, g_o, v_o) = refs
    W = RWKV_WIDTH
    p = p_ref[...]
    last = jnp.where(pl.program_id(0) == 0, 0.0, prev_ref[7:8, :])
    row = lax.broadcasted_iota(jnp.int32, p.shape, 0)
    p_prev = jnp.where(row == 0, last, pltpu.roll(p, 1, axis=0))
    xs = p + (p_prev - p) * mu_ref[...]
    r, k, v = xs[:, 0:W], xs[:, W:2 * W], xs[:, 2 * W:3 * W]
    w_low = xs[:, 3 * W:3 * W + LORA_PAD]
    a_low = xs[:, 3 * W + LORA_PAD:3 * W + 2 * LORA_PAD]
    g_low = xs[:, 3 * W + 2 * LORA_PAD:]
    lora = lambda t, w_ref: _dot(t.astype(BF16), w_ref[...].astype(BF16))
    log_w = _log_sigmoid(w0_ref[...] + lora(jnp.tanh(w_low), w2_ref)) - 0.5
    a = _sigmoid(a0_ref[...] + lora(a_low, a2_ref))
    g = lora(_sigmoid(g_low), g2_ref)
    if has_vmix:
        mix = _sigmoid(v0_ref[...] + lora(lora(v, v1_ref), v2_ref))
        v = v + (vf_ref[...] - v) * mix
    kk = k * kk_ref[...]
    kk = kk / jnp.maximum(jnp.sqrt(_group_sum_lanes(kk * kk, hd_ref[...])), 1e-12)
    k = k * (1.0 + (a - 1.0) * ka_ref[...])
    lw = -jnp.exp(log_w)
    lw_parts = _split3(lw)
    cum = _group_rows(lt_ref[...], lw_parts)
    e_in = jnp.exp(cum)
    e_inv = jnp.exp(-cum)
    at_o[...] = (-kk * jnp.exp(cum - lw)).astype(at_o.dtype)
    rt_o[...] = (r * e_in).astype(rt_o.dtype)
    bt_o[...] = (kk * a * e_inv).astype(bt_o.dtype)
    kt_o[...] = (k * e_inv).astype(kt_o.dtype)
    v16_o[...] = v.astype(v16_o.dtype)
    gc_o[...] = jnp.exp(sum(_dot(pick_ref[...], part) for part in lw_parts))
    bonus_o[...] = _group_sum_lanes(r * k * rk_ref[...], hd_ref[...]) * v
    g_o[...] = g
    if v_o is not None:
        v_o[...] = v


def _pad_rows(w, rows):
    return jnp.zeros((rows, w.shape[1]), w.dtype).at[:w.shape[0]].set(w)


def _pad_rwkv_cols(t):
    W = RWKV_WIDTH
    z = jnp.zeros(t.shape[:-1] + (LORA_PAD - RWKV_DECAY_LORA,), t.dtype)
    o1, o2 = 3 * W + RWKV_DECAY_LORA, 3 * W + RWKV_DECAY_LORA + RWKV_AAA_LORA
    return jnp.concatenate([t[..., :o1], z, t[..., o1:o2], z, t[..., o2:]], axis=-1)


def _head_indicator():
    i = np.arange(SLAB) // RWKV_HEAD_SIZE
    return jnp.asarray((i[:, None] == i[None, :]).astype(np.float32), BF16)


def _rwkv_prep(p, mu, w0, w2, a0, a2, g2, k_k, k_a, r_k, v_first, v_mix):
    S = p.shape[0]
    W = RWKV_WIDTH
    C = RWKV_CHUNK
    R = min(S, 512)
    has_vmix = v_mix is not None
    row = lambda n: pl.BlockSpec((1, n), lambda i: (0, 0))
    full = lambda a: pl.BlockSpec(a.shape, lambda i: (0, 0))
    tile = pl.BlockSpec((R, W), lambda i: (i, 0))
    w2p, a2p = _pad_rows(w2, LORA_PAD), _pad_rows(a2, LORA_PAD)
    hd = _head_indicator()
    ri = np.arange(SLAB)
    same = (ri[:, None] // C) == (ri[None, :] // C)
    ltri = jnp.asarray((same & (ri[None, :] <= ri[:, None])).astype(np.float32), BF16)
    pick = jnp.asarray((np.arange(R // C)[:, None] == np.arange(R)[None, :] // C).astype(np.float32), BF16)
    args = [p, p, _pad_rwkv_cols(mu)[None, :], w0[None, :], w2p, a0[None, :], a2p, g2, k_k[None, :], k_a[None, :],
            r_k[None, :], hd, ltri, pick]
    in_specs = [pl.BlockSpec((R, RWKV_IN_PAD), lambda i: (i, 0)),
                pl.BlockSpec((8, RWKV_IN_PAD), lambda i: (jnp.maximum(i * (R // 8) - 1, 0), 0)),
                row(RWKV_IN_PAD), row(W), full(w2p), row(W), full(a2p), full(g2), row(W), row(W), row(W),
                full(hd), full(ltri), full(pick)]
    if has_vmix:
        v0, v1, v2 = v_mix
        args += [v_first, v0[None, :], v1, v2]
        in_specs += [tile, row(W), full(v1), full(v2)]
    gc_tile = pl.BlockSpec((R // C, W), lambda i: (i, 0))
    out_specs = [tile] * 5 + [gc_tile, tile, tile]
    out_shape = ([jax.ShapeDtypeStruct((S, W), BF16)] * 5
                 + [jax.ShapeDtypeStruct((S // C, W), F32)] + [jax.ShapeDtypeStruct((S, W), F32)] * 2)
    if not has_vmix:
        out_specs.append(tile)
        out_shape.append(jax.ShapeDtypeStruct((S, W), F32))
    return pl.pallas_call(
        functools.partial(_rwkv_prep_kernel, has_vmix),
        grid=(S // R,),
        in_specs=in_specs,
        out_specs=out_specs,
        out_shape=out_shape,
        compiler_params=_cp("parallel"),
        name="rwkv_prep",
    )(*args)


def _rwkv_scan_kernel(at_ref, rt_ref, bt_ref, kt_ref, v_ref, gc_ref, mask_ref, eye_ref, y_ref, st_ref):
    C = RWKV_CHUNK
    N = RWKV_HEAD_SIZE
    n_chunks = at_ref.shape[0] // C

    @pl.when(pl.program_id(0) == 0)
    def _():
        st_ref[...] = jnp.zeros_like(st_ref)

    mask = mask_ref[...]
    eye = eye_ref[...]

    H = RWKV_HEADS
    K = RWKV_CHUNKS_PER_TRIP
    HS = range(H)
    zeros_cn = jnp.zeros((C, N), BF16)

    def body(trip, carry):
        units = []
        for kc in range(K):
            c = trip * K + kc
            rows = pl.ds(pl.multiple_of(c * C, C), C)
            tiles = [ref[rows, :] for ref in (at_ref, rt_ref, bt_ref, kt_ref, v_ref)]
            g_row = gc_ref[pl.ds(c, 1), :]
            for h in HS:
                s = slice(h * N, (h + 1) * N)
                units.append(tuple(t[:, s] for t in tiles) + (g_row[:, s],))
        U = range(len(units))
        a_t, r_t, b_t, k_t, vv, g_c = ([u[i] for u in units] for i in range(6))
        m_all = [_dot_nt(jnp.concatenate([a_t[u], r_t[u]], axis=0), jnp.concatenate([b_t[u], k_t[u]], axis=0)) * mask
                 for u in U]
        m16 = [m.astype(BF16) for m in m_all]
        m_ab16 = [m[:C, :C] for m in m16]
        t_inv = [eye + m[:C, :C] for m in m_all]
        x = [_dot(m, m) for m in m_ab16]
        for step in range(5):
            x16 = [v.astype(BF16) for v in x]
            t_inv = [t_inv[u] + _dot(x16[u], t_inv[u].astype(BF16)) for u in U]
            if step < 4:
                x = [_dot(v, v) for v in x16]
        t16 = [t.astype(BF16) for t in t_inv]
        mv = [_dot(m16[u], jnp.concatenate([zeros_cn, vv[u]], axis=0)) for u in U]
        p16 = [_dot(t16[u], a_t[u]).astype(BF16) for u in U]
        q16 = [_dot(t16[u], mv[u][:C].astype(BF16)).astype(BF16) for u in U]
        m_rb16 = [m[C:, :C] for m in m16]
        p2 = [r_t[u].astype(F32) + _dot(m_rb16[u], p16[u]) for u in U]
        q2 = [mv[u][C:] + _dot(m_rb16[u], q16[u]) for u in U]
        b_end = [(b_t[u] * g_c[u]).astype(BF16) for u in U]
        k_end = [(k_t[u] * g_c[u]).astype(BF16) for u in U]
        g_mat = [_dot_tn(b_end[u], p16[u]) + eye * g_c[u] for u in U]
        h_mat = [_dot_tn(jnp.concatenate([b_end[u], k_end[u]], axis=0), jnp.concatenate([q16[u], vv[u]], axis=0))
                 for u in U]
        pg16 = [jnp.concatenate([p2[u], g_mat[u]], axis=0).astype(BF16) for u in U]
        st = [st_ref[h] for h in HS]
        for kc in range(K):
            ys = []
            for h in HS:
                u = kc * H + h
                ys_st = _dot(pg16[u], st[h].astype(BF16))
                ys.append(ys_st[:C] + q2[u])
                st[h] = ys_st[C:] + h_mat[u]
            rows = pl.ds(pl.multiple_of((trip * K + kc) * C, C), C)
            y_ref[rows, :] = jnp.concatenate(ys, axis=1)
        for h in HS:
            st_ref[h] = st[h]
        return carry

    lax.fori_loop(0, n_chunks // K, body, 0)


def _rwkv_scan(at, rt, bt, kt, v16, gc):
    S, W = at.shape
    C = RWKV_CHUNK
    R = min(S, 512)
    ti = np.arange(C)
    strict = (ti[None, :] < ti[:, None]).astype(np.float32)
    incl = (ti[None, :] <= ti[:, None]).astype(np.float32)
    mask = jnp.asarray(np.block([[strict, strict], [incl, incl]]))
    eye = jnp.eye(C, dtype=F32)
    tile = pl.BlockSpec((R, W), lambda i: (i, 0))
    const = lambda a_: pl.BlockSpec(a_.shape, lambda i: (0, 0))
    return pl.pallas_call(
        _rwkv_scan_kernel,
        grid=(S // R,),
        in_specs=[tile] * 5 + [pl.BlockSpec((R // C, W), lambda i: (i, 0)), const(mask), const(eye)],
        out_specs=tile,
        out_shape=jax.ShapeDtypeStruct((S, W), F32),
        scratch_shapes=[pltpu.VMEM((RWKV_HEADS, RWKV_HEAD_SIZE, RWKV_HEAD_SIZE), F32)],
        compiler_params=_cp("arbitrary"),
        name="rwkv_scan",
    )(at, rt, bt, kt, v16, gc, mask, eye)


def _rwkv_post_kernel(y_ref, bonus_ref, g_ref, lw_ref, lb_ref, hd_ref, o_ref):
    hd = hd_ref[...]
    inv_n = 1.0 / RWKV_HEAD_SIZE
    y = y_ref[...]
    yc = y - _group_sum_lanes(y, hd) * inv_n
    var = _group_sum_lanes(yc * yc, hd) * inv_n
    yn = yc * lax.rsqrt(var + RWKV_LNX_EPS) * lw_ref[...] + lb_ref[...]
    o_ref[...] = ((yn + bonus_ref[...]) * g_ref[...]).astype(o_ref.dtype)


def _rwkv_post(y, bonus, g, lnx_w, lnx_b):
    S, W = y.shape
    R = min(S, 512)
    tile = pl.BlockSpec((R, W), lambda i: (i, 0))
    row = pl.BlockSpec((1, W), lambda i: (0, 0))
    hd = _head_indicator()
    return pl.pallas_call(
        _rwkv_post_kernel,
        grid=(S // R,),
        in_specs=[tile] * 3 + [row] * 2 + [pl.BlockSpec((SLAB, SLAB), lambda i: (0, 0))],
        out_specs=tile,
        out_shape=jax.ShapeDtypeStruct((S, W), BF16),
        compiler_params=_cp("parallel"),
        name="rwkv_post",
    )(y, bonus, g, lnx_w[None, :], lnx_b[None, :], hd)


def _rwkv(p, mu, w0, w2, a0, a2, g2, k_k, k_a, r_k, lnx_w, lnx_b, v_first, v_mix):
    outs = _rwkv_prep(p, mu, w0, w2, a0, a2, g2, k_k, k_a, r_k, v_first, v_mix)
    at, rt, bt, kt, v16, gc, bonus, g = outs[:8]
    y = _rwkv_scan(at, rt, bt, kt, v16, gc)
    out = _rwkv_post(y, bonus, g, lnx_w, lnx_b)
    return out, (outs[8] if v_mix is None else v_first)


def _moba_prep_kernel(k_ref, v_ref, c_ref, s_ref, kr_ref, vt_ref, km_ref):
    cos_f, sin_f = c_ref[...], s_ref[...]
    parts = []
    for h in range(MOBA_HEADS):
        parts.append(_rope(k_ref[:, h * HEAD_DIM:(h + 1) * HEAD_DIM], cos_f, sin_f))
    kr = jnp.concatenate(parts, axis=1)
    lane = lax.broadcasted_iota(jnp.int32, (k_ref.shape[0], HEAD_DIM), 1)
    tag = jnp.where(lane == pl.program_id(0), 1.0, 0.0).astype(BF16)
    kr_ref[...] = jnp.concatenate(
        [t for h in range(MOBA_HEADS) for t in (parts[h].astype(BF16), tag)], axis=1)
    vt = v_ref[...].T
    ones = jnp.where(lax.broadcasted_iota(jnp.int32, (MOBA_VROWS - HEAD_DIM, vt.shape[1]), 0) == 0, 1.0, 0.0)
    vt_ref[...] = jnp.concatenate(
        [t for h in range(MOBA_HEADS) for t in (vt[h * HEAD_DIM:(h + 1) * HEAD_DIM], ones)], axis=0).astype(BF16)
    km_ref[...] = jnp.broadcast_to(jnp.mean(kr, axis=0, keepdims=True), km_ref.shape)


def _moba_kernel(q_ref, c_ref, s_ref, k_ref, vt_ref, km_ref, o_ref):
    blk = MOBA_BLOCK
    nb = km_ref.shape[0]
    T = MOBA_TILES_PER_STEP
    cb0 = pl.program_id(1) * T
    bi = lax.broadcasted_iota(jnp.int32, (nb, blk), 0)
    ki = lax.broadcasted_iota(jnp.int32, (blk, blk), 0)
    qi = lax.broadcasted_iota(jnp.int32, (blk, blk), 1)

    def update(m, acc, s, vt):
        m_new = jnp.maximum(m, jnp.max(s, axis=0, keepdims=True))
        return m_new, jnp.exp(m - m_new) * acc + _dot(vt, jnp.exp(s - m_new).astype(BF16))

    chains = []
    for h in range(MOBA_HEADS_PER_STEP):
        kdim = slice(h * 2 * HEAD_DIM, h * 2 * HEAD_DIM + HEAD_DIM)
        vrow = slice(h * MOBA_VROWS, (h + 1) * MOBA_VROWS)
        for t in range(T):
            cb = cb0 + t
            qrows = slice(t * blk, (t + 1) * blk)
            q = _rope(q_ref[qrows, h * HEAD_DIM:(h + 1) * HEAD_DIM], c_ref[qrows, :], s_ref[qrows, :])
            qt = (q * (HEAD_DIM ** -0.5)).T
            gate = jnp.where(bi < cb, _dot_hi(km_ref[:, h * HEAD_DIM:(h + 1) * HEAD_DIM], qt), -jnp.inf)
            sel = jnp.zeros((nb, blk), F32)
            for _ in range(MOBA_TOPK):
                m = jnp.max(gate, axis=0, keepdims=True)
                idx = jnp.min(jnp.where(gate == m, bi, nb), axis=0, keepdims=True)
                hit = bi == idx
                sel = jnp.where(hit, jnp.where(m > -jnp.inf, 1.0, sel), sel)
                gate = jnp.where(hit, -jnp.inf, gate)
            qt16 = qt.astype(BF16)
            own = pl.ds(pl.multiple_of(cb * blk, blk), blk)
            s = jnp.where(ki <= qi, _dot(k_ref[own, kdim], qt16), -jnp.inf)
            m_run = jnp.max(s, axis=0, keepdims=True)
            acc = _dot(vt_ref[cb, vrow, :], jnp.exp(s - m_run).astype(BF16))
            for u in range(t):
                j = cb0 + u
                picked = jnp.sum(jnp.where(bi == j, sel, 0.0), axis=0, keepdims=True)
                rows = pl.ds(pl.multiple_of(j * blk, blk), blk)
                s = jnp.where(picked > 0.5, _dot(k_ref[rows, kdim], qt16), -jnp.inf)
                m_run, acc = update(m_run, acc, s, vt_ref[j, vrow, :])
            bias = (jnp.where(bi < cb0, 1.0 - sel, 1.0) * MOBA_MASKED).astype(BF16)
            qa16 = jnp.concatenate([qt16, bias, jnp.zeros((HEAD_DIM - nb, blk), BF16)], axis=0)
            chains.append(dict(kcol=slice(h * 2 * HEAD_DIM, (h + 1) * 2 * HEAD_DIM), vrow=vrow, qa16=qa16,
                               m=m_run, acc=acc))
    CH = range(len(chains))
    G = MOBA_GROUP

    def body(jg, carry):
        m, acc = list(carry[0]), list(carry[1])
        js = [jg * G + g for g in range(G)]

        def scores(j):
            rows = pl.ds(pl.multiple_of(j * blk, blk), blk)
            return [_dot(k_ref[rows, chains[c]["kcol"]], chains[c]["qa16"]) for c in CH]

        s = scores(js[0])
        for g in range(G):
            s_next = scores(js[g + 1]) if g + 1 < G else None
            for c in CH:
                m[c], acc[c] = update(m[c], acc[c], s[c], vt_ref[js[g], chains[c]["vrow"], :])
            s = s_next
        return tuple(m), tuple(acc)

    m, acc = lax.fori_loop(0, (cb0 + G - 1) // G, body,
                           (tuple(c["m"] for c in chains), tuple(c["acc"] for c in chains)))
    out = [(a[:HEAD_DIM] / a[HEAD_DIM:HEAD_DIM + 1]).T for a in acc]
    o_ref[...] = jnp.concatenate(
        [jnp.concatenate(out[h * T:(h + 1) * T], axis=0) for h in range(MOBA_HEADS_PER_STEP)], axis=1
    ).astype(o_ref.dtype)


def _moba(p, cos_f, sin_f):
    S = p.shape[0]
    blk = MOBA_BLOCK
    nb = S // blk
    assert nb <= HEAD_DIM, "block ids are one-hot encoded in HEAD_DIM lanes"
    assert nb % MOBA_GROUP == 0, "the block loop over-runs to a multiple of MOBA_GROUP"
    HW = MOBA_HEADS * HEAD_DIM
    kr, vt, km = pl.pallas_call(
        _moba_prep_kernel,
        grid=(nb,),
        in_specs=[pl.BlockSpec((blk, HW), lambda i: (i, 1)), pl.BlockSpec((blk, HW), lambda i: (i, 2)),
                  pl.BlockSpec((blk, HEAD_DIM), lambda i: (i, 0)), pl.BlockSpec((blk, HEAD_DIM), lambda i: (i, 0))],
        out_specs=[pl.BlockSpec((blk, 2 * HW), lambda i: (i, 0)),
                   pl.BlockSpec((None, MOBA_HEADS * MOBA_VROWS, blk), lambda i: (i, 0, 0)),
                   pl.BlockSpec((8, HW), lambda i: (i, 0))],
        out_shape=[jax.ShapeDtypeStruct((S, 2 * HW), BF16),
                   jax.ShapeDtypeStruct((nb, MOBA_HEADS * MOBA_VROWS, blk), BF16),
                   jax.ShapeDtypeStruct((nb * 8, HW), F32)],
        compiler_params=_cp("parallel"),
        name="moba_prep",
    )(p, p, cos_f, sin_f)
    k_mean = km.reshape(nb, 8, HW)[:, 0]
    P = MOBA_HEADS_PER_STEP
    T = MOBA_TILES_PER_STEP
    assert nb % T == 0
    tab = pl.BlockSpec((T * blk, HEAD_DIM), lambda h, cb: (cb, 0))
    once = pl.Buffered(1)
    return pl.pallas_call(
        _moba_kernel,
        grid=(MOBA_HEADS // P, nb // T),
        in_specs=[pl.BlockSpec((T * blk, P * HEAD_DIM), lambda h, cb: (cb, h)), tab, tab,
                  pl.BlockSpec((S, P * 2 * HEAD_DIM), lambda h, cb: (0, h), pipeline_mode=once),
                  pl.BlockSpec((nb, P * MOBA_VROWS, blk), lambda h, cb: (0, h, 0), pipeline_mode=once),
                  pl.BlockSpec((nb, P * HEAD_DIM), lambda h, cb: (0, h))],
        out_specs=pl.BlockSpec((T * blk, P * HEAD_DIM), lambda h, cb: (cb, h)),
        out_shape=jax.ShapeDtypeStruct((S, HW), BF16),
        compiler_params=_cp("parallel", "arbitrary"),
        name="moba",
    )(p, cos_f, sin_f, kr, vt, k_mean)


def _split_w_in(w_in_l):
    o = np.cumsum((N_BRANCH * D_MODEL, GLA_IN, DIL_IN, RWKV_IN, MOBA_IN))
    w_gla, w_dil, w_rwkv, w_moba = (w_in_l[:, a:b] for a, b in zip(o[:-1], o[1:]))
    w_gla = jnp.concatenate([w_gla, jnp.zeros((D_MODEL, GLA_IN_PAD - GLA_IN), w_gla.dtype)], axis=1)
    w_rwkv = _pad_rwkv_cols(w_rwkv)
    return tuple(w.astype(BF16) for w in (w_gla, w_dil, w_rwkv, w_moba))


def kernel(x, c, positions, w_ada, b_ada, norm1, w_in, gla_w_a2, gla_b_a2, gla_gnorm, rwkv_mu, rwkv_w0, rwkv_w2, rwkv_a0, rwkv_a2, rwkv_g2, rwkv_k_k, rwkv_k_a, rwkv_r_k, rwkv_lnx_w, rwkv_lnx_b, rwkv_v0, rwkv_v1, rwkv_v2, w_branch_a, w_branch_b, w_branch_c, w_branch_d, w_out, norm2, w_ffn_in, w_ffn_out, norm_f):
    B, S, D = x.shape
    assert B == 1 and D == D_MODEL
    xs = x.reshape(S, D)
    cos_f, sin_f = _rope_tables(positions)
    c8 = jnp.broadcast_to(c, (8, D))
    v_first = None
    for l in range(DEPTH):
        mod = _adaln(c8, w_ada, b_ada, l)[0:1]
        shift1, scale1, gate1, shift2, scale2, gate2 = jnp.split(mod, 6, axis=-1)

        h = _normmod(xs, norm1[l][None, :], scale1, shift1, BF16)
        w_gla, w_dil, w_rwkv, w_moba = _split_w_in(w_in[l])
        p_gla = _mm(h, w_gla, 512, GLA_IN_PAD, F32, "proj_gla")
        p_dil = _mm(h, w_dil, 512, DIL_IN, F32, "proj_dil")
        p_rwkv = _mm(h, w_rwkv, 512, RWKV_IN_PAD, F32, "proj_rwkv")
        p_moba = _mm(h, w_moba, 512, MOBA_IN, F32, "proj_moba")

        o_gla = _gla(p_gla, gla_w_a2[l], gla_b_a2[l], gla_gnorm[l])
        o_dil = _dilated(p_dil, cos_f, sin_f)
        v_mix = None if l == 0 else (rwkv_v0[l - 1], rwkv_v1[l - 1], rwkv_v2[l - 1])
        o_rwkv, v_first = _rwkv(p_rwkv, rwkv_mu[l], rwkv_w0[l], rwkv_w2[l], rwkv_a0[l], rwkv_a2[l], rwkv_g2[l],
                                rwkv_k_k[l], rwkv_k_a[l], rwkv_r_k[l], rwkv_lnx_w[l], rwkv_lnx_b[l], v_first, v_mix)
        o_moba = _moba(p_moba, cos_f, sin_f)

        merged = _merge(h, w_in, l, (o_gla, o_dil, o_rwkv, o_moba),
                        tuple(w[l].astype(BF16) for w in (w_branch_a, w_branch_b, w_branch_c, w_branch_d)), 1024, 256)
        xs = _mm_resid(merged, w_out[l].astype(BF16), xs, gate1, 1024, 1024, "out_proj_resid")

        act = _norm_mm_swiglu(xs, norm2[l][None, :], scale2, shift2, w_ffn_in, l, 1024, 512)
        xs = _mm_resid(act, w_ffn_out[l].astype(BF16), xs, gate2, 1024, 512, "ffn_out_resid")

    zero = jnp.zeros((1, D), F32)
    return _normmod(xs, norm_f[None, :], zero, zero, F32).reshape(B, S, D)
```

```python
import functools

import numpy as np
import jax
import jax.numpy as jnp
from jax import lax
from jax.experimental import pallas as pl
from jax.experimental.pallas import tpu as pltpu

F32 = jnp.float32
BF16 = jnp.bfloat16
HI = lax.Precision.HIGHEST

D_MODEL = 2048
DEPTH = 2
HEAD_DIM = 128
ROT_DIM = HEAD_DIM // 4
ROPE_THETA = 500000.0
N_BRANCH = 4
NORM_EPS = 1e-6

GLA_HEADS = 4
GLA_DK = 64
GLA_DV = 128
GLA_RANK = 16
GLA_NORMALIZER = 16.0
GLA_SUB = 16
GLA_SUBS_PER_TRIP = 8

DIL_PAIRS = ((128, 1), (512, 4), (2048, 16))
DIL_HEADS_PER_GROUP = 2
DIL_HEADS = DIL_HEADS_PER_GROUP * len(DIL_PAIRS)
DIL_BLK = 128
DIL_TILE = DIL_BLK * max(d for _, d in DIL_PAIRS)

RWKV_HEAD_SIZE = 64
RWKV_HEADS = 8
RWKV_WIDTH = RWKV_HEADS * RWKV_HEAD_SIZE
RWKV_DECAY_LORA = 96
RWKV_AAA_LORA = 96
RWKV_MV_LORA = 64
RWKV_GATE_LORA = 256
RWKV_LNX_EPS = 64e-5
RWKV_CHUNK = 64
RWKV_CHUNKS_PER_TRIP = 4
LORA_PAD = 128

MOBA_HEADS = 4
MOBA_BLOCK = 256
MOBA_TOPK = 3
MOBA_MASKED = -1e30
MOBA_VROWS = HEAD_DIM + 16
MOBA_HEADS_PER_STEP = 2
MOBA_MAX_GAP = 60.0
MOBA_CEIL_SLACK = 1.0 + 2.0 ** -7
MOBA_TILES_PER_STEP = 2
MOBA_GROUP = 4

FFN_HIDDEN = -(-8 * D_MODEL // (3 * 256)) * 256

GLA_SIZES = (GLA_HEADS * GLA_DK, GLA_HEADS * GLA_DK, GLA_HEADS * GLA_DV, GLA_HEADS * GLA_DV, GLA_RANK)
RWKV_SIZES = (RWKV_WIDTH, RWKV_WIDTH, RWKV_WIDTH, RWKV_DECAY_LORA, RWKV_AAA_LORA, RWKV_GATE_LORA)
GLA_IN = sum(GLA_SIZES)
DIL_IN = 3 * DIL_HEADS * HEAD_DIM
RWKV_IN = sum(RWKV_SIZES)
MOBA_IN = 3 * MOBA_HEADS * HEAD_DIM
GLA_OUT = GLA_HEADS * GLA_DV
DIL_OUT = DIL_HEADS_PER_GROUP * HEAD_DIM
RWKV_OUT = RWKV_WIDTH
MOBA_OUT = MOBA_HEADS * HEAD_DIM

GLA_IN_PAD = GLA_IN - GLA_RANK + 128
RWKV_IN_PAD = 3 * RWKV_WIDTH + 2 * LORA_PAD + RWKV_GATE_LORA

VMEM_LIMIT_BYTES = 56 * 1024 * 1024
SLAB = 128


def _cp(*sem):
    return pltpu.CompilerParams(dimension_semantics=sem, vmem_limit_bytes=VMEM_LIMIT_BYTES)


def _dot(a, b):
    return jnp.dot(a, b, preferred_element_type=F32)


def _dot_hi(a, b):
    return jnp.dot(a, b, precision=HI, preferred_element_type=F32)


def _split3(x):
    x1 = x.astype(BF16)
    r1 = x - x1.astype(F32)
    x2 = r1.astype(BF16)
    x3 = (r1 - x2.astype(F32)).astype(BF16)
    return x1, x2, x3


def _dot_sel_l(sel16, x):
    x1, x2, x3 = _split3(x)
    return _dot(sel16, x1) + _dot(sel16, x2) + _dot(sel16, x3)


def _dot_sel_r(x, sel16):
    x1, x2, x3 = _split3(x)
    return _dot(x1, sel16) + _dot(x2, sel16) + _dot(x3, sel16)


def _group_sum_lanes(x, ind16):
    n = ind16.shape[0]
    return jnp.concatenate([_dot_sel_r(x[:, i:i + n], ind16) for i in range(0, x.shape[1], n)], axis=1)


def _group_rows(sel16, parts):
    n = sel16.shape[1]
    slabs = range(0, parts[0].shape[0], n)
    return jnp.concatenate([sum(_dot(sel16, p[i:i + n]) for p in parts) for i in slabs], axis=0)


def _dot_nt(a, b, precision=None):
    return lax.dot_general(a, b, (((1,), (1,)), ((), ())), precision=precision, preferred_element_type=F32)


def _dot_tn(a, b, precision=None):
    return lax.dot_general(a, b, (((0,), (0,)), ((), ())), precision=precision, preferred_element_type=F32)


def _log_sigmoid(z):
    return jnp.minimum(z, 0.0) - jnp.log1p(jnp.exp(-jnp.abs(z)))


def _sigmoid(z):
    return 1.0 / (1.0 + jnp.exp(-z))


def _rope(x, cos_f, sin_f):
    half = ROT_DIM // 2
    lane = lax.broadcasted_iota(jnp.int32, x.shape, 1)
    partner = jnp.where(lane < half, pltpu.roll(x, HEAD_DIM - half, axis=1), pltpu.roll(x, half, axis=1))
    return x * cos_f + partner * sin_f


def _rope_table_kernel(pos_ref, inv_ref, cos_ref, sin_ref):
    ang = pos_ref[...] * inv_ref[...]
    lane = lax.broadcasted_iota(jnp.int32, ang.shape, 1)
    half = ROT_DIM // 2
    c = jnp.cos(ang)
    s = jnp.sin(ang)
    cos_ref[...] = jnp.where(lane < ROT_DIM, c, 1.0)
    sin_ref[...] = jnp.where(lane < half, -s, jnp.where(lane < ROT_DIM, s, 0.0))


def _rope_tables(positions):
    S = positions.shape[1]
    inv = ROPE_THETA ** (-jnp.arange(0, ROT_DIM, 2, dtype=F32) / ROT_DIM)
    inv_full = jnp.concatenate([inv, inv, jnp.zeros((HEAD_DIM - ROT_DIM,), F32)])[None, :]
    pos = positions.astype(F32).reshape(S, 1)
    tm = min(S, 1024)
    return pl.pallas_call(
        _rope_table_kernel,
        grid=(S // tm,),
        in_specs=[pl.BlockSpec((tm, 1), lambda i: (i, 0)), pl.BlockSpec((1, HEAD_DIM), lambda i: (0, 0))],
        out_specs=[pl.BlockSpec((tm, HEAD_DIM), lambda i: (i, 0))] * 2,
        out_shape=[jax.ShapeDtypeStruct((S, HEAD_DIM), F32)] * 2,
        compiler_params=_cp("parallel"),
        name="rope_tables",
    )(pos, inv_full)


def _ada_kernel(c_ref, w_ref, b_ref, o_ref):
    c = c_ref[...]
    w16 = w_ref[...].astype(BF16)
    o_ref[...] = sum(_dot(part, w16) for part in _split3(c * _sigmoid(c))) + b_ref[...]


def _adaln(c8, w_ada, b_ada, layer):
    D = D_MODEL
    tn = 1536
    return pl.pallas_call(
        _ada_kernel,
        grid=(6 * D // tn,),
        in_specs=[pl.BlockSpec((8, D), lambda n: (0, 0)),
                  pl.BlockSpec((None, D, tn), lambda n: (layer, 0, n)),
                  pl.BlockSpec((None, 1, tn), lambda n: (layer, 0, n))],
        out_specs=pl.BlockSpec((8, tn), lambda n: (0, n)),
        out_shape=jax.ShapeDtypeStruct((8, 6 * D), F32),
        compiler_params=_cp("parallel"),
        name="adaln",
    )(c8, w_ada, b_ada.reshape(DEPTH, 1, 6 * D))


def _normmod_kernel(x_ref, g_ref, sc_ref, sh_ref, o_ref):
    x = x_ref[...]
    y = x * lax.rsqrt(jnp.mean(x * x, axis=-1, keepdims=True) + NORM_EPS) * g_ref[...]
    o_ref[...] = (y * (1.0 + sc_ref[...]) + sh_ref[...]).astype(o_ref.dtype)


def _normmod(x, gain, scale, shift, out_dtype):
    S, D = x.shape
    tm = min(S, 512)
    row = pl.BlockSpec((1, D), lambda i: (0, 0))
    return pl.pallas_call(
        _normmod_kernel,
        grid=(S // tm,),
        in_specs=[pl.BlockSpec((tm, D), lambda i: (i, 0)), row, row, row],
        out_specs=pl.BlockSpec((tm, D), lambda i: (i, 0)),
        out_shape=jax.ShapeDtypeStruct((S, D), out_dtype),
        compiler_params=_cp("parallel"),
        name="normmod",
    )(x, gain, scale, shift)


def _mm_kernel(a_ref, b_ref, o_ref):
    o_ref[...] = _dot(a_ref[...], b_ref[...]).astype(o_ref.dtype)


def _mm(a, b, tm, tn, out_dtype, name):
    M, K = a.shape
    N = b.shape[1]
    tm, tn = min(tm, M), min(tn, N)
    return pl.pallas_call(
        _mm_kernel,
        grid=(N // tn, M // tm),
        in_specs=[pl.BlockSpec((tm, K), lambda n, m: (m, 0)), pl.BlockSpec((K, tn), lambda n, m: (0, n))],
        out_specs=pl.BlockSpec((tm, tn), lambda n, m: (m, n)),
        out_shape=jax.ShapeDtypeStruct((M, N), out_dtype),
        compiler_params=_cp("parallel", "parallel"),
        name=name,
    )(a, b)


def _norm_swiglu_kernel(x_ref, gn_ref, sc_ref, sh_ref, bg_ref, bu_ref, o_ref, h_ref):
    @pl.when(pl.program_id(1) == 0)
    def _():
        x = x_ref[...]
        y = x * lax.rsqrt(jnp.mean(x * x, axis=-1, keepdims=True) + NORM_EPS) * gn_ref[...]
        h_ref[...] = (y * (1.0 + sc_ref[...]) + sh_ref[...]).astype(h_ref.dtype)

    a = h_ref[...]
    g = _dot(a, bg_ref[...].astype(BF16))
    u = _dot(a, bu_ref[...].astype(BF16))
    o_ref[...] = (g * _sigmoid(g) * u).astype(o_ref.dtype)


def _norm_mm_swiglu(x, gain, scale, shift, w_stack, layer, tm, tn):
    M, K = x.shape
    H = w_stack.shape[2] // 2
    tm = min(tm, M)
    nb = H // tn
    row = pl.BlockSpec((1, K), lambda m, n: (0, 0))
    w = w_stack
    return pl.pallas_call(
        _norm_swiglu_kernel,
        grid=(M // tm, nb),
        in_specs=[pl.BlockSpec((tm, K), lambda m, n: (m, 0)), row, row, row,
                  pl.BlockSpec((None, K, tn), lambda m, n: (layer, 0, n)),
                  pl.BlockSpec((None, K, tn), lambda m, n: (layer, 0, nb + n))],
        out_specs=pl.BlockSpec((tm, tn), lambda m, n: (m, n)),
        out_shape=jax.ShapeDtypeStruct((M, H), BF16),
        scratch_shapes=[pltpu.VMEM((tm, K), BF16)],
        compiler_params=_cp("parallel", "arbitrary"),
        name="ffn_in_swiglu",
    )(x, gain, scale, shift, w, w)


def _resid_kernel(a_ref, b_ref, x_ref, g_ref, o_ref):
    o_ref[...] = x_ref[...] + g_ref[...] * _dot(a_ref[...], b_ref[...])


def _mm_resid(a, b, x, gate, tm, tn, name):
    M, K = a.shape
    N = b.shape[1]
    tm = min(tm, M)
    return pl.pallas_call(
        _resid_kernel,
        grid=(N // tn, M // tm),
        in_specs=[pl.BlockSpec((tm, K), lambda n, m: (m, 0)),
                  pl.BlockSpec((K, tn), lambda n, m: (0, n)),
                  pl.BlockSpec((tm, tn), lambda n, m: (m, n)),
                  pl.BlockSpec((1, tn), lambda n, m: (0, n))],
        out_specs=pl.BlockSpec((tm, tn), lambda n, m: (m, n)),
        out_shape=jax.ShapeDtypeStruct((M, N), F32),
        compiler_params=_cp("parallel", "parallel"),
        name=name,
    )(a, b, x, gate)


def _merge_kernel(h_ref, wg0, wg1, wg2, wg3, oa, ob, oc, od, wa, wb, wc, wd, o_ref):
    h = h_ref[...]
    acc = None
    for wg, o_br, w_br in ((wg0, oa, wa), (wg1, ob, wb), (wg2, oc, wc), (wg3, od, wd)):
        term = _sigmoid(_dot(h, wg[...])) * _dot(o_br[...], w_br[...])
        acc = term if acc is None else acc + term
    o_ref[...] = acc.astype(o_ref.dtype)


def _merge(h, w_gate, branches, w_branches, tm, tn):
    M, D = h.shape
    tm = min(tm, M)
    nb = D // tn
    in_specs = [pl.BlockSpec((tm, D), lambda n, m: (m, 0))]
    for i in range(N_BRANCH):
        in_specs.append(pl.BlockSpec((D, tn), lambda n, m, i=i: (0, i * nb + n)))
    for br in branches:
        in_specs.append(pl.BlockSpec((tm, br.shape[1]), lambda n, m: (m, 0)))
    for wb in w_branches:
        in_specs.append(pl.BlockSpec((wb.shape[0], tn), lambda n, m: (0, n)))
    return pl.pallas_call(
        _merge_kernel,
        grid=(nb, M // tm),
        in_specs=in_specs,
        out_specs=pl.BlockSpec((tm, tn), lambda n, m: (m, n)),
        out_shape=jax.ShapeDtypeStruct((M, D), BF16),
        compiler_params=_cp("parallel", "parallel"),
        name="gated_merge",
    )(h, w_gate, w_gate, w_gate, w_gate, *branches, *w_branches)


def _gla_kernel(q_ref, k_ref, v_ref, g_ref, al_ref, wa_ref, ba_ref, gn_ref, lt_ref, blk_ref, e_ref, sel_ref,
                o_ref, st_ref, qe_ref, ke_ref, b_ref, dec_ref, acc_ref):
    C = GLA_SUB
    R = q_ref.shape[0]

    @pl.when(pl.program_id(0) == 0)
    def _():
        st_ref[...] = jnp.zeros_like(st_ref)

    gk = _log_sigmoid(_dot_hi(al_ref[...], wa_ref[...]) + ba_ref[...]) * (1.0 / GLA_NORMALIZER)
    gk_parts = _split3(gk)
    b = _group_rows(lt_ref[...], gk_parts)
    tot = _group_rows(blk_ref[...], gk_parts)
    scale = GLA_DK ** -0.5
    qe_ref[...] = (q_ref[...] * scale * jnp.exp(b)).astype(BF16)
    ke_ref[...] = (k_ref[...] * jnp.exp(tot - b)).astype(BF16)
    b_ref[...] = b
    dec_ref[...] = jnp.exp(tot)

    e_mat = e_ref[...]
    sel = sel_ref[...]
    s_idx = lax.broadcasted_iota(jnp.int32, (C, GLA_HEADS * GLA_DK), 0)

    K = GLA_SUBS_PER_TRIP
    HS = range(GLA_HEADS)
    kh = [slice(h * GLA_DK, (h + 1) * GLA_DK) for h in HS]
    vh = [slice(h * GLA_DV, (h + 1) * GLA_DV) for h in HS]

    def body(trip, carry):
        r0 = [pl.multiple_of((trip * K + i) * C, C) for i in range(K)]
        rows = [pl.ds(r, C) for r in r0]
        vs = [v_ref[r, :] for r in rows]
        p_st = []
        for i in range(K):
            qs = q_ref[rows[i], :] * scale
            ks = k_ref[rows[i], :]
            bs = b_ref[rows[i], :]
            pairs = []
            for t in range(C):
                e = jnp.exp(jnp.where(s_idx <= t, bs[t:t + 1, :] - bs, -jnp.inf))
                pairs.append(qs[t:t + 1, :] * ks * e)
            p_st.append(jnp.concatenate(pairs, axis=0).astype(BF16))
        sc = [_dot(p, e_mat) for p in p_st]
        w = [(sc[i] * jnp.concatenate([vs[i]] * C, axis=0)).astype(BF16) for i in range(K)]
        intra = [_dot(sel, x) for x in w]
        qe = [qe_ref[r, :] for r in rows]
        ke = [ke_ref[r, :] for r in rows]
        kv = [[_dot_tn(vs[i].astype(BF16)[:, vh[h]], ke[i][:, kh[h]]) for h in HS] for i in range(K)]
        st = [st_ref[h] for h in HS]
        for i in range(K):
            dec = dec_ref[pl.ds(r0[i], 1), :]
            outs = []
            for h in HS:
                outs.append(_dot_nt(qe[i][:, kh[h]], st[h].astype(BF16)))
                st[h] = st[h] * dec[:, kh[h]] + kv[i][h]
            acc_ref[rows[i], :] = jnp.concatenate(outs, axis=1) + intra[i]
        for h in HS:
            st_ref[h] = st[h]
        return carry

    lax.fori_loop(0, R // (C * K), body, 0)

    o = acc_ref[...]
    g = g_ref[...]
    gn = gn_ref[...]
    parts = []
    for h in range(GLA_HEADS):
        oh = o[:, h * GLA_DV:(h + 1) * GLA_DV]
        parts.append(oh * lax.rsqrt(jnp.mean(oh * oh, axis=-1, keepdims=True) + NORM_EPS) * gn)
    o_ref[...] = (jnp.concatenate(parts, axis=1) * (g * _sigmoid(g))).astype(o_ref.dtype)


def _gla(p, w_a2, b_a2, g_norm):
    S = p.shape[0]
    R = min(S, 512)
    C = GLA_SUB
    HK, HV = GLA_HEADS * GLA_DK, GLA_HEADS * GLA_DV
    wa = jnp.zeros((128, HK), F32).at[:GLA_RANK].set(w_a2)
    ri = np.arange(SLAB)
    same = (ri[:, None] // C) == (ri[None, :] // C)
    ltri = jnp.asarray((same & (ri[None, :] <= ri[:, None])).astype(np.float32), BF16)
    blk = jnp.asarray(same.astype(np.float32), BF16)
    e_mat =jnp.asarray((np.arange(HK)[:, None] // GLA_DK == np.arange(HV)[None, :] // GLA_DV).astype(np.float32), BF16)
    sel = jnp.asarray((np.arange(C)[:, None] == np.arange(C * C)[None, :] // C).astype(np.float32), BF16)
    const = lambda shape: pl.BlockSpec(shape, lambda i: (0, 0))
    return pl.pallas_call(
        _gla_kernel,
        grid=(S // R,),
        in_specs=[pl.BlockSpec((R, HK), lambda i: (i, 0)),
                  pl.BlockSpec((R, HK), lambda i: (i, 1)),
                  pl.BlockSpec((R, HV), lambda i: (i, 1)),
                  pl.BlockSpec((R, HV), lambda i: (i, 2)),
                  pl.BlockSpec((R, 128), lambda i: (i, (2 * HK + 2 * HV) // 128)),
                  const((128, HK)), const((1, HK)), const((1, GLA_DV)),
                  const((SLAB, SLAB)), const((SLAB, SLAB)), const((HK, HV)), const((C, C * C))],
        out_specs=pl.BlockSpec((R, HV), lambda i: (i, 0)),
        out_shape=jax.ShapeDtypeStruct((S, HV), BF16),
        scratch_shapes=[pltpu.VMEM((GLA_HEADS, GLA_DV, GLA_DK), F32),
                        pltpu.VMEM((R, HK), BF16), pltpu.VMEM((R, HK), BF16),
                        pltpu.VMEM((R, HK), F32), pltpu.VMEM((R, HK), F32),
                        pltpu.VMEM((R, HV), F32)],
        compiler_params=_cp("arbitrary"),
        name="gla",
    )(p, p, p, p, p, wa, b_a2[None, :], g_norm[None, :], ltri, blk, e_mat, sel)


def _band_kernel(dil, q_ref, k_ref, v_ref, kp_ref, vp_ref, c_ref, s_ref, cp_ref, sp_ref, o_ref, l_ref):
    t = pl.program_id(1)
    blk = DIL_BLK
    span = blk * dil
    qi = lax.broadcasted_iota(jnp.int32, (blk, 2 * blk), 0)
    ki = lax.broadcasted_iota(jnp.int32, (blk, 2 * blk), 1)
    dist = blk + qi - ki
    band = jnp.where(dist >= 0, jnp.where(dist <= blk, 1, 0), 0)
    band_first = band * jnp.where(ki >= blk, 1, jnp.where(t > 0, 1, 0))

    def rows_of(start):
        return pl.ds(start, blk, stride=dil) if dil > 1 else pl.ds(start, blk)

    for r in range(dil):
        kp = vp = None
        for b in range(DIL_TILE // span):
            rows = rows_of(b * span + r)
            cos_c, sin_c = c_ref[rows, :], s_ref[rows, :]
            q = _rope(q_ref[rows, :], cos_c, sin_c) * (HEAD_DIM ** -0.5)
            kc = _rope(k_ref[rows, :], cos_c, sin_c).astype(BF16)
            vc = v_ref[rows, :].astype(BF16)
            if b == 0:
                prow = rows_of(r)
                kp = _rope(kp_ref[prow, :], cp_ref[prow, :], sp_ref[prow, :]).astype(BF16)
                vp = vp_ref[prow, :].astype(BF16)
            s = _dot_nt(q.astype(BF16), jnp.concatenate([kp, kc], axis=0))
            s = jnp.where((band_first if b == 0 else band) > 0, s, -jnp.inf)
            m = jnp.max(s, axis=-1, keepdims=True)
            pr = jnp.exp(s - m)
            den = jnp.sum(pr, axis=-1, keepdims=True)
            o_ref[rows, :] = _dot(pr.astype(BF16), jnp.concatenate([vp, vc], axis=0)) / den
            l_ref[rows, :] = jnp.broadcast_to(m + jnp.log(den), (blk, HEAD_DIM))
            kp, vp = kc, vc


def _banded_group(p_dil, cos_f, sin_f, group, dil):
    S = p_dil.shape[0]
    hp = DIL_HEADS_PER_GROUP
    span = DIL_BLK * dil
    assert S % DIL_TILE == 0 and DIL_TILE % span == 0
    per_tile = DIL_TILE // span

    def col(part):
        return lambda h, t: (t, part * DIL_HEADS + group * hp + h)

    def col_prev(part):
        return lambda h, t: (jnp.maximum(t * per_tile - 1, 0), part * DIL_HEADS + group * hp + h)

    cur = (DIL_TILE, HEAD_DIM)
    prev = (span, HEAD_DIM)
    tab_c = lambda h, t: (t, 0)
    tab_p = lambda h, t: (jnp.maximum(t * per_tile - 1, 0), 0)
    out_map = lambda h, t: (t, h)
    o, l = pl.pallas_call(
        functools.partial(_band_kernel, dil),
        grid=(hp, S // DIL_TILE),
        in_specs=[pl.BlockSpec(cur, col(0)), pl.BlockSpec(cur, col(1)), pl.BlockSpec(cur, col(2)),
                  pl.BlockSpec(prev, col_prev(1)), pl.BlockSpec(prev, col_prev(2)),
                  pl.BlockSpec(cur, tab_c), pl.BlockSpec(cur, tab_c), pl.BlockSpec(prev, tab_p), pl.BlockSpec(prev, tab_p)],
        out_specs=[pl.BlockSpec(cur, out_map)] * 2,
        out_shape=[jax.ShapeDtypeStruct((S, DIL_OUT), F32)] * 2,
        compiler_params=_cp("parallel", "parallel"),
        name=f"dilated_band_d{dil}",
    )(p_dil, p_dil, p_dil, p_dil, p_dil, cos_f, sin_f, cos_f, sin_f)
    return o, l


def _dil_combine_kernel(o0, o1, o2, l0, l1, l2, out_ref):
    a, b, c = l0[...], l1[...], l2[...]
    m = jnp.maximum(jnp.maximum(a, b), c)
    ea, eb, ec = jnp.exp(a - m), jnp.exp(b - m), jnp.exp(c - m)
    out_ref[...] = ((ea * o0[...] + eb * o1[...] + ec * o2[...]) / (ea + eb + ec)).astype(out_ref.dtype)


def _dilated(p_dil, cos_f, sin_f):
    S = p_dil.shape[0]
    outs, lses = [], []
    for g, (window, dil) in enumerate(DIL_PAIRS):
        assert window // dil == DIL_BLK
        o, l = _banded_group(p_dil, cos_f, sin_f, g, dil)
        outs.append(o)
        lses.append(l)
    tm = min(S, 1024)
    spec = pl.BlockSpec((tm, DIL_OUT), lambda i: (i, 0))
    return pl.pallas_call(
        _dil_combine_kernel,
        grid=(S // tm,),
        in_specs=[spec] * 6,
        out_specs=spec,
        out_shape=jax.ShapeDtypeStruct((S, DIL_OUT), BF16),
        compiler_params=_cp("parallel"),
        name="dilated_combine",
    )(*outs, *lses)


def _rwkv_prep_kernel(has_vmix, *refs):
    if has_vmix:
        (p_ref, prev_ref, mu_ref, w0_ref, w2_ref, a0_ref, a2_ref, g2_ref, kk_ref, ka_ref, rk_ref, hd_ref,
         lt_ref, pick_ref, vf_ref, v0_ref, v1_ref, v2_ref,
         at_o, rt_o, bt_o, kt_o, v16_o, gc_o, bonus_o, g_o) = refs
        v_o = None
    else:
        (p_ref, prev_ref, mu_ref, w0_ref, w2_ref, a0_ref, a2_ref, g2_ref, kk_ref, ka_ref, rk_ref, hd_ref,
         lt_ref, pick_ref,
         at_o, rt_o, bt_o, kt_o, v16_o, gc_o, bonus_o, g_o, v_o) = refs
    W = RWKV_WIDTH
    p = p_ref[...]
    last = jnp.where(pl.program_id(0) == 0, 0.0, prev_ref[7:8, :])
    row = lax.broadcasted_iota(jnp.int32, p.shape, 0)
    p_prev = jnp.where(row == 0, last, pltpu.roll(p, 1, axis=0))
    xs = p + (p_prev - p) * mu_ref[...]
    r, k, v = xs[:, 0:W], xs[:, W:2 * W], xs[:, 2 * W:3 * W]
    w_low = xs[:, 3 * W:3 * W + LORA_PAD]
    a_low = xs[:, 3 * W + LORA_PAD:3 * W + 2 * LORA_PAD]
    g_low = xs[:, 3 * W + 2 * LORA_PAD:]
    lora = lambda t, w_ref: _dot(t.astype(BF16), w_ref[...].astype(BF16))
    log_w = _log_sigmoid(w0_ref[...] + lora(jnp.tanh(w_low), w2_ref)) - 0.5
    a = _sigmoid(a0_ref[...] + lora(a_low, a2_ref))
    g = lora(_sigmoid(g_low), g2_ref)
    if has_vmix:
        mix = _sigmoid(v0_ref[...] + lora(lora(v, v1_ref), v2_ref))
        v = v + (vf_ref[...] - v) * mix
    kk = k * kk_ref[...]
    kk = kk / jnp.maximum(jnp.sqrt(_group_sum_lanes(kk * kk, hd_ref[...])), 1e-12)
    k = k * (1.0 + (a - 1.0) * ka_ref[...])
    lw = -jnp.exp(log_w)
    lw_parts = _split3(lw)
    cum = _group_rows(lt_ref[...], lw_parts)
    e_in = jnp.exp(cum)
    e_inv = jnp.exp(-cum)
    at_o[...] = (-kk * jnp.exp(cum - lw)).astype(at_o.dtype)
    rt_o[...] = (r * e_in).astype(rt_o.dtype)
    bt_o[...] = (kk * a * e_inv).astype(bt_o.dtype)
    kt_o[...] = (k * e_inv).astype(kt_o.dtype)
    v16_o[...] = v.astype(v16_o.dtype)
    gc_o[...] = jnp.exp(sum(_dot(pick_ref[...], part) for part in lw_parts))
    bonus_o[...] = _group_sum_lanes(r * k * rk_ref[...], hd_ref[...]) * v
    g_o[...] = g
    if v_o is not None:
        v_o[...] = v


def _pad_rows(w, rows):
    return jnp.zeros((rows, w.shape[1]), w.dtype).at[:w.shape[0]].set(w)


def _pad_rwkv_cols(t):
    W = RWKV_WIDTH
    z = jnp.zeros(t.shape[:-1] + (LORA_PAD - RWKV_DECAY_LORA,), t.dtype)
    o1, o2 = 3 * W + RWKV_DECAY_LORA, 3 * W + RWKV_DECAY_LORA + RWKV_AAA_LORA
    return jnp.concatenate([t[..., :o1], z, t[..., o1:o2], z, t[..., o2:]], axis=-1)


def _head_indicator():
    i = np.arange(SLAB) // RWKV_HEAD_SIZE
    return jnp.asarray((i[:, None] == i[None, :]).astype(np.float32), BF16)


def _rwkv_prep(p, mu, w0, w2, a0, a2, g2, k_k, k_a, r_k, v_first, v_mix):
    S = p.shape[0]
    W = RWKV_WIDTH
    C = RWKV_CHUNK
    R = min(S, 512)
    has_vmix = v_mix is not None
    row = lambda n: pl.BlockSpec((1, n), lambda i: (0, 0))
    full = lambda a: pl.BlockSpec(a.shape, lambda i: (0, 0))
    tile = pl.BlockSpec((R, W), lambda i: (i, 0))
    w2p, a2p = _pad_rows(w2, LORA_PAD), _pad_rows(a2, LORA_PAD)
    hd = _head_indicator()
    ri = np.arange(SLAB)
    same = (ri[:, None] // C) == (ri[None, :] // C)
    ltri = jnp.asarray((same & (ri[None, :] <= ri[:, None])).astype(np.float32), BF16)
    pick = jnp.asarray((np.arange(R // C)[:, None] == np.arange(R)[None, :] // C).astype(np.float32), BF16)
    args = [p, p, _pad_rwkv_cols(mu)[None, :], w0[None, :], w2p, a0[None, :], a2p, g2, k_k[None, :], k_a[None, :],
            r_k[None, :], hd, ltri, pick]
    in_specs = [pl.BlockSpec((R, RWKV_IN_PAD), lambda i: (i, 0)),
                pl.BlockSpec((8, RWKV_IN_PAD), lambda i: (jnp.maximum(i * (R // 8) - 1, 0), 0)),
                row(RWKV_IN_PAD), row(W), full(w2p), row(W), full(a2p), full(g2), row(W), row(W), row(W),
                full(hd), full(ltri), full(pick)]
    if has_vmix:
        v0, v1, v2 = v_mix
        args += [v_first, v0[None, :], v1, v2]
        in_specs += [tile, row(W), full(v1), full(v2)]
    gc_tile = pl.BlockSpec((R // C, W), lambda i: (i, 0))
    out_specs = [tile] * 5 + [gc_tile, tile, tile]
    out_shape = ([jax.ShapeDtypeStruct((S, W), BF16)] * 5
                 + [jax.ShapeDtypeStruct((S // C, W), F32)] + [jax.ShapeDtypeStruct((S, W), F32)] * 2)
    if not has_vmix:
        out_specs.append(tile)
        out_shape.append(jax.ShapeDtypeStruct((S, W), F32))
    return pl.pallas_call(
        functools.partial(_rwkv_prep_kernel, has_vmix),
        grid=(S // R,),
        in_specs=in_specs,
        out_specs=out_specs,
        out_shape=out_shape,
        compiler_params=_cp("parallel"),
        name="rwkv_prep",
    )(*args)


def _rwkv_scan_kernel(at_ref, rt_ref, bt_ref, kt_ref, v_ref, gc_ref, mask_ref, eye_ref, y_ref, st_ref):
    C = RWKV_CHUNK
    N = RWKV_HEAD_SIZE
    n_chunks = at_ref.shape[0] // C

    @pl.when(pl.program_id(0) == 0)
    def _():
        st_ref[...] = jnp.zeros_like(st_ref)

    mask = mask_ref[...]
    eye = eye_ref[...]

    H = RWKV_HEADS
    K = RWKV_CHUNKS_PER_TRIP
    HS = range(H)
    zeros_cn = jnp.zeros((C, N), BF16)

    def body(trip, carry):
        units = []
        for kc in range(K):
            c = trip * K + kc
            rows = pl.ds(pl.multiple_of(c * C, C), C)
            tiles = [ref[rows, :] for ref in (at_ref, rt_ref, bt_ref, kt_ref, v_ref)]
            g_row = gc_ref[pl.ds(c, 1), :]
            for h in HS:
                s = slice(h * N, (h + 1) * N)
                units.append(tuple(t[:, s] for t in tiles) + (g_row[:, s],))
        U = range(len(units))
        a_t, r_t, b_t, k_t, vv, g_c = ([u[i] for u in units] for i in range(6))
        m_all = [_dot_nt(jnp.concatenate([a_t[u], r_t[u]], axis=0), jnp.concatenate([b_t[u], k_t[u]], axis=0)) * mask
                 for u in U]
        m16 = [m.astype(BF16) for m in m_all]
        m_ab16 = [m[:C, :C] for m in m16]
        t_inv = [eye + m[:C, :C] for m in m_all]
        x = [_dot(m, m) for m in m_ab16]
        for step in range(5):
            x16 = [v.astype(BF16) for v in x]
            t_inv = [t_inv[u] + _dot(x16[u], t_inv[u].astype(BF16)) for u in U]
            if step < 4:
                x = [_dot(v, v) for v in x16]
        t16 = [t.astype(BF16) for t in t_inv]
        mv = [_dot(m16[u], jnp.concatenate([zeros_cn, vv[u]], axis=0)) for u in U]
        p16 = [_dot(t16[u], a_t[u]).astype(BF16) for u in U]
        q16 = [_dot(t16[u], mv[u][:C].astype(BF16)).astype(BF16) for u in U]
        m_rb16 = [m[C:, :C] for m in m16]
        p2 = [r_t[u].astype(F32) + _dot(m_rb16[u], p16[u]) for u in U]
        q2 = [mv[u][C:] + _dot(m_rb16[u], q16[u]) for u in U]
        b_end = [(b_t[u] * g_c[u]).astype(BF16) for u in U]
        k_end = [(k_t[u] * g_c[u]).astype(BF16) for u in U]
        g_mat = [_dot_tn(b_end[u], p16[u]) + eye * g_c[u] for u in U]
        h_mat = [_dot_tn(jnp.concatenate([b_end[u], k_end[u]], axis=0), jnp.concatenate([q16[u], vv[u]], axis=0))
                 for u in U]
        pg16 = [jnp.concatenate([p2[u], g_mat[u]], axis=0).astype(BF16) for u in U]
        st = [st_ref[h] for h in HS]
        for kc in range(K):
            ys = []
            for h in HS:
                u = kc * H + h
                ys_st = _dot(pg16[u], st[h].astype(BF16))
                ys.append(ys_st[:C] + q2[u])
                st[h] = ys_st[C:] + h_mat[u]
            rows = pl.ds(pl.multiple_of((trip * K + kc) * C, C), C)
            y_ref[rows, :] = jnp.concatenate(ys, axis=1)
        for h in HS:
            st_ref[h] = st[h]
        return carry

    lax.fori_loop(0, n_chunks // K, body, 0)


def _rwkv_scan(at, rt, bt, kt, v16, gc):
    S, W = at.shape
    C = RWKV_CHUNK
    R = min(S, 512)
    ti = np.arange(C)
    strict = (ti[None, :] < ti[:, None]).astype(np.float32)
    incl = (ti[None, :] <= ti[:, None]).astype(np.float32)
    mask = jnp.asarray(np.block([[strict, strict], [incl, incl]]))
    eye = jnp.eye(C, dtype=F32)
    tile = pl.BlockSpec((R, W), lambda i: (i, 0))
    const = lambda a_: pl.BlockSpec(a_.shape, lambda i: (0, 0))
    return pl.pallas_call(
        _rwkv_scan_kernel,
        grid=(S // R,),
        in_specs=[tile] * 5 + [pl.BlockSpec((R // C, W), lambda i: (i, 0)), const(mask), const(eye)],
        out_specs=tile,
        out_shape=jax.ShapeDtypeStruct((S, W), F32),
        scratch_shapes=[pltpu.VMEM((RWKV_HEADS, RWKV_HEAD_SIZE, RWKV_HEAD_SIZE), F32)],
        compiler_params=_cp("arbitrary"),
        name="rwkv_scan",
    )(at, rt, bt, kt, v16, gc, mask, eye)


def _rwkv_post_kernel(y_ref, bonus_ref, g_ref, lw_ref, lb_ref, hd_ref, o_ref):
    hd = hd_ref[...]
    inv_n = 1.0 / RWKV_HEAD_SIZE
    y = y_ref[...]
    yc = y - _group_sum_lanes(y, hd) * inv_n
    var = _group_sum_lanes(yc * yc, hd) * inv_n
    yn = yc * lax.rsqrt(var + RWKV_LNX_EPS) * lw_ref[...] + lb_ref[...]
    o_ref[...] = ((yn + bonus_ref[...]) * g_ref[...]).astype(o_ref.dtype)


def _rwkv_post(y, bonus, g, lnx_w, lnx_b):
    S, W = y.shape
    R = min(S, 512)
    tile = pl.BlockSpec((R, W), lambda i: (i, 0))
    row = pl.BlockSpec((1, W), lambda i: (0, 0))
    hd = _head_indicator()
    return pl.pallas_call(
        _rwkv_post_kernel,
        grid=(S // R,),
        in_specs=[tile] * 3 + [row] * 2 + [pl.BlockSpec((SLAB, SLAB), lambda i: (0, 0))],
        out_specs=tile,
        out_shape=jax.ShapeDtypeStruct((S, W), BF16),
        compiler_params=_cp("parallel"),
        name="rwkv_post",
    )(y, bonus, g, lnx_w[None, :], lnx_b[None, :], hd)


def _rwkv(p, mu, w0, w2, a0, a2, g2, k_k, k_a, r_k, lnx_w, lnx_b, v_first, v_mix):
    outs = _rwkv_prep(p, mu, w0, w2, a0, a2, g2, k_k, k_a, r_k, v_first, v_mix)
    at, rt, bt, kt, v16, gc, bonus, g = outs[:8]
    y = _rwkv_scan(at, rt, bt, kt, v16, gc)
    out = _rwkv_post(y, bonus, g, lnx_w, lnx_b)
    return out, (outs[8] if v_mix is None else v_first)


def _moba_prep_kernel(k_ref, v_ref, c_ref, s_ref, kr_ref, vt_ref, km_ref, kn_ref):
    cos_f, sin_f = c_ref[...], s_ref[...]
    parts = []
    for h in range(MOBA_HEADS):
        parts.append(_rope(k_ref[:, h * HEAD_DIM:(h + 1) * HEAD_DIM], cos_f, sin_f))
    kr = jnp.concatenate(parts, axis=1)
    lane = lax.broadcasted_iota(jnp.int32, (k_ref.shape[0], HEAD_DIM), 1)
    tag = jnp.where(lane == pl.program_id(0), 1.0, 0.0).astype(BF16)
    kr_ref[...] = jnp.concatenate(
        [t for h in range(MOBA_HEADS) for t in (parts[h].astype(BF16), tag)], axis=1)
    vt = v_ref[...].T
    ones = jnp.where(lax.broadcasted_iota(jnp.int32, (MOBA_VROWS - HEAD_DIM, vt.shape[1]), 0) == 0, 1.0, 0.0)
    vt_ref[...] = jnp.concatenate(
        [t for h in range(MOBA_HEADS) for t in (vt[h * HEAD_DIM:(h + 1) * HEAD_DIM], ones)], axis=0).astype(BF16)
    km_ref[...] = jnp.broadcast_to(jnp.mean(kr, axis=0, keepdims=True), km_ref.shape)
    norms = []
    for h in range(MOBA_HEADS):
        k16 = parts[h].astype(BF16).astype(F32)
        n2 = jnp.max(jnp.sum(k16 * k16, axis=1, keepdims=True), axis=0, keepdims=True)
        norms.append(jnp.broadcast_to(jnp.sqrt(n2), (kn_ref.shape[0], HEAD_DIM)))
    kn_ref[...] = jnp.concatenate(norms, axis=1)


def _moba_kernel(q_ref, c_ref, s_ref, k_ref, vt_ref, km_ref, kn_ref, o_ref):
    blk = MOBA_BLOCK
    nb = km_ref.shape[0]
    T = MOBA_TILES_PER_STEP
    cb0 = pl.program_id(1) * T
    bi = lax.broadcasted_iota(jnp.int32, (nb, blk), 0)
    ki = lax.broadcasted_iota(jnp.int32, (blk, blk), 0)
    qi = lax.broadcasted_iota(jnp.int32, (blk, blk), 1)

    def update(m, acc, s, vt):
        m_new = jnp.maximum(m, jnp.max(s, axis=0, keepdims=True))
        return m_new, jnp.exp(m - m_new) * acc + _dot(vt, jnp.exp(s - m_new).astype(BF16))

    chains = []
    for h in range(MOBA_HEADS_PER_STEP):
        kdim = slice(h * 2 * HEAD_DIM, h * 2 * HEAD_DIM + HEAD_DIM)
        vrow = slice(h * MOBA_VROWS, (h + 1) * MOBA_VROWS)
        for t in range(T):
            cb = cb0 + t
            qrows = slice(t * blk, (t + 1) * blk)
            q = _rope(q_ref[qrows, h * HEAD_DIM:(h + 1) * HEAD_DIM], c_ref[qrows, :], s_ref[qrows, :])
            qt = (q * (HEAD_DIM ** -0.5)).T
            gate = jnp.where(bi < cb, _dot_hi(km_ref[:, h * HEAD_DIM:(h + 1) * HEAD_DIM], qt), -jnp.inf)
            sel = jnp.zeros((nb, blk), F32)
            for _ in range(MOBA_TOPK):
                m = jnp.max(gate, axis=0, keepdims=True)
                idx = jnp.min(jnp.where(gate == m, bi, nb), axis=0, keepdims=True)
                hit = bi == idx
                sel = jnp.where(hit, jnp.where(m > -jnp.inf, 1.0, sel), sel)
                gate = jnp.where(hit, -jnp.inf, gate)
            qt16 = qt.astype(BF16)
            own = pl.ds(pl.multiple_of(cb * blk, blk), blk)
            s = jnp.where(ki <= qi, _dot(k_ref[own, kdim], qt16), -jnp.inf)
            m_run = jnp.max(s, axis=0, keepdims=True)
            acc = _dot(vt_ref[cb, vrow, :], jnp.exp(s - m_run).astype(BF16))
            for u in range(t):
                j = cb0 + u
                picked = jnp.sum(jnp.where(bi == j, sel, 0.0), axis=0, keepdims=True)
                rows = pl.ds(pl.multiple_of(j * blk, blk), blk)
                s = jnp.where(picked > 0.5, _dot(k_ref[rows, kdim], qt16), -jnp.inf)
                m_run, acc = update(m_run, acc, s, vt_ref[j, vrow, :])
            in_loop = bi < cb0
            bias = (jnp.where(in_loop, 1.0 - sel, 1.0) * MOBA_MASKED).astype(BF16)
            qa16 = jnp.concatenate([qt16, bias, jnp.zeros((HEAD_DIM - nb, blk), BF16)], axis=0)
            q32 = qt16.astype(F32)
            q_norm = jnp.sqrt(jnp.sum(q32 * q32, axis=0, keepdims=True))
            ceil = jnp.where(jnp.where(in_loop, sel, 0.0) > 0.5, kn_ref[:, h * HEAD_DIM:h * HEAD_DIM + 1] * q_norm,
                             -jnp.inf)
            m_ref = jnp.maximum(m_run, jnp.max(ceil, axis=0, keepdims=True) * MOBA_CEIL_SLACK)
            chains.append(dict(kcol=slice(h * 2 * HEAD_DIM, (h + 1) * 2 * HEAD_DIM), vrow=vrow, qa16=qa16,
                               m=m_run, acc=acc, m_ref=m_ref))
    CH = range(len(chains))
    G = MOBA_GROUP
    n_trips = (cb0 + G - 1) // G

    def scores(j):
        rows = pl.ds(pl.multiple_of(j * blk, blk), blk)
        return [_dot(k_ref[rows, chains[c]["kcol"]], chains[c]["qa16"]) for c in CH]

    def fixed_reference_loop():
        m_ref = [c["m_ref"] for c in chains]

        def body(jg, acc):
            acc = list(acc)
            js = [jg * G + g for g in range(G)]
            s = scores(js[0])
            for g in range(G):
                s_next = scores(js[g + 1]) if g + 1 < G else None
                for c in CH:
                    acc[c] = acc[c] + _dot(vt_ref[js[g], chains[c]["vrow"], :], jnp.exp(s[c] - m_ref[c]).astype(BF16))
                s = s_next
            return tuple(acc)

        return lax.fori_loop(0, n_trips, body, tuple(jnp.exp(c["m"] - c["m_ref"]) * c["acc"] for c in chains))

    def running_max_loop():
        def body(jg, carry):
            m, acc = list(carry[0]), list(carry[1])
            js = [jg * G + g for g in range(G)]
            s = scores(js[0])
            for g in range(G):
                s_next = scores(js[g + 1]) if g + 1 < G else None
                for c in CH:
                    m[c], acc[c] = update(m[c], acc[c], s[c], vt_ref[js[g], chains[c]["vrow"], :])
                s = s_next
            return tuple(m), tuple(acc)

        return lax.fori_loop(0, n_trips, body, (tuple(c["m"] for c in chains), tuple(c["acc"] for c in chains)))[1]

    gap = functools.reduce(jnp.maximum, [jnp.max(c["m_ref"] - c["m"]) for c in chains])
    acc = lax.cond(gap <= MOBA_MAX_GAP, fixed_reference_loop, running_max_loop)
    out = [(a[:HEAD_DIM] / a[HEAD_DIM:HEAD_DIM + 1]).T for a in acc]
    o_ref[...] = jnp.concatenate(
        [jnp.concatenate(out[h * T:(h + 1) * T], axis=0) for h in range(MOBA_HEADS_PER_STEP)], axis=1
    ).astype(o_ref.dtype)


def _moba(p, cos_f, sin_f):
    S = p.shape[0]
    blk = MOBA_BLOCK
    nb = S // blk
    assert nb <= HEAD_DIM, "block ids are one-hot encoded in HEAD_DIM lanes"
    assert nb % MOBA_GROUP == 0, "the block loop over-runs to a multiple of MOBA_GROUP"
    HW = MOBA_HEADS * HEAD_DIM
    kr, vt, km, kn = pl.pallas_call(
        _moba_prep_kernel,
        grid=(nb,),
        in_specs=[pl.BlockSpec((blk, HW), lambda i: (i, 1)), pl.BlockSpec((blk, HW), lambda i: (i, 2)),
                  pl.BlockSpec((blk, HEAD_DIM), lambda i: (i, 0)), pl.BlockSpec((blk, HEAD_DIM), lambda i: (i, 0))],
        out_specs=[pl.BlockSpec((blk, 2 * HW), lambda i: (i, 0)),
                   pl.BlockSpec((None, MOBA_HEADS * MOBA_VROWS, blk), lambda i: (i, 0, 0)),
                   pl.BlockSpec((8, HW), lambda i: (i, 0)), pl.BlockSpec((8, HW), lambda i: (i, 0))],
        out_shape=[jax.ShapeDtypeStruct((S, 2 * HW), BF16),
                   jax.ShapeDtypeStruct((nb, MOBA_HEADS * MOBA_VROWS, blk), BF16),
                   jax.ShapeDtypeStruct((nb * 8, HW), F32), jax.ShapeDtypeStruct((nb * 8, HW), F32)],
        compiler_params=_cp("parallel"),
        name="moba_prep",
    )(p, p, cos_f, sin_f)
    k_mean = km.reshape(nb, 8, HW)[:, 0]
    k_norm = kn.reshape(nb, 8, HW)[:, 0]
    P = MOBA_HEADS_PER_STEP
    T = MOBA_TILES_PER_STEP
    assert nb % T == 0
    tab = pl.BlockSpec((T * blk, HEAD_DIM), lambda h, cb: (cb, 0))
    once = pl.Buffered(1)
    return pl.pallas_call(
        _moba_kernel,
        grid=(MOBA_HEADS // P, nb // T),
        in_specs=[pl.BlockSpec((T * blk, P * HEAD_DIM), lambda h, cb: (cb, h)), tab, tab,
                  pl.BlockSpec((S, P * 2 * HEAD_DIM), lambda h, cb: (0, h), pipeline_mode=once),
                  pl.BlockSpec((nb, P * MOBA_VROWS, blk), lambda h, cb: (0, h, 0), pipeline_mode=once),
                  pl.BlockSpec((nb, P * HEAD_DIM), lambda h, cb: (0, h)),
                  pl.BlockSpec((nb, P * HEAD_DIM), lambda h, cb: (0, h))],
        out_specs=pl.BlockSpec((T * blk, P * HEAD_DIM), lambda h, cb: (cb, h)),
        out_shape=jax.ShapeDtypeStruct((S, HW), BF16),
        compiler_params=_cp("parallel", "arbitrary"),
        name="moba",
    )(p, cos_f, sin_f, kr, vt, k_mean, k_norm)


def _split_w_in(w_in_l):
    o = np.cumsum((N_BRANCH * D_MODEL, GLA_IN, DIL_IN, RWKV_IN, MOBA_IN))
    w_gate, w_gla, w_dil, w_rwkv, w_moba = (w_in_l[:, a:b] for a, b in zip((0,) + tuple(o[:-1]), o))
    w_gla = jnp.concatenate([w_gla, jnp.zeros((D_MODEL, GLA_IN_PAD - GLA_IN), w_gla.dtype)], axis=1)
    w_rwkv = _pad_rwkv_cols(w_rwkv)
    return tuple(w.astype(BF16) for w in (w_gate, w_gla, w_dil, w_rwkv, w_moba))


def kernel(x, c, positions, w_ada, b_ada, norm1, w_in, gla_w_a2, gla_b_a2, gla_gnorm, rwkv_mu, rwkv_w0, rwkv_w2, rwkv_a0, rwkv_a2, rwkv_g2, rwkv_k_k, rwkv_k_a, rwkv_r_k, rwkv_lnx_w, rwkv_lnx_b, rwkv_v0, rwkv_v1, rwkv_v2, w_branch_a, w_branch_b, w_branch_c, w_branch_d, w_out, norm2, w_ffn_in, w_ffn_out, norm_f):
    B, S, D = x.shape
    assert B == 1 and D == D_MODEL
    xs = x.reshape(S, D)
    cos_f, sin_f = _rope_tables(positions)
    c8 = jnp.broadcast_to(c, (8, D))
    v_first = None
    for l in range(DEPTH):
        mod = _adaln(c8, w_ada, b_ada, l)[0:1]
        shift1, scale1, gate1, shift2, scale2, gate2 = jnp.split(mod, 6, axis=-1)

        h = _normmod(xs, norm1[l][None, :], scale1, shift1, BF16)
        w_gate, w_gla, w_dil, w_rwkv, w_moba = _split_w_in(w_in[l])
        p_gla = _mm(h, w_gla, 512, GLA_IN_PAD, F32, "proj_gla")
        p_dil = _mm(h, w_dil, 512, DIL_IN, F32, "proj_dil")
        p_rwkv = _mm(h, w_rwkv, 512, RWKV_IN_PAD, F32, "proj_rwkv")
        p_moba = _mm(h, w_moba, 512, MOBA_IN, F32, "proj_moba")

        o_gla = _gla(p_gla, gla_w_a2[l], gla_b_a2[l], gla_gnorm[l])
        o_dil = _dilated(p_dil, cos_f, sin_f)
        v_mix = None if l == 0 else (rwkv_v0[l - 1], rwkv_v1[l - 1], rwkv_v2[l - 1])
        o_rwkv, v_first = _rwkv(p_rwkv, rwkv_mu[l], rwkv_w0[l], rwkv_w2[l], rwkv_a0[l], rwkv_a2[l], rwkv_g2[l],
                                rwkv_k_k[l], rwkv_k_a[l], rwkv_r_k[l], rwkv_lnx_w[l], rwkv_lnx_b[l], v_first, v_mix)
        o_moba = _moba(p_moba, cos_f, sin_f)

        merged = _merge(h, w_gate, (o_gla, o_dil, o_rwkv, o_moba),
                        tuple(w[l].astype(BF16) for w in (w_branch_a, w_branch_b, w_branch_c, w_branch_d)), 1024, 512)
        xs = _mm_resid(merged, w_out[l].astype(BF16), xs, gate1, 1024, 1024, "out_proj_resid")

        act = _norm_mm_swiglu(xs, norm2[l][None, :], scale2, shift2, w_ffn_in, l, 1024, 512)
        xs = _mm_resid(act, w_ffn_out[l].astype(BF16), xs, gate2, 1024, 512, "ffn_out_resid")

    zero = jnp.zeros((1, D), F32)
    return _normmod(xs, norm_f[None, :], zero, zero, F32).reshape(B, S, D)
```

```python
import functools

import numpy as np
import jax
import jax.numpy as jnp
from jax import lax
from jax.experimental import pallas as pl
from jax.experimental.pallas import tpu as pltpu

F32 = jnp.float32
BF16 = jnp.bfloat16
HI = lax.Precision.HIGHEST

D_MODEL = 2048
DEPTH = 2
HEAD_DIM = 128
ROT_DIM = HEAD_DIM // 4
ROPE_THETA = 500000.0
N_BRANCH = 4
NORM_EPS = 1e-6

GLA_HEADS = 4
GLA_DK = 64
GLA_DV = 128
GLA_RANK = 16
GLA_NORMALIZER = 16.0
GLA_SUB = 16
GLA_SUBS_PER_TRIP = 8

DIL_PAIRS = ((128, 1), (512, 4), (2048, 16))
DIL_HEADS_PER_GROUP = 2
DIL_HEADS = DIL_HEADS_PER_GROUP * len(DIL_PAIRS)
DIL_BLK = 128
DIL_TILE = DIL_BLK * max(d for _, d in DIL_PAIRS)
DIL_UNITS_PER_STAGE = 8

RWKV_HEAD_SIZE = 64
RWKV_HEADS = 8
RWKV_WIDTH = RWKV_HEADS * RWKV_HEAD_SIZE
RWKV_DECAY_LORA = 96
RWKV_AAA_LORA = 96
RWKV_MV_LORA = 64
RWKV_GATE_LORA = 256
RWKV_LNX_EPS = 64e-5
RWKV_CHUNK = 64
RWKV_CHUNKS_PER_TRIP = 4
LORA_PAD = 128

MOBA_HEADS = 4
MOBA_BLOCK = 256
MOBA_TOPK = 3
MOBA_MASKED = -1e30
MOBA_VROWS = HEAD_DIM + 16
MOBA_HEADS_PER_STEP = 2
MOBA_MAX_GAP = 60.0
MOBA_CEIL_SLACK = 1.0 + 2.0 ** -7
MOBA_TILES_PER_STEP = 2
MOBA_GROUP = 8

FFN_HIDDEN = -(-8 * D_MODEL // (3 * 256)) * 256

GLA_SIZES = (GLA_HEADS * GLA_DK, GLA_HEADS * GLA_DK, GLA_HEADS * GLA_DV, GLA_HEADS * GLA_DV, GLA_RANK)
RWKV_SIZES = (RWKV_WIDTH, RWKV_WIDTH, RWKV_WIDTH, RWKV_DECAY_LORA, RWKV_AAA_LORA, RWKV_GATE_LORA)
GLA_IN = sum(GLA_SIZES)
DIL_IN = 3 * DIL_HEADS * HEAD_DIM
RWKV_IN = sum(RWKV_SIZES)
MOBA_IN = 3 * MOBA_HEADS * HEAD_DIM
GLA_OUT = GLA_HEADS * GLA_DV
DIL_OUT = DIL_HEADS_PER_GROUP * HEAD_DIM
RWKV_OUT = RWKV_WIDTH
MOBA_OUT = MOBA_HEADS * HEAD_DIM

GLA_IN_PAD = GLA_IN - GLA_RANK + 128
RWKV_IN_PAD = 3 * RWKV_WIDTH + 2 * LORA_PAD + RWKV_GATE_LORA

VMEM_LIMIT_BYTES = 56 * 1024 * 1024
SLAB = 128


def _cp(*sem):
    return pltpu.CompilerParams(dimension_semantics=sem, vmem_limit_bytes=VMEM_LIMIT_BYTES)


def _dot(a, b):
    return jnp.dot(a, b, preferred_element_type=F32)


def _dot_hi(a, b):
    return jnp.dot(a, b, precision=HI, preferred_element_type=F32)


def _split3(x):
    x1 = x.astype(BF16)
    r1 = x - x1.astype(F32)
    x2 = r1.astype(BF16)
    x3 = (r1 - x2.astype(F32)).astype(BF16)
    return x1, x2, x3


def _dot_sel_l(sel16, x):
    x1, x2, x3 = _split3(x)
    return _dot(sel16, x1) + _dot(sel16, x2) + _dot(sel16, x3)


def _dot_sel_r(x, sel16):
    x1, x2, x3 = _split3(x)
    return _dot(x1, sel16) + _dot(x2, sel16) + _dot(x3, sel16)


def _group_sum_lanes(x, ind16):
    n = ind16.shape[0]
    return jnp.concatenate([_dot_sel_r(x[:, i:i + n], ind16) for i in range(0, x.shape[1], n)], axis=1)


def _group_rows(sel16, parts):
    n = sel16.shape[1]
    slabs = range(0, parts[0].shape[0], n)
    return jnp.concatenate([sum(_dot(sel16, p[i:i + n]) for p in parts) for i in slabs], axis=0)


def _dot_nt(a, b, precision=None):
    return lax.dot_general(a, b, (((1,), (1,)), ((), ())), precision=precision, preferred_element_type=F32)


def _dot_tn(a, b, precision=None):
    return lax.dot_general(a, b, (((0,), (0,)), ((), ())), precision=precision, preferred_element_type=F32)


def _log_sigmoid(z):
    return jnp.minimum(z, 0.0) - jnp.log1p(jnp.exp(-jnp.abs(z)))


def _sigmoid(z):
    return 1.0 / (1.0 + jnp.exp(-z))


def _rope(x, cos_f, sin_f):
    half = ROT_DIM // 2
    lane = lax.broadcasted_iota(jnp.int32, x.shape, 1)
    partner = jnp.where(lane < half, pltpu.roll(x, HEAD_DIM - half, axis=1), pltpu.roll(x, half, axis=1))
    return x * cos_f + partner * sin_f


def _rope_table_kernel(pos_ref, inv_ref, cos_ref, sin_ref):
    ang = pos_ref[...] * inv_ref[...]
    lane = lax.broadcasted_iota(jnp.int32, ang.shape, 1)
    half = ROT_DIM // 2
    c = jnp.cos(ang)
    s = jnp.sin(ang)
    cos_ref[...] = jnp.where(lane < ROT_DIM, c, 1.0)
    sin_ref[...] = jnp.where(lane < half, -s, jnp.where(lane < ROT_DIM, s, 0.0))


def _rope_tables(positions):
    S = positions.shape[1]
    inv = ROPE_THETA ** (-jnp.arange(0, ROT_DIM, 2, dtype=F32) / ROT_DIM)
    inv_full = jnp.concatenate([inv, inv, jnp.zeros((HEAD_DIM - ROT_DIM,), F32)])[None, :]
    pos = positions.astype(F32).reshape(S, 1)
    tm = min(S, 1024)
    return pl.pallas_call(
        _rope_table_kernel,
        grid=(S // tm,),
        in_specs=[pl.BlockSpec((tm, 1), lambda i: (i, 0)), pl.BlockSpec((1, HEAD_DIM), lambda i: (0, 0))],
        out_specs=[pl.BlockSpec((tm, HEAD_DIM), lambda i: (i, 0))] * 2,
        out_shape=[jax.ShapeDtypeStruct((S, HEAD_DIM), F32)] * 2,
        compiler_params=_cp("parallel"),
        name="rope_tables",
    )(pos, inv_full)


def _ada_kernel(c_ref, w_ref, b_ref, o_ref):
    c = c_ref[...]
    w16 = w_ref[...].astype(BF16)
    o_ref[...] = sum(_dot(part, w16) for part in _split3(c * _sigmoid(c))) + b_ref[...]


def _adaln(c8, w_ada, b_ada, layer):
    D = D_MODEL
    tn = 1536
    return pl.pallas_call(
        _ada_kernel,
        grid=(6 * D // tn,),
        in_specs=[pl.BlockSpec((8, D), lambda n: (0, 0)),
                  pl.BlockSpec((None, D, tn), lambda n: (layer, 0, n)),
                  pl.BlockSpec((None, 1, tn), lambda n: (layer, 0, n))],
        out_specs=pl.BlockSpec((8, tn), lambda n: (0, n)),
        out_shape=jax.ShapeDtypeStruct((8, 6 * D), F32),
        compiler_params=_cp("parallel"),
        name="adaln",
    )(c8, w_ada, b_ada.reshape(DEPTH, 1, 6 * D))


def _normmod_kernel(x_ref, g_ref, sc_ref, sh_ref, o_ref):
    x = x_ref[...]
    y = x * lax.rsqrt(jnp.mean(x * x, axis=-1, keepdims=True) + NORM_EPS) * g_ref[...]
    o_ref[...] = (y * (1.0 + sc_ref[...]) + sh_ref[...]).astype(o_ref.dtype)


def _normmod(x, gain, scale, shift, out_dtype):
    S, D = x.shape
    tm = min(S, 512)
    row = pl.BlockSpec((1, D), lambda i: (0, 0))
    return pl.pallas_call(
        _normmod_kernel,
        grid=(S // tm,),
        in_specs=[pl.BlockSpec((tm, D), lambda i: (i, 0)), row, row, row],
        out_specs=pl.BlockSpec((tm, D), lambda i: (i, 0)),
        out_shape=jax.ShapeDtypeStruct((S, D), out_dtype),
        compiler_params=_cp("parallel"),
        name="normmod",
    )(x, gain, scale, shift)


def _mm_kernel(a_ref, b_ref, o_ref):
    o_ref[...] = _dot(a_ref[...], b_ref[...]).astype(o_ref.dtype)


def _mm(a, b, tm, tn, out_dtype, name):
    M, K = a.shape
    N = b.shape[1]
    tm, tn = min(tm, M), min(tn, N)
    return pl.pallas_call(
        _mm_kernel,
        grid=(N // tn, M // tm),
        in_specs=[pl.BlockSpec((tm, K), lambda n, m: (m, 0)), pl.BlockSpec((K, tn), lambda n, m: (0, n))],
        out_specs=pl.BlockSpec((tm, tn), lambda n, m: (m, n)),
        out_shape=jax.ShapeDtypeStruct((M, N), out_dtype),
        compiler_params=_cp("parallel", "parallel"),
        name=name,
    )(a, b)


def _norm_swiglu_kernel(x_ref, gn_ref, sc_ref, sh_ref, bg_ref, bu_ref, o_ref, h_ref):
    @pl.when(pl.program_id(1) == 0)
    def _():
        x = x_ref[...]
        y = x * lax.rsqrt(jnp.mean(x * x, axis=-1, keepdims=True) + NORM_EPS) * gn_ref[...]
        h_ref[...] = (y * (1.0 + sc_ref[...]) + sh_ref[...]).astype(h_ref.dtype)

    a = h_ref[...]
    g = _dot(a, bg_ref[...].astype(BF16))
    u = _dot(a, bu_ref[...].astype(BF16))
    o_ref[...] = (g * _sigmoid(g) * u).astype(o_ref.dtype)


def _norm_mm_swiglu(x, gain, scale, shift, w_stack, layer, tm, tn):
    M, K = x.shape
    H = w_stack.shape[2] // 2
    tm = min(tm, M)
    nb = H // tn
    row = pl.BlockSpec((1, K), lambda m, n: (0, 0))
    w = w_stack
    return pl.pallas_call(
        _norm_swiglu_kernel,
        grid=(M // tm, nb),
        in_specs=[pl.BlockSpec((tm, K), lambda m, n: (m, 0)), row, row, row,
                  pl.BlockSpec((None, K, tn), lambda m, n: (layer, 0, n)),
                  pl.BlockSpec((None, K, tn), lambda m, n: (layer, 0, nb + n))],
        out_specs=pl.BlockSpec((tm, tn), lambda m, n: (m, n)),
        out_shape=jax.ShapeDtypeStruct((M, H), BF16),
        scratch_shapes=[pltpu.VMEM((tm, K), BF16)],
        compiler_params=_cp("parallel", "arbitrary"),
        name="ffn_in_swiglu",
    )(x, gain, scale, shift, w, w)


def _resid_kernel(a_ref, b_ref, x_ref, g_ref, o_ref):
    o_ref[...] = x_ref[...] + g_ref[...] * _dot(a_ref[...], b_ref[...])


def _mm_resid(a, b, x, gate, tm, tn, name):
    M, K = a.shape
    N = b.shape[1]
    tm = min(tm, M)
    return pl.pallas_call(
        _resid_kernel,
        grid=(N // tn, M // tm),
        in_specs=[pl.BlockSpec((tm, K), lambda n, m: (m, 0)),
                  pl.BlockSpec((K, tn), lambda n, m: (0, n)),
                  pl.BlockSpec((tm, tn), lambda n, m: (m, n)),
                  pl.BlockSpec((1, tn), lambda n, m: (0, n))],
        out_specs=pl.BlockSpec((tm, tn), lambda n, m: (m, n)),
        out_shape=jax.ShapeDtypeStruct((M, N), F32),
        compiler_params=_cp("parallel", "parallel"),
        name=name,
    )(a, b, x, gate)


def _merge_kernel(h_ref, wg0, wg1, wg2, wg3, oa, ob, oc, od, wa, wb, wc, wd, o_ref):
    h = h_ref[...]
    acc = None
    for wg, o_br, w_br in ((wg0, oa, wa), (wg1, ob, wb), (wg2, oc, wc), (wg3, od, wd)):
        term = _sigmoid(_dot(h, wg[...])) * _dot(o_br[...], w_br[...])
        acc = term if acc is None else acc + term
    o_ref[...] = acc.astype(o_ref.dtype)


def _merge(h, w_gate, branches, w_branches, tm, tn):
    M, D = h.shape
    tm = min(tm, M)
    nb = D // tn
    in_specs = [pl.BlockSpec((tm, D), lambda n, m: (m, 0))]
    for i in range(N_BRANCH):
        in_specs.append(pl.BlockSpec((D, tn), lambda n, m, i=i: (0, i * nb + n)))
    for br in branches:
        in_specs.append(pl.BlockSpec((tm, br.shape[1]), lambda n, m: (m, 0)))
    for wb in w_branches:
        in_specs.append(pl.BlockSpec((wb.shape[0], tn), lambda n, m: (0, n)))
    return pl.pallas_call(
        _merge_kernel,
        grid=(nb, M // tm),
        in_specs=in_specs,
        out_specs=pl.BlockSpec((tm, tn), lambda n, m: (m, n)),
        out_shape=jax.ShapeDtypeStruct((M, D), BF16),
        compiler_params=_cp("parallel", "parallel"),
        name="gated_merge",
    )(h, w_gate, w_gate, w_gate, w_gate, *branches, *w_branches)


def _gla_kernel(q_ref, k_ref, v_ref, g_ref, al_ref, wa_ref, ba_ref, gn_ref, lt_ref, blk_ref, e_ref, sel_ref,
                o_ref, st_ref, qe_ref, ke_ref, b_ref, dec_ref, acc_ref):
    C = GLA_SUB
    R = q_ref.shape[0]

    @pl.when(pl.program_id(0) == 0)
    def _():
        st_ref[...] = jnp.zeros_like(st_ref)

    gk = _log_sigmoid(_dot_hi(al_ref[...], wa_ref[...]) + ba_ref[...]) * (1.0 / GLA_NORMALIZER)
    gk_parts = _split3(gk)
    b = _group_rows(lt_ref[...], gk_parts)
    tot = _group_rows(blk_ref[...], gk_parts)
    scale = GLA_DK ** -0.5
    qe_ref[...] = (q_ref[...] * scale * jnp.exp(b)).astype(BF16)
    ke_ref[...] = (k_ref[...] * jnp.exp(tot - b)).astype(BF16)
    b_ref[...] = b
    dec_ref[...] = jnp.exp(tot)

    e_mat = e_ref[...]
    sel = sel_ref[...]
    s_idx = lax.broadcasted_iota(jnp.int32, (C, GLA_HEADS * GLA_DK), 0)

    K = GLA_SUBS_PER_TRIP
    HS = range(GLA_HEADS)
    kh = [slice(h * GLA_DK, (h + 1) * GLA_DK) for h in HS]
    vh = [slice(h * GLA_DV, (h + 1) * GLA_DV) for h in HS]

    def body(trip, carry):
        r0 = [pl.multiple_of((trip * K + i) * C, C) for i in range(K)]
        rows = [pl.ds(r, C) for r in r0]
        vs = [v_ref[r, :] for r in rows]
        p_st = []
        for i in range(K):
            qs = q_ref[rows[i], :] * scale
            ks = k_ref[rows[i], :]
            bs = b_ref[rows[i], :]
            pairs = []
            for t in range(C):
                e = jnp.exp(jnp.where(s_idx <= t, bs[t:t + 1, :] - bs, -jnp.inf))
                pairs.append(qs[t:t + 1, :] * ks * e)
            p_st.append(jnp.concatenate(pairs, axis=0).astype(BF16))
        sc = [_dot(p, e_mat) for p in p_st]
        w = [(sc[i] * jnp.concatenate([vs[i]] * C, axis=0)).astype(BF16) for i in range(K)]
        intra = [_dot(sel, x) for x in w]
        qe = [qe_ref[r, :] for r in rows]
        ke = [ke_ref[r, :] for r in rows]
        kv = [[_dot_tn(vs[i].astype(BF16)[:, vh[h]], ke[i][:, kh[h]]) for h in HS] for i in range(K)]
        st = [st_ref[h] for h in HS]
        for i in range(K):
            dec = dec_ref[pl.ds(r0[i], 1), :]
            outs = []
            for h in HS:
                outs.append(_dot_nt(qe[i][:, kh[h]], st[h].astype(BF16)))
                st[h] = st[h] * dec[:, kh[h]] + kv[i][h]
            acc_ref[rows[i], :] = jnp.concatenate(outs, axis=1) + intra[i]
        for h in HS:
            st_ref[h] = st[h]
        return carry

    lax.fori_loop(0, R // (C * K), body, 0)

    o = acc_ref[...]
    g = g_ref[...]
    gn = gn_ref[...]
    parts = []
    for h in range(GLA_HEADS):
        oh = o[:, h * GLA_DV:(h + 1) * GLA_DV]
        parts.append(oh * lax.rsqrt(jnp.mean(oh * oh, axis=-1, keepdims=True) + NORM_EPS) * gn)
    o_ref[...] = (jnp.concatenate(parts, axis=1) * (g * _sigmoid(g))).astype(o_ref.dtype)


def _gla(p, w_a2, b_a2, g_norm):
    S = p.shape[0]
    R = min(S, 512)
    C = GLA_SUB
    HK, HV = GLA_HEADS * GLA_DK, GLA_HEADS * GLA_DV
    wa = jnp.zeros((128, HK), F32).at[:GLA_RANK].set(w_a2)
    ri = np.arange(SLAB)
    same = (ri[:, None] // C) == (ri[None, :] // C)
    ltri = jnp.asarray((same & (ri[None, :] <= ri[:, None])).astype(np.float32), BF16)
    blk = jnp.asarray(same.astype(np.float32), BF16)
    e_mat =jnp.asarray((np.arange(HK)[:, None] // GLA_DK == np.arange(HV)[None, :] // GLA_DV).astype(np.float32), BF16)
    sel = jnp.asarray((np.arange(C)[:, None] == np.arange(C * C)[None, :] // C).astype(np.float32), BF16)
    const = lambda shape: pl.BlockSpec(shape, lambda i: (0, 0))
    return pl.pallas_call(
        _gla_kernel,
        grid=(S // R,),
        in_specs=[pl.BlockSpec((R, HK), lambda i: (i, 0)),
                  pl.BlockSpec((R, HK), lambda i: (i, 1)),
                  pl.BlockSpec((R, HV), lambda i: (i, 1)),
                  pl.BlockSpec((R, HV), lambda i: (i, 2)),
                  pl.BlockSpec((R, 128), lambda i: (i, (2 * HK + 2 * HV) // 128)),
                  const((128, HK)), const((1, HK)), const((1, GLA_DV)),
                  const((SLAB, SLAB)), const((SLAB, SLAB)), const((HK, HV)), const((C, C * C))],
        out_specs=pl.BlockSpec((R, HV), lambda i: (i, 0)),
        out_shape=jax.ShapeDtypeStruct((S, HV), BF16),
        scratch_shapes=[pltpu.VMEM((GLA_HEADS, GLA_DV, GLA_DK), F32),
                        pltpu.VMEM((R, HK), BF16), pltpu.VMEM((R, HK), BF16),
                        pltpu.VMEM((R, HK), F32), pltpu.VMEM((R, HK), F32),
                        pltpu.VMEM((R, HV), F32)],
        compiler_params=_cp("arbitrary"),
        name="gla",
    )(p, p, p, p, p, wa, b_a2[None, :], g_norm[None, :], ltri, blk, e_mat, sel)


def _band_kernel(dil, q_ref, k_ref, v_ref, kp_ref, vp_ref, c_ref, s_ref, cp_ref, sp_ref, o_ref, l_ref):
    t = pl.program_id(1)
    blk = DIL_BLK
    span = blk * dil
    qi = lax.broadcasted_iota(jnp.int32, (blk, 2 * blk), 0)
    ki = lax.broadcasted_iota(jnp.int32, (blk, 2 * blk), 1)
    dist = blk + qi - ki
    band = jnp.where(dist >= 0, jnp.where(dist <= blk, 1, 0), 0)
    band_first = band * jnp.where(ki >= blk, 1, jnp.where(t > 0, 1, 0))

    def rows_of(start):
        return pl.ds(start, blk, stride=dil) if dil > 1 else pl.ds(start, blk)

    units = [(r, b) for r in range(dil) for b in range(DIL_TILE // span)]
    roped = {}
    for g0 in range(0, len(units), DIL_UNITS_PER_STAGE):
        grp = units[g0:g0 + DIL_UNITS_PER_STAGE]
        rows, q16 = {}, {}
        for r, b in grp:
            rows[r, b] = rows_of(b * span + r)
            cos_c, sin_c = c_ref[rows[r, b], :], s_ref[rows[r, b], :]
            q16[r, b] = (_rope(q_ref[rows[r, b], :], cos_c, sin_c) * (HEAD_DIM ** -0.5)).astype(BF16)
            roped[r, b] = (_rope(k_ref[rows[r, b], :], cos_c, sin_c).astype(BF16), v_ref[rows[r, b], :].astype(BF16))
            if b == 0:
                prow = rows_of(r)
                roped[r, -1] = (_rope(kp_ref[prow, :], cp_ref[prow, :], sp_ref[prow, :]).astype(BF16),
                                vp_ref[prow, :].astype(BF16))
        s = {u: _dot_nt(q16[u], jnp.concatenate([roped[u[0], u[1] - 1][0], roped[u][0]], axis=0)) for u in grp}
        s = {u: jnp.where((band_first if u[1] == 0 else band) > 0, s[u], -jnp.inf) for u in grp}
        m = {u: jnp.max(s[u], axis=-1, keepdims=True) for u in grp}
        pr = {u: jnp.exp(s[u] - m[u]) for u in grp}
        den = {u: jnp.sum(pr[u], axis=-1, keepdims=True) for u in grp}
        pv = {u: _dot(pr[u].astype(BF16), jnp.concatenate([roped[u[0], u[1] - 1][1], roped[u][1]], axis=0))
              for u in grp}
        for u in grp:
            o_ref[rows[u], :] = pv[u] / den[u]
            l_ref[rows[u], :] = jnp.broadcast_to(m[u] + jnp.log(den[u]), (blk, HEAD_DIM))


def _banded_group(p_dil, cos_f, sin_f, group, dil):
    S = p_dil.shape[0]
    hp = DIL_HEADS_PER_GROUP
    span = DIL_BLK * dil
    assert S % DIL_TILE == 0 and DIL_TILE % span == 0
    per_tile = DIL_TILE // span

    def col(part):
        return lambda h, t: (t, part * DIL_HEADS + group * hp + h)

    def col_prev(part):
        return lambda h, t: (jnp.maximum(t * per_tile - 1, 0), part * DIL_HEADS + group * hp + h)

    cur = (DIL_TILE, HEAD_DIM)
    prev = (span, HEAD_DIM)
    tab_c = lambda h, t: (t, 0)
    tab_p = lambda h, t: (jnp.maximum(t * per_tile - 1, 0), 0)
    out_map = lambda h, t: (t, h)
    o, l = pl.pallas_call(
        functools.partial(_band_kernel, dil),
        grid=(hp, S // DIL_TILE),
        in_specs=[pl.BlockSpec(cur, col(0)), pl.BlockSpec(cur, col(1)), pl.BlockSpec(cur, col(2)),
                  pl.BlockSpec(prev, col_prev(1)), pl.BlockSpec(prev, col_prev(2)),
                  pl.BlockSpec(cur, tab_c), pl.BlockSpec(cur, tab_c), pl.BlockSpec(prev, tab_p), pl.BlockSpec(prev, tab_p)],
        out_specs=[pl.BlockSpec(cur, out_map)] * 2,
        out_shape=[jax.ShapeDtypeStruct((S, DIL_OUT), F32)] * 2,
        compiler_params=_cp("parallel", "parallel"),
        name=f"dilated_band_d{dil}",
    )(p_dil, p_dil, p_dil, p_dil, p_dil, cos_f, sin_f, cos_f, sin_f)
    return o, l


def _dil_combine_kernel(o0, o1, o2, l0, l1, l2, out_ref):
    a, b, c = l0[...], l1[...], l2[...]
    m = jnp.maximum(jnp.maximum(a, b), c)
    ea, eb, ec = jnp.exp(a - m), jnp.exp(b - m), jnp.exp(c - m)
    out_ref[...] = ((ea * o0[...] + eb * o1[...] + ec * o2[...]) / (ea + eb + ec)).astype(out_ref.dtype)


def _dilated(p_dil, cos_f, sin_f):
    S = p_dil.shape[0]
    outs, lses = [], []
    for g, (window, dil) in enumerate(DIL_PAIRS):
        assert window // dil == DIL_BLK
        o, l = _banded_group(p_dil, cos_f, sin_f, g, dil)
        outs.append(o)
        lses.append(l)
    tm = min(S, 1024)
    spec = pl.BlockSpec((tm, DIL_OUT), lambda i: (i, 0))
    return pl.pallas_call(
        _dil_combine_kernel,
        grid=(S // tm,),
        in_specs=[spec] * 6,
        out_specs=spec,
        out_shape=jax.ShapeDtypeStruct((S, DIL_OUT), BF16),
        compiler_params=_cp("parallel"),
        name="dilated_combine",
    )(*outs, *lses)


def _rwkv_prep_kernel(has_vmix, *refs):
    if has_vmix:
        (p_ref, prev_ref, mu_ref, w0_ref, w2_ref, a0_ref, a2_ref, g2_ref, kk_ref, ka_ref, rk_ref, hd_ref,
         lt_ref, pick_ref, vf_ref, v0_ref, v1_ref, v2_ref,
         at_o, rt_o, bt_o, kt_o, v16_o, gc_o, bonus_o, g_o) = refs
        v_o = None
    else:
        (p_ref, prev_ref, mu_ref, w0_ref, w2_ref, a0_ref, a2_ref, g2_ref, kk_ref, ka_ref, rk_ref, hd_ref,
         lt_ref, pick_ref,
         at_o, rt_o, bt_o, kt_o, v16_o, gc_o, bonus_o, g_o, v_o) = refs
    W = RWKV_WIDTH
    p = p_ref[...]
    last = jnp.where(pl.program_id(0) == 0, 0.0, prev_ref[7:8, :])
    row = lax.broadcasted_iota(jnp.int32, p.shape, 0)
    p_prev = jnp.where(row == 0, last, pltpu.roll(p, 1, axis=0))
    xs = p + (p_prev - p) * mu_ref[...]
    r, k, v = xs[:, 0:W], xs[:, W:2 * W], xs[:, 2 * W:3 * W]
    w_low = xs[:, 3 * W:3 * W + LORA_PAD]
    a_low = xs[:, 3 * W + LORA_PAD:3 * W + 2 * LORA_PAD]
    g_low = xs[:, 3 * W + 2 * LORA_PAD:]
    lora = lambda t, w_ref: _dot(t.astype(BF16), w_ref[...].astype(BF16))
    log_w = _log_sigmoid(w0_ref[...] + lora(jnp.tanh(w_low), w2_ref)) - 0.5
    a = _sigmoid(a0_ref[...] + lora(a_low, a2_ref))
    g = lora(_sigmoid(g_low), g2_ref)
    if has_vmix:
        mix = _sigmoid(v0_ref[...] + lora(lora(v, v1_ref), v2_ref))
        v = v + (vf_ref[...] - v) * mix
    kk = k * kk_ref[...]
    kk = kk / jnp.maximum(jnp.sqrt(_group_sum_lanes(kk * kk, hd_ref[...])), 1e-12)
    k = k * (1.0 + (a - 1.0) * ka_ref[...])
    lw = -jnp.exp(log_w)
    lw_parts = _split3(lw)
    cum = _group_rows(lt_ref[...], lw_parts)
    e_in = jnp.exp(cum)
    e_inv = jnp.exp(-cum)
    at_o[...] = (-kk * jnp.exp(cum - lw)).astype(at_o.dtype)
    rt_o[...] = (r * e_in).astype(rt_o.dtype)
    bt_o[...] = (kk * a * e_inv).astype(bt_o.dtype)
    kt_o[...] = (k * e_inv).astype(kt_o.dtype)
    v16_o[...] = v.astype(v16_o.dtype)
    gc_o[...] = jnp.exp(sum(_dot(pick_ref[...], part) for part in lw_parts))
    bonus_o[...] = _group_sum_lanes(r * k * rk_ref[...], hd_ref[...]) * v
    g_o[...] = g
    if v_o is not None:
        v_o[...] = v


def _pad_rows(w, rows):
    return jnp.zeros((rows, w.shape[1]), w.dtype).at[:w.shape[0]].set(w)


def _pad_rwkv_cols(t):
    W = RWKV_WIDTH
    z = jnp.zeros(t.shape[:-1] + (LORA_PAD - RWKV_DECAY_LORA,), t.dtype)
    o1, o2 = 3 * W + RWKV_DECAY_LORA, 3 * W + RWKV_DECAY_LORA + RWKV_AAA_LORA
    return jnp.concatenate([t[..., :o1], z, t[..., o1:o2], z, t[..., o2:]], axis=-1)


def _head_indicator():
    i = np.arange(SLAB) // RWKV_HEAD_SIZE
    return jnp.asarray((i[:, None] == i[None, :]).astype(np.float32), BF16)


def _rwkv_prep(p, mu, w0, w2, a0, a2, g2, k_k, k_a, r_k, v_first, v_mix):
    S = p.shape[0]
    W = RWKV_WIDTH
    C = RWKV_CHUNK
    R = min(S, 512)
    has_vmix = v_mix is not None
    row = lambda n: pl.BlockSpec((1, n), lambda i: (0, 0))
    full = lambda a: pl.BlockSpec(a.shape, lambda i: (0, 0))
    tile = pl.BlockSpec((R, W), lambda i: (i, 0))
    w2p, a2p = _pad_rows(w2, LORA_PAD), _pad_rows(a2, LORA_PAD)
    hd = _head_indicator()
    ri = np.arange(SLAB)
    same = (ri[:, None] // C) == (ri[None, :] // C)
    ltri = jnp.asarray((same & (ri[None, :] <= ri[:, None])).astype(np.float32), BF16)
    pick = jnp.asarray((np.arange(R // C)[:, None] == np.arange(R)[None, :] // C).astype(np.float32), BF16)
    args = [p, p, _pad_rwkv_cols(mu)[None, :], w0[None, :], w2p, a0[None, :], a2p, g2, k_k[None, :], k_a[None, :],
            r_k[None, :], hd, ltri, pick]
    in_specs = [pl.BlockSpec((R, RWKV_IN_PAD), lambda i: (i, 0)),
                pl.BlockSpec((8, RWKV_IN_PAD), lambda i: (jnp.maximum(i * (R // 8) - 1, 0), 0)),
                row(RWKV_IN_PAD), row(W), full(w2p), row(W), full(a2p), full(g2), row(W), row(W), row(W),
                full(hd), full(ltri), full(pick)]
    if has_vmix:
        v0, v1, v2 = v_mix
        args += [v_first, v0[None, :], v1, v2]
        in_specs += [tile, row(W), full(v1), full(v2)]
    gc_tile = pl.BlockSpec((R // C, W), lambda i: (i, 0))
    out_specs = [tile] * 5 + [gc_tile, tile, tile]
    out_shape = ([jax.ShapeDtypeStruct((S, W), BF16)] * 5
                 + [jax.ShapeDtypeStruct((S // C, W), F32)] + [jax.ShapeDtypeStruct((S, W), F32)] * 2)
    if not has_vmix:
        out_specs.append(tile)
        out_shape.append(jax.ShapeDtypeStruct((S, W), F32))
    return pl.pallas_call(
        functools.partial(_rwkv_prep_kernel, has_vmix),
        grid=(S // R,),
        in_specs=in_specs,
        out_specs=out_specs,
        out_shape=out_shape,
        compiler_params=_cp("parallel"),
        name="rwkv_prep",
    )(*args)


def _rwkv_scan_kernel(at_ref, rt_ref, bt_ref, kt_ref, v_ref, gc_ref, mask_ref, eye_ref, y_ref, st_ref):
    C = RWKV_CHUNK
    N = RWKV_HEAD_SIZE
    n_chunks = at_ref.shape[0] // C

    @pl.when(pl.program_id(0) == 0)
    def _():
        st_ref[...] = jnp.zeros_like(st_ref)

    mask = mask_ref[...]
    eye = eye_ref[...]

    H = RWKV_HEADS
    K = RWKV_CHUNKS_PER_TRIP
    HS = range(H)
    zeros_cn = jnp.zeros((C, N), BF16)

    def body(trip, carry):
        units = []
        for kc in range(K):
            c = trip * K + kc
            rows = pl.ds(pl.multiple_of(c * C, C), C)
            tiles = [ref[rows, :] for ref in (at_ref, rt_ref, bt_ref, kt_ref, v_ref)]
            g_row = gc_ref[pl.ds(c, 1), :]
            for h in HS:
                s = slice(h * N, (h + 1) * N)
                units.append(tuple(t[:, s] for t in tiles) + (g_row[:, s],))
        U = range(len(units))
        a_t, r_t, b_t, k_t, vv, g_c = ([u[i] for u in units] for i in range(6))
        m_all = [_dot_nt(jnp.concatenate([a_t[u], r_t[u]], axis=0), jnp.concatenate([b_t[u], k_t[u]], axis=0)) * mask
                 for u in U]
        m16 = [m.astype(BF16) for m in m_all]
        m_ab16 = [m[:C, :C] for m in m16]
        t_inv = [eye + m[:C, :C] for m in m_all]
        x = [_dot(m, m) for m in m_ab16]
        for step in range(5):
            x16 = [v.astype(BF16) for v in x]
            t_inv = [t_inv[u] + _dot(x16[u], t_inv[u].astype(BF16)) for u in U]
            if step < 4:
                x = [_dot(v, v) for v in x16]
        t16 = [t.astype(BF16) for t in t_inv]
        mv = [_dot(m16[u], jnp.concatenate([zeros_cn, vv[u]], axis=0)) for u in U]
        p16 = [_dot(t16[u], a_t[u]).astype(BF16) for u in U]
        q16 = [_dot(t16[u], mv[u][:C].astype(BF16)).astype(BF16) for u in U]
        m_rb16 = [m[C:, :C] for m in m16]
        p2 = [r_t[u].astype(F32) + _dot(m_rb16[u], p16[u]) for u in U]
        q2 = [mv[u][C:] + _dot(m_rb16[u], q16[u]) for u in U]
        b_end = [(b_t[u] * g_c[u]).astype(BF16) for u in U]
        k_end = [(k_t[u] * g_c[u]).astype(BF16) for u in U]
        g_mat = [_dot_tn(b_end[u], p16[u]) + eye * g_c[u] for u in U]
        h_mat = [_dot_tn(jnp.concatenate([b_end[u], k_end[u]], axis=0), jnp.concatenate([q16[u], vv[u]], axis=0))
                 for u in U]
        pg16 = [jnp.concatenate([p2[u], g_mat[u]], axis=0).astype(BF16) for u in U]
        st = [st_ref[h] for h in HS]
        for kc in range(K):
            ys = []
            for h in HS:
                u = kc * H + h
                ys_st = _dot(pg16[u], st[h].astype(BF16))
                ys.append(ys_st[:C] + q2[u])
                st[h] = ys_st[C:] + h_mat[u]
            rows = pl.ds(pl.multiple_of((trip * K + kc) * C, C), C)
            y_ref[rows, :] = jnp.concatenate(ys, axis=1)
        for h in HS:
            st_ref[h] = st[h]
        return carry

    lax.fori_loop(0, n_chunks // K, body, 0)


def _rwkv_scan(at, rt, bt, kt, v16, gc):
    S, W = at.shape
    C = RWKV_CHUNK
    R = min(S, 512)
    ti = np.arange(C)
    strict = (ti[None, :] < ti[:, None]).astype(np.float32)
    incl = (ti[None, :] <= ti[:, None]).astype(np.float32)
    mask = jnp.asarray(np.block([[strict, strict], [incl, incl]]))
    eye = jnp.eye(C, dtype=F32)
    tile = pl.BlockSpec((R, W), lambda i: (i, 0))
    const = lambda a_: pl.BlockSpec(a_.shape, lambda i: (0, 0))
    return pl.pallas_call(
        _rwkv_scan_kernel,
        grid=(S // R,),
        in_specs=[tile] * 5 + [pl.BlockSpec((R // C, W), lambda i: (i, 0)), const(mask), const(eye)],
        out_specs=tile,
        out_shape=jax.ShapeDtypeStruct((S, W), F32),
        scratch_shapes=[pltpu.VMEM((RWKV_HEADS, RWKV_HEAD_SIZE, RWKV_HEAD_SIZE), F32)],
        compiler_params=_cp("arbitrary"),
        name="rwkv_scan",
    )(at, rt, bt, kt, v16, gc, mask, eye)


def _rwkv_post_kernel(y_ref, bonus_ref, g_ref, lw_ref, lb_ref, hd_ref, o_ref):
    hd = hd_ref[...]
    inv_n = 1.0 / RWKV_HEAD_SIZE
    y = y_ref[...]
    yc = y - _group_sum_lanes(y, hd) * inv_n
    var = _group_sum_lanes(yc * yc, hd) * inv_n
    yn = yc * lax.rsqrt(var + RWKV_LNX_EPS) * lw_ref[...] + lb_ref[...]
    o_ref[...] = ((yn + bonus_ref[...]) * g_ref[...]).astype(o_ref.dtype)


def _rwkv_post(y, bonus, g, lnx_w, lnx_b):
    S, W = y.shape
    R = min(S, 512)
    tile = pl.BlockSpec((R, W), lambda i: (i, 0))
    row = pl.BlockSpec((1, W), lambda i: (0, 0))
    hd = _head_indicator()
    return pl.pallas_call(
        _rwkv_post_kernel,
        grid=(S // R,),
        in_specs=[tile] * 3 + [row] * 2 + [pl.BlockSpec((SLAB, SLAB), lambda i: (0, 0))],
        out_specs=tile,
        out_shape=jax.ShapeDtypeStruct((S, W), BF16),
        compiler_params=_cp("parallel"),
        name="rwkv_post",
    )(y, bonus, g, lnx_w[None, :], lnx_b[None, :], hd)


def _rwkv(p, mu, w0, w2, a0, a2, g2, k_k, k_a, r_k, lnx_w, lnx_b, v_first, v_mix):
    outs = _rwkv_prep(p, mu, w0, w2, a0, a2, g2, k_k, k_a, r_k, v_first, v_mix)
    at, rt, bt, kt, v16, gc, bonus, g = outs[:8]
    y = _rwkv_scan(at, rt, bt, kt, v16, gc)
    out = _rwkv_post(y, bonus, g, lnx_w, lnx_b)
    return out, (outs[8] if v_mix is None else v_first)


def _moba_prep_kernel(k_ref, v_ref, c_ref, s_ref, kr_ref, vt_ref, km_ref, kn_ref):
    cos_f, sin_f = c_ref[...], s_ref[...]
    parts = []
    for h in range(MOBA_HEADS):
        parts.append(_rope(k_ref[:, h * HEAD_DIM:(h + 1) * HEAD_DIM], cos_f, sin_f))
    kr = jnp.concatenate(parts, axis=1)
    lane = lax.broadcasted_iota(jnp.int32, (k_ref.shape[0], HEAD_DIM), 1)
    tag = jnp.where(lane == pl.program_id(0), 1.0, 0.0).astype(BF16)
    kr_ref[...] = jnp.concatenate(
        [t for h in range(MOBA_HEADS) for t in (parts[h].astype(BF16), tag)], axis=1)
    vt = v_ref[...].T
    ones = jnp.where(lax.broadcasted_iota(jnp.int32, (MOBA_VROWS - HEAD_DIM, vt.shape[1]), 0) == 0, 1.0, 0.0)
    vt_ref[...] = jnp.concatenate(
        [t for h in range(MOBA_HEADS) for t in (vt[h * HEAD_DIM:(h + 1) * HEAD_DIM], ones)], axis=0).astype(BF16)
    km_ref[...] = jnp.broadcast_to(jnp.mean(kr, axis=0, keepdims=True), km_ref.shape)
    norms = []
    for h in range(MOBA_HEADS):
        k16 = parts[h].astype(BF16).astype(F32)
        n2 = jnp.max(jnp.sum(k16 * k16, axis=1, keepdims=True), axis=0, keepdims=True)
        norms.append(jnp.broadcast_to(jnp.sqrt(n2), (kn_ref.shape[0], HEAD_DIM)))
    kn_ref[...] = jnp.concatenate(norms, axis=1)


def _moba_kernel(q_ref, c_ref, s_ref, k_ref, vt_ref, km_ref, kn_ref, o_ref):
    blk = MOBA_BLOCK
    nb = km_ref.shape[0]
    T = MOBA_TILES_PER_STEP
    cb0 = pl.program_id(1) * T
    bi = lax.broadcasted_iota(jnp.int32, (nb, blk), 0)
    ki = lax.broadcasted_iota(jnp.int32, (blk, blk), 0)
    qi = lax.broadcasted_iota(jnp.int32, (blk, blk), 1)

    def update(m, acc, s, vt):
        m_new = jnp.maximum(m, jnp.max(s, axis=0, keepdims=True))
        return m_new, jnp.exp(m - m_new) * acc + _dot(vt, jnp.exp(s - m_new).astype(BF16))

    chains = []
    for h in range(MOBA_HEADS_PER_STEP):
        kdim = slice(h * 2 * HEAD_DIM, h * 2 * HEAD_DIM + HEAD_DIM)
        vrow = slice(h * MOBA_VROWS, (h + 1) * MOBA_VROWS)
        for t in range(T):
            cb = cb0 + t
            qrows = slice(t * blk, (t + 1) * blk)
            q = _rope(q_ref[qrows, h * HEAD_DIM:(h + 1) * HEAD_DIM], c_ref[qrows, :], s_ref[qrows, :])
            qt = (q * (HEAD_DIM ** -0.5)).T
            gate = jnp.where(bi < cb, _dot_hi(km_ref[:, h * HEAD_DIM:(h + 1) * HEAD_DIM], qt), -jnp.inf)
            sel = jnp.zeros((nb, blk), F32)
            for _ in range(MOBA_TOPK):
                m = jnp.max(gate, axis=0, keepdims=True)
                idx = jnp.min(jnp.where(gate == m, bi, nb), axis=0, keepdims=True)
                hit = bi == idx
                sel = jnp.where(hit, jnp.where(m > -jnp.inf, 1.0, sel), sel)
                gate = jnp.where(hit, -jnp.inf, gate)
            qt16 = qt.astype(BF16)
            own = pl.ds(pl.multiple_of(cb * blk, blk), blk)
            s = jnp.where(ki <= qi, _dot(k_ref[own, kdim], qt16), -jnp.inf)
            m_run = jnp.max(s, axis=0, keepdims=True)
            acc = _dot(vt_ref[cb, vrow, :], jnp.exp(s - m_run).astype(BF16))
            for u in range(t):
                j = cb0 + u
                picked = jnp.sum(jnp.where(bi == j, sel, 0.0), axis=0, keepdims=True)
                rows = pl.ds(pl.multiple_of(j * blk, blk), blk)
                s = jnp.where(picked > 0.5, _dot(k_ref[rows, kdim], qt16), -jnp.inf)
                m_run, acc = update(m_run, acc, s, vt_ref[j, vrow, :])
            in_loop = bi < cb0
            bias = (jnp.where(in_loop, 1.0 - sel, 1.0) * MOBA_MASKED).astype(BF16)
            qa16 = jnp.concatenate([qt16, bias, jnp.zeros((HEAD_DIM - nb, blk), BF16)], axis=0)
            q32 = qt16.astype(F32)
            q_norm = jnp.sqrt(jnp.sum(q32 * q32, axis=0, keepdims=True))
            ceil = jnp.where(jnp.where(in_loop, sel, 0.0) > 0.5, kn_ref[:, h * HEAD_DIM:h * HEAD_DIM + 1] * q_norm,
                             -jnp.inf)
            m_ref = jnp.maximum(m_run, jnp.max(ceil, axis=0, keepdims=True) * MOBA_CEIL_SLACK)
            chains.append(dict(kcol=slice(h * 2 * HEAD_DIM, (h + 1) * 2 * HEAD_DIM), vrow=vrow, qa16=qa16,
                               m=m_run, acc=acc, m_ref=m_ref))
    CH = range(len(chains))
    G = MOBA_GROUP
    n_trips = (cb0 + G - 1) // G

    def scores(j):
        rows = pl.ds(pl.multiple_of(j * blk, blk), blk)
        return [_dot(k_ref[rows, chains[c]["kcol"]], chains[c]["qa16"]) for c in CH]

    def fixed_reference_loop():
        m_ref = [c["m_ref"] for c in chains]

        def body(jg, acc):
            acc = list(acc)
            js = [jg * G + g for g in range(G)]
            s = scores(js[0])
            for g in range(G):
                s_next = scores(js[g + 1]) if g + 1 < G else None
                for c in CH:
                    acc[c] = acc[c] + _dot(vt_ref[js[g], chains[c]["vrow"], :], jnp.exp(s[c] - m_ref[c]).astype(BF16))
                s = s_next
            return tuple(acc)

        return lax.fori_loop(0, n_trips, body, tuple(jnp.exp(c["m"] - c["m_ref"]) * c["acc"] for c in chains))

    def running_max_loop():
        def body(jg, carry):
            m, acc = list(carry[0]), list(carry[1])
            js = [jg * G + g for g in range(G)]
            s = scores(js[0])
            for g in range(G):
                s_next = scores(js[g + 1]) if g + 1 < G else None
                for c in CH:
                    m[c], acc[c] = update(m[c], acc[c], s[c], vt_ref[js[g], chains[c]["vrow"], :])
                s = s_next
            return tuple(m), tuple(acc)

        return lax.fori_loop(0, n_trips, body, (tuple(c["m"] for c in chains), tuple(c["acc"] for c in chains)))[1]

    gap = functools.reduce(jnp.maximum, [jnp.max(c["m_ref"] - c["m"]) for c in chains])
    acc = lax.cond(gap <= MOBA_MAX_GAP, fixed_reference_loop, running_max_loop)
    out = [(a[:HEAD_DIM] / a[HEAD_DIM:HEAD_DIM + 1]).T for a in acc]
    o_ref[...] = jnp.concatenate(
        [jnp.concatenate(out[h * T:(h + 1) * T], axis=0) for h in range(MOBA_HEADS_PER_STEP)], axis=1
    ).astype(o_ref.dtype)


def _moba(p, cos_f, sin_f):
    S = p.shape[0]
    blk = MOBA_BLOCK
    nb = S // blk
    assert nb <= HEAD_DIM, "block ids are one-hot encoded in HEAD_DIM lanes"
    assert nb % MOBA_GROUP == 0, "the block loop over-runs to a multiple of MOBA_GROUP"
    HW = MOBA_HEADS * HEAD_DIM
    kr, vt, km, kn = pl.pallas_call(
        _moba_prep_kernel,
        grid=(nb,),
        in_specs=[pl.BlockSpec((blk, HW), lambda i: (i, 1)), pl.BlockSpec((blk, HW), lambda i: (i, 2)),
                  pl.BlockSpec((blk, HEAD_DIM), lambda i: (i, 0)), pl.BlockSpec((blk, HEAD_DIM), lambda i: (i, 0))],
        out_specs=[pl.BlockSpec((blk, 2 * HW), lambda i: (i, 0)),
                   pl.BlockSpec((None, MOBA_HEADS * MOBA_VROWS, blk), lambda i: (i, 0, 0)),
                   pl.BlockSpec((8, HW), lambda i: (i, 0)), pl.BlockSpec((8, HW), lambda i: (i, 0))],
        out_shape=[jax.ShapeDtypeStruct((S, 2 * HW), BF16),
                   jax.ShapeDtypeStruct((nb, MOBA_HEADS * MOBA_VROWS, blk), BF16),
                   jax.ShapeDtypeStruct((nb * 8, HW), F32), jax.ShapeDtypeStruct((nb * 8, HW), F32)],
        compiler_params=_cp("parallel"),
        name="moba_prep",
    )(p, p, cos_f, sin_f)
    k_mean = km.reshape(nb, 8, HW)[:, 0]
    k_norm = kn.reshape(nb, 8, HW)[:, 0]
    P = MOBA_HEADS_PER_STEP
    T = MOBA_TILES_PER_STEP
    assert nb % T == 0
    tab = pl.BlockSpec((T * blk, HEAD_DIM), lambda h, cb: (cb, 0))
    once = pl.Buffered(1)
    return pl.pallas_call(
        _moba_kernel,
        grid=(MOBA_HEADS // P, nb // T),
        in_specs=[pl.BlockSpec((T * blk, P * HEAD_DIM), lambda h, cb: (cb, h)), tab, tab,
                  pl.BlockSpec((S, P * 2 * HEAD_DIM), lambda h, cb: (0, h), pipeline_mode=once),
                  pl.BlockSpec((nb, P * MOBA_VROWS, blk), lambda h, cb: (0, h, 0), pipeline_mode=once),
                  pl.BlockSpec((nb, P * HEAD_DIM), lambda h, cb: (0, h)),
                  pl.BlockSpec((nb, P * HEAD_DIM), lambda h, cb: (0, h))],
        out_specs=pl.BlockSpec((T * blk, P * HEAD_DIM), lambda h, cb: (cb, h)),
        out_shape=jax.ShapeDtypeStruct((S, HW), BF16),
        compiler_params=_cp("parallel", "arbitrary"),
        name="moba",
    )(p, cos_f, sin_f, kr, vt, k_mean, k_norm)


def _split_w_in(w_in_l):
    o = np.cumsum((N_BRANCH * D_MODEL, GLA_IN, DIL_IN, RWKV_IN, MOBA_IN))
    w_gate, w_gla, w_dil, w_rwkv, w_moba = (w_in_l[:, a:b] for a, b in zip((0,) + tuple(o[:-1]), o))
    w_gla = jnp.concatenate([w_gla, jnp.zeros((D_MODEL, GLA_IN_PAD - GLA_IN), w_gla.dtype)], axis=1)
    w_rwkv = _pad_rwkv_cols(w_rwkv)
    return tuple(w.astype(BF16) for w in (w_gate, w_gla, w_dil, w_rwkv, w_moba))


def kernel(x, c, positions, w_ada, b_ada, norm1, w_in, gla_w_a2, gla_b_a2, gla_gnorm, rwkv_mu, rwkv_w0, rwkv_w2, rwkv_a0, rwkv_a2, rwkv_g2, rwkv_k_k, rwkv_k_a, rwkv_r_k, rwkv_lnx_w, rwkv_lnx_b, rwkv_v0, rwkv_v1, rwkv_v2, w_branch_a, w_branch_b, w_branch_c, w_branch_d, w_out, norm2, w_ffn_in, w_ffn_out, norm_f):
    B, S, D = x.shape
    assert B == 1 and D == D_MODEL
    xs = x.reshape(S, D)
    cos_f, sin_f = _rope_tables(positions)
    c8 = jnp.broadcast_to(c, (8, D))
    v_first = None
    for l in range(DEPTH):
        mod = _adaln(c8, w_ada, b_ada, l)[0:1]
        shift1, scale1, gate1, shift2, scale2, gate2 = jnp.split(mod, 6, axis=-1)

        h = _normmod(xs, norm1[l][None, :], scale1, shift1, BF16)
        w_gate, w_gla, w_dil, w_rwkv, w_moba = _split_w_in(w_in[l])
        p_gla = _mm(h, w_gla, 512, GLA_IN_PAD, F32, "proj_gla")
        p_dil = _mm(h, w_dil, 512, DIL_IN, F32, "proj_dil")
        p_rwkv = _mm(h, w_rwkv, 512, RWKV_IN_PAD, F32, "proj_rwkv")
        p_moba = _mm(h, w_moba, 512, MOBA_IN, F32, "proj_moba")

        o_gla = _gla(p_gla, gla_w_a2[l], gla_b_a2[l], gla_gnorm[l])
        o_dil = _dilated(p_dil, cos_f, sin_f)
        v_mix = None if l == 0 else (rwkv_v0[l - 1], rwkv_v1[l - 1], rwkv_v2[l - 1])
        o_rwkv, v_first = _rwkv(p_rwkv, rwkv_mu[l], rwkv_w0[l], rwkv_w2[l], rwkv_a0[l], rwkv_a2[l], rwkv_g2[l],
                                rwkv_k_k[l], rwkv_k_a[l], rwkv_r_k[l], rwkv_lnx_w[l], rwkv_lnx_b[l], v_first, v_mix)
        o_moba = _moba(p_moba, cos_f, sin_f)

        merged = _merge(h, w_gate, (o_gla, o_dil, o_rwkv, o_moba),
                        tuple(w[l].astype(BF16) for w in (w_branch_a, w_branch_b, w_branch_c, w_branch_d)), 1024, 512)
        xs = _mm_resid(merged, w_out[l].astype(BF16), xs, gate1, 1024, 1024, "out_proj_resid")

        act = _norm_mm_swiglu(xs, norm2[l][None, :], scale2, shift2, w_ffn_in, l, 1024, 512)
        xs = _mm_resid(act, w_ffn_out[l].astype(BF16), xs, gate2, 1024, 512, "ffn_out_resid")

    zero = jnp.zeros((1, D), F32)
    return _normmod(xs, norm_f[None, :], zero, zero, F32).reshape(B, S, D)
```

```python
import functools

import numpy as np
import jax
import jax.numpy as jnp
from jax import lax
from jax.experimental import pallas as pl
from jax.experimental.pallas import tpu as pltpu

F32 = jnp.float32
BF16 = jnp.bfloat16
HI = lax.Precision.HIGHEST

D_MODEL = 2048
DEPTH = 2
HEAD_DIM = 128
ROT_DIM = HEAD_DIM // 4
ROPE_THETA = 500000.0
N_BRANCH = 4
NORM_EPS = 1e-6

GLA_HEADS = 4
GLA_DK = 64
GLA_DV = 128
GLA_RANK = 16
GLA_NORMALIZER = 16.0
GLA_SUB = 16
GLA_SUBS_PER_TRIP = 8

DIL_PAIRS = ((128, 1), (512, 4), (2048, 16))
DIL_HEADS_PER_GROUP = 2
DIL_HEADS = DIL_HEADS_PER_GROUP * len(DIL_PAIRS)
DIL_BLK = 128
DIL_TILE = DIL_BLK * max(d for _, d in DIL_PAIRS)
DIL_UNITS_PER_STAGE = 8

RWKV_HEAD_SIZE = 64
RWKV_HEADS = 8
RWKV_WIDTH = RWKV_HEADS * RWKV_HEAD_SIZE
RWKV_DECAY_LORA = 96
RWKV_AAA_LORA = 96
RWKV_MV_LORA = 64
RWKV_GATE_LORA = 256
RWKV_LNX_EPS = 64e-5
RWKV_CHUNK = 64
RWKV_CHUNKS_PER_TRIP = 4
LORA_PAD = 128

MOBA_HEADS = 4
MOBA_BLOCK = 256
MOBA_TOPK = 3
MOBA_MASKED = -1e30
MOBA_VROWS = HEAD_DIM + 16
MOBA_HEADS_PER_STEP = 2
MOBA_MAX_GAP = 60.0
MOBA_CEIL_SLACK = 1.0 + 2.0 ** -7
MOBA_TILES_PER_STEP = 2
MOBA_GROUP = 8

FFN_HIDDEN = -(-8 * D_MODEL // (3 * 256)) * 256

GLA_SIZES = (GLA_HEADS * GLA_DK, GLA_HEADS * GLA_DK, GLA_HEADS * GLA_DV, GLA_HEADS * GLA_DV, GLA_RANK)
RWKV_SIZES = (RWKV_WIDTH, RWKV_WIDTH, RWKV_WIDTH, RWKV_DECAY_LORA, RWKV_AAA_LORA, RWKV_GATE_LORA)
GLA_IN = sum(GLA_SIZES)
DIL_IN = 3 * DIL_HEADS * HEAD_DIM
RWKV_IN = sum(RWKV_SIZES)
MOBA_IN = 3 * MOBA_HEADS * HEAD_DIM
GLA_OUT = GLA_HEADS * GLA_DV
DIL_OUT = DIL_HEADS_PER_GROUP * HEAD_DIM
RWKV_OUT = RWKV_WIDTH
MOBA_OUT = MOBA_HEADS * HEAD_DIM

GLA_IN_PAD = GLA_IN - GLA_RANK + 128
RWKV_IN_PAD = 3 * RWKV_WIDTH + 2 * LORA_PAD + RWKV_GATE_LORA

VMEM_LIMIT_BYTES = 56 * 1024 * 1024
SLAB = 128


def _cp(*sem):
    return pltpu.CompilerParams(dimension_semantics=sem, vmem_limit_bytes=VMEM_LIMIT_BYTES)


def _dot(a, b):
    return jnp.dot(a, b, preferred_element_type=F32)


def _dot_hi(a, b):
    return jnp.dot(a, b, precision=HI, preferred_element_type=F32)


def _split3(x):
    x1 = x.astype(BF16)
    r1 = x - x1.astype(F32)
    x2 = r1.astype(BF16)
    x3 = (r1 - x2.astype(F32)).astype(BF16)
    return x1, x2, x3


def _dot_sel_l(sel16, x):
    x1, x2, x3 = _split3(x)
    return _dot(sel16, x1) + _dot(sel16, x2) + _dot(sel16, x3)


def _dot_sel_r(x, sel16):
    x1, x2, x3 = _split3(x)
    return _dot(x1, sel16) + _dot(x2, sel16) + _dot(x3, sel16)


def _group_sum_lanes(x, ind16):
    n = ind16.shape[0]
    return jnp.concatenate([_dot_sel_r(x[:, i:i + n], ind16) for i in range(0, x.shape[1], n)], axis=1)


def _group_rows(sel16, parts):
    n = sel16.shape[1]
    slabs = range(0, parts[0].shape[0], n)
    return jnp.concatenate([sum(_dot(sel16, p[i:i + n]) for p in parts) for i in slabs], axis=0)


def _dot_nt(a, b, precision=None):
    return lax.dot_general(a, b, (((1,), (1,)), ((), ())), precision=precision, preferred_element_type=F32)


def _dot_tn(a, b, precision=None):
    return lax.dot_general(a, b, (((0,), (0,)), ((), ())), precision=precision, preferred_element_type=F32)


def _log_sigmoid(z):
    return jnp.minimum(z, 0.0) - jnp.log1p(jnp.exp(-jnp.abs(z)))


def _sigmoid(z):
    return 1.0 / (1.0 + jnp.exp(-z))


def _rope(x, cos_f, sin_f):
    half = ROT_DIM // 2
    lane = lax.broadcasted_iota(jnp.int32, x.shape, 1)
    partner = jnp.where(lane < half, pltpu.roll(x, HEAD_DIM - half, axis=1), pltpu.roll(x, half, axis=1))
    return x * cos_f + partner * sin_f


def _rope_table_kernel(pos_ref, inv_ref, cos_ref, sin_ref):
    ang = pos_ref[...] * inv_ref[...]
    lane = lax.broadcasted_iota(jnp.int32, ang.shape, 1)
    half = ROT_DIM // 2
    c = jnp.cos(ang)
    s = jnp.sin(ang)
    cos_ref[...] = jnp.where(lane < ROT_DIM, c, 1.0)
    sin_ref[...] = jnp.where(lane < half, -s, jnp.where(lane < ROT_DIM, s, 0.0))


def _rope_tables(positions):
    S = positions.shape[1]
    inv = ROPE_THETA ** (-jnp.arange(0, ROT_DIM, 2, dtype=F32) / ROT_DIM)
    inv_full = jnp.concatenate([inv, inv, jnp.zeros((HEAD_DIM - ROT_DIM,), F32)])[None, :]
    pos = positions.astype(F32).reshape(S, 1)
    tm = min(S, 1024)
    return pl.pallas_call(
        _rope_table_kernel,
        grid=(S // tm,),
        in_specs=[pl.BlockSpec((tm, 1), lambda i: (i, 0)), pl.BlockSpec((1, HEAD_DIM), lambda i: (0, 0))],
        out_specs=[pl.BlockSpec((tm, HEAD_DIM), lambda i: (i, 0))] * 2,
        out_shape=[jax.ShapeDtypeStruct((S, HEAD_DIM), F32)] * 2,
        compiler_params=_cp("parallel"),
        name="rope_tables",
    )(pos, inv_full)


def _ada_kernel(c_ref, w_ref, b_ref, o_ref):
    c = c_ref[...]
    w16 = w_ref[...].astype(BF16)
    o_ref[...] = sum(_dot(part, w16) for part in _split3(c * _sigmoid(c))) + b_ref[...]


def _adaln(c8, w_ada, b_ada, layer):
    D = D_MODEL
    tn = 1536
    return pl.pallas_call(
        _ada_kernel,
        grid=(6 * D // tn,),
        in_specs=[pl.BlockSpec((8, D), lambda n: (0, 0)),
                  pl.BlockSpec((None, D, tn), lambda n: (layer, 0, n)),
                  pl.BlockSpec((None, 1, tn), lambda n: (layer, 0, n))],
        out_specs=pl.BlockSpec((8, tn), lambda n: (0, n)),
        out_shape=jax.ShapeDtypeStruct((8, 6 * D), F32),
        compiler_params=_cp("parallel"),
        name="adaln",
    )(c8, w_ada, b_ada.reshape(DEPTH, 1, 6 * D))


def _normmod_kernel(x_ref, g_ref, sc_ref, sh_ref, o_ref):
    x = x_ref[...]
    y = x * lax.rsqrt(jnp.mean(x * x, axis=-1, keepdims=True) + NORM_EPS) * g_ref[...]
    o_ref[...] = (y * (1.0 + sc_ref[...]) + sh_ref[...]).astype(o_ref.dtype)


def _normmod(x, gain, scale, shift, out_dtype):
    S, D = x.shape
    tm = min(S, 512)
    row = pl.BlockSpec((1, D), lambda i: (0, 0))
    return pl.pallas_call(
        _normmod_kernel,
        grid=(S // tm,),
        in_specs=[pl.BlockSpec((tm, D), lambda i: (i, 0)), row, row, row],
        out_specs=pl.BlockSpec((tm, D), lambda i: (i, 0)),
        out_shape=jax.ShapeDtypeStruct((S, D), out_dtype),
        compiler_params=_cp("parallel"),
        name="normmod",
    )(x, gain, scale, shift)


def _mm_kernel(a_ref, b_ref, o_ref):
    o_ref[...] = _dot(a_ref[...], b_ref[...]).astype(o_ref.dtype)


def _mm(a, b, tm, tn, out_dtype, name):
    M, K = a.shape
    N = b.shape[1]
    tm, tn = min(tm, M), min(tn, N)
    return pl.pallas_call(
        _mm_kernel,
        grid=(N // tn, M // tm),
        in_specs=[pl.BlockSpec((tm, K), lambda n, m: (m, 0)), pl.BlockSpec((K, tn), lambda n, m: (0, n))],
        out_specs=pl.BlockSpec((tm, tn), lambda n, m: (m, n)),
        out_shape=jax.ShapeDtypeStruct((M, N), out_dtype),
        compiler_params=_cp("parallel", "parallel"),
        name=name,
    )(a, b)


def _norm_swiglu_kernel(x_ref, gn_ref, sc_ref, sh_ref, bg_ref, bu_ref, o_ref, h_ref):
    @pl.when(pl.program_id(1) == 0)
    def _():
        x = x_ref[...]
        y = x * lax.rsqrt(jnp.mean(x * x, axis=-1, keepdims=True) + NORM_EPS) * gn_ref[...]
        h_ref[...] = (y * (1.0 + sc_ref[...]) + sh_ref[...]).astype(h_ref.dtype)

    a = h_ref[...]
    g = _dot(a, bg_ref[...].astype(BF16))
    u = _dot(a, bu_ref[...].astype(BF16))
    o_ref[...] = (g * _sigmoid(g) * u).astype(o_ref.dtype)


def _norm_mm_swiglu(x, gain, scale, shift, w_stack, layer, tm, tn):
    M, K = x.shape
    H = w_stack.shape[2] // 2
    tm = min(tm, M)
    nb = H // tn
    row = pl.BlockSpec((1, K), lambda m, n: (0, 0))
    w = w_stack
    return pl.pallas_call(
        _norm_swiglu_kernel,
        grid=(M // tm, nb),
        in_specs=[pl.BlockSpec((tm, K), lambda m, n: (m, 0)), row, row, row,
                  pl.BlockSpec((None, K, tn), lambda m, n: (layer, 0, n)),
                  pl.BlockSpec((None, K, tn), lambda m, n: (layer, 0, nb + n))],
        out_specs=pl.BlockSpec((tm, tn), lambda m, n: (m, n)),
        out_shape=jax.ShapeDtypeStruct((M, H), BF16),
        scratch_shapes=[pltpu.VMEM((tm, K), BF16)],
        compiler_params=_cp("parallel", "arbitrary"),
        name="ffn_in_swiglu",
    )(x, gain, scale, shift, w, w)


def _resid_kernel(a_ref, b_ref, x_ref, g_ref, o_ref):
    o_ref[...] = x_ref[...] + g_ref[...] * _dot(a_ref[...], b_ref[...])


def _mm_resid(a, b, x, gate, tm, tn, name):
    M, K = a.shape
    N = b.shape[1]
    tm = min(tm, M)
    return pl.pallas_call(
        _resid_kernel,
        grid=(N // tn, M // tm),
        in_specs=[pl.BlockSpec((tm, K), lambda n, m: (m, 0)),
                  pl.BlockSpec((K, tn), lambda n, m: (0, n)),
                  pl.BlockSpec((tm, tn), lambda n, m: (m, n)),
                  pl.BlockSpec((1, tn), lambda n, m: (0, n))],
        out_specs=pl.BlockSpec((tm, tn), lambda n, m: (m, n)),
        out_shape=jax.ShapeDtypeStruct((M, N), F32),
        compiler_params=_cp("parallel", "parallel"),
        name=name,
    )(a, b, x, gate)


def _merge_kernel(h_ref, wg0, wg1, wg2, wg3, oa, ob, oc, od, wa, wb, wc, wd, o_ref):
    h = h_ref[...]
    acc = None
    for wg, o_br, w_br in ((wg0, oa, wa), (wg1, ob, wb), (wg2, oc, wc), (wg3, od, wd)):
        term = _sigmoid(_dot(h, wg[...])) * _dot(o_br[...], w_br[...])
        acc = term if acc is None else acc + term
    o_ref[...] = acc.astype(o_ref.dtype)


def _merge(h, w_gate, branches, w_branches, tm, tn):
    M, D = h.shape
    tm = min(tm, M)
    nb = D // tn
    in_specs = [pl.BlockSpec((tm, D), lambda n, m: (m, 0))]
    for i in range(N_BRANCH):
        in_specs.append(pl.BlockSpec((D, tn), lambda n, m, i=i: (0, i * nb + n)))
    for br in branches:
        in_specs.append(pl.BlockSpec((tm, br.shape[1]), lambda n, m: (m, 0)))
    for wb in w_branches:
        in_specs.append(pl.BlockSpec((wb.shape[0], tn), lambda n, m: (0, n)))
    return pl.pallas_call(
        _merge_kernel,
        grid=(nb, M // tm),
        in_specs=in_specs,
        out_specs=pl.BlockSpec((tm, tn), lambda n, m: (m, n)),
        out_shape=jax.ShapeDtypeStruct((M, D), BF16),
        compiler_params=_cp("parallel", "parallel"),
        name="gated_merge",
    )(h, w_gate, w_gate, w_gate, w_gate, *branches, *w_branches)


def _gla_kernel(q_ref, k_ref, v_ref, g_ref, al_ref, wa_ref, ba_ref, gn_ref, lt_ref, blk_ref, e_ref, sel_ref,
                o_ref, st_ref, qe_ref, ke_ref, b_ref, dec_ref, acc_ref):
    C = GLA_SUB
    R = q_ref.shape[0]

    @pl.when(pl.program_id(0) == 0)
    def _():
        st_ref[...] = jnp.zeros_like(st_ref)

    gk = _log_sigmoid(_dot_hi(al_ref[...], wa_ref[...]) + ba_ref[...]) * (1.0 / GLA_NORMALIZER)
    gk_parts = _split3(gk)
    b = _group_rows(lt_ref[...], gk_parts)
    tot = _group_rows(blk_ref[...], gk_parts)
    scale = GLA_DK ** -0.5
    qe_ref[...] = (q_ref[...] * scale * jnp.exp(b)).astype(BF16)
    ke_ref[...] = (k_ref[...] * jnp.exp(tot - b)).astype(BF16)
    b_ref[...] = b
    dec_ref[...] = jnp.exp(tot)

    e_mat = e_ref[...]
    sel = sel_ref[...]
    s_idx = lax.broadcasted_iota(jnp.int32, (C, GLA_HEADS * GLA_DK), 0)

    K = GLA_SUBS_PER_TRIP
    HS = range(GLA_HEADS)
    kh = [slice(h * GLA_DK, (h + 1) * GLA_DK) for h in HS]
    vh = [slice(h * GLA_DV, (h + 1) * GLA_DV) for h in HS]

    def body(trip, carry):
        r0 = [pl.multiple_of((trip * K + i) * C, C) for i in range(K)]
        rows = [pl.ds(r, C) for r in r0]
        vs = [v_ref[r, :] for r in rows]
        p_st = []
        for i in range(K):
            qs = q_ref[rows[i], :] * scale
            ks = k_ref[rows[i], :]
            bs = b_ref[rows[i], :]
            pairs = []
            for t in range(C):
                e = jnp.exp(jnp.where(s_idx <= t, bs[t:t + 1, :] - bs, -jnp.inf))
                pairs.append(qs[t:t + 1, :] * ks * e)
            p_st.append(jnp.concatenate(pairs, axis=0).astype(BF16))
        sc = [_dot(p, e_mat) for p in p_st]
        w = [(sc[i] * jnp.concatenate([vs[i]] * C, axis=0)).astype(BF16) for i in range(K)]
        intra = [_dot(sel, x) for x in w]
        qe = [qe_ref[r, :] for r in rows]
        ke = [ke_ref[r, :] for r in rows]
        kv = [[_dot_tn(vs[i].astype(BF16)[:, vh[h]], ke[i][:, kh[h]]) for h in HS] for i in range(K)]
        st = [st_ref[h] for h in HS]
        for i in range(K):
            dec = dec_ref[pl.ds(r0[i], 1), :]
            outs = []
            for h in HS:
                outs.append(_dot_nt(qe[i][:, kh[h]], st[h].astype(BF16)))
                st[h] = st[h] * dec[:, kh[h]] + kv[i][h]
            acc_ref[rows[i], :] = jnp.concatenate(outs, axis=1) + intra[i]
        for h in HS:
            st_ref[h] = st[h]
        return carry

    lax.fori_loop(0, R // (C * K), body, 0)

    o = acc_ref[...]
    g = g_ref[...]
    gn = gn_ref[...]
    parts = []
    for h in range(GLA_HEADS):
        oh = o[:, h * GLA_DV:(h + 1) * GLA_DV]
        parts.append(oh * lax.rsqrt(jnp.mean(oh * oh, axis=-1, keepdims=True) + NORM_EPS) * gn)
    o_ref[...] = (jnp.concatenate(parts, axis=1) * (g * _sigmoid(g))).astype(o_ref.dtype)


def _gla(p, w_a2, b_a2, g_norm):
    S = p.shape[0]
    R = min(S, 512)
    C = GLA_SUB
    HK, HV = GLA_HEADS * GLA_DK, GLA_HEADS * GLA_DV
    wa = jnp.zeros((128, HK), F32).at[:GLA_RANK].set(w_a2)
    ri = np.arange(SLAB)
    same = (ri[:, None] // C) == (ri[None, :] // C)
    ltri = jnp.asarray((same & (ri[None, :] <= ri[:, None])).astype(np.float32), BF16)
    blk = jnp.asarray(same.astype(np.float32), BF16)
    e_mat =jnp.asarray((np.arange(HK)[:, None] // GLA_DK == np.arange(HV)[None, :] // GLA_DV).astype(np.float32), BF16)
    sel = jnp.asarray((np.arange(C)[:, None] == np.arange(C * C)[None, :] // C).astype(np.float32), BF16)
    const = lambda shape: pl.BlockSpec(shape, lambda i: (0, 0))
    return pl.pallas_call(
        _gla_kernel,
        grid=(S // R,),
        in_specs=[pl.BlockSpec((R, HK), lambda i: (i, 0)),
                  pl.BlockSpec((R, HK), lambda i: (i, 1)),
                  pl.BlockSpec((R, HV), lambda i: (i, 1)),
                  pl.BlockSpec((R, HV), lambda i: (i, 2)),
                  pl.BlockSpec((R, 128), lambda i: (i, (2 * HK + 2 * HV) // 128)),
                  const((128, HK)), const((1, HK)), const((1, GLA_DV)),
                  const((SLAB, SLAB)), const((SLAB, SLAB)), const((HK, HV)), const((C, C * C))],
        out_specs=pl.BlockSpec((R, HV), lambda i: (i, 0)),
        out_shape=jax.ShapeDtypeStruct((S, HV), BF16),
        scratch_shapes=[pltpu.VMEM((GLA_HEADS, GLA_DV, GLA_DK), F32),
                        pltpu.VMEM((R, HK), BF16), pltpu.VMEM((R, HK), BF16),
                        pltpu.VMEM((R, HK), F32), pltpu.VMEM((R, HK), F32),
                        pltpu.VMEM((R, HV), F32)],
        compiler_params=_cp("arbitrary"),
        name="gla",
    )(p, p, p, p, p, wa, b_a2[None, :], g_norm[None, :], ltri, blk, e_mat, sel)


def _band_kernel(dil, q_ref, k_ref, v_ref, kp_ref, vp_ref, c_ref, s_ref, cp_ref, sp_ref, o_ref, l_ref):
    t = pl.program_id(1)
    blk = DIL_BLK
    span = blk * dil
    qi = lax.broadcasted_iota(jnp.int32, (blk, 2 * blk), 0)
    ki = lax.broadcasted_iota(jnp.int32, (blk, 2 * blk), 1)
    dist = blk + qi - ki
    band = jnp.where(dist >= 0, jnp.where(dist <= blk, 1, 0), 0)
    band_first = band * jnp.where(ki >= blk, 1, jnp.where(t > 0, 1, 0))

    def rows_of(start):
        return pl.ds(start, blk, stride=dil) if dil > 1 else pl.ds(start, blk)

    units = [(r, b) for r in range(dil) for b in range(DIL_TILE // span)]
    roped = {}
    for g0 in range(0, len(units), DIL_UNITS_PER_STAGE):
        grp = units[g0:g0 + DIL_UNITS_PER_STAGE]
        rows, q16 = {}, {}
        for r, b in grp:
            rows[r, b] = rows_of(b * span + r)
            cos_c, sin_c = c_ref[rows[r, b], :], s_ref[rows[r, b], :]
            q16[r, b] = (_rope(q_ref[rows[r, b], :], cos_c, sin_c) * (HEAD_DIM ** -0.5)).astype(BF16)
            roped[r, b] = (_rope(k_ref[rows[r, b], :], cos_c, sin_c).astype(BF16), v_ref[rows[r, b], :].astype(BF16))
            if b == 0:
                prow = rows_of(r)
                roped[r, -1] = (_rope(kp_ref[prow, :], cp_ref[prow, :], sp_ref[prow, :]).astype(BF16),
                                vp_ref[prow, :].astype(BF16))
        s = {u: _dot_nt(q16[u], jnp.concatenate([roped[u[0], u[1] - 1][0], roped[u][0]], axis=0)) for u in grp}
        s = {u: jnp.where((band_first if u[1] == 0 else band) > 0, s[u], -jnp.inf) for u in grp}
        m = {u: jnp.max(s[u], axis=-1, keepdims=True) for u in grp}
        pr = {u: jnp.exp(s[u] - m[u]) for u in grp}
        den = {u: jnp.sum(pr[u], axis=-1, keepdims=True) for u in grp}
        pv = {u: _dot(pr[u].astype(BF16), jnp.concatenate([roped[u[0], u[1] - 1][1], roped[u][1]], axis=0))
              for u in grp}
        for u in grp:
            o_ref[rows[u], :] = pv[u] / den[u]
            l_ref[rows[u], :] = jnp.broadcast_to(m[u] + jnp.log(den[u]), (blk, HEAD_DIM))


def _banded_group(p_dil, cos_f, sin_f, group, dil):
    S = p_dil.shape[0]
    hp = DIL_HEADS_PER_GROUP
    span = DIL_BLK * dil
    assert S % DIL_TILE == 0 and DIL_TILE % span == 0
    per_tile = DIL_TILE // span

    def col(part):
        return lambda h, t: (t, part * DIL_HEADS + group * hp + h)

    def col_prev(part):
        return lambda h, t: (jnp.maximum(t * per_tile - 1, 0), part * DIL_HEADS + group * hp + h)

    cur = (DIL_TILE, HEAD_DIM)
    prev = (span, HEAD_DIM)
    tab_c = lambda h, t: (t, 0)
    tab_p = lambda h, t: (jnp.maximum(t * per_tile - 1, 0), 0)
    out_map = lambda h, t: (t, h)
    o, l = pl.pallas_call(
        functools.partial(_band_kernel, dil),
        grid=(hp, S // DIL_TILE),
        in_specs=[pl.BlockSpec(cur, col(0)), pl.BlockSpec(cur, col(1)), pl.BlockSpec(cur, col(2)),
                  pl.BlockSpec(prev, col_prev(1)), pl.BlockSpec(prev, col_prev(2)),
                  pl.BlockSpec(cur, tab_c), pl.BlockSpec(cur, tab_c), pl.BlockSpec(prev, tab_p), pl.BlockSpec(prev, tab_p)],
        out_specs=[pl.BlockSpec(cur, out_map)] * 2,
        out_shape=[jax.ShapeDtypeStruct((S, DIL_OUT), F32)] * 2,
        compiler_params=_cp("parallel", "parallel"),
        name=f"dilated_band_d{dil}",
    )(p_dil, p_dil, p_dil, p_dil, p_dil, cos_f, sin_f, cos_f, sin_f)
    return o, l


def _dil_combine_kernel(o0, o1, o2, l0, l1, l2, out_ref):
    a, b, c = l0[...], l1[...], l2[...]
    m = jnp.maximum(jnp.maximum(a, b), c)
    ea, eb, ec = jnp.exp(a - m), jnp.exp(b - m), jnp.exp(c - m)
    out_ref[...] = ((ea * o0[...] + eb * o1[...] + ec * o2[...]) / (ea + eb + ec)).astype(out_ref.dtype)


def _dilated(p_dil, cos_f, sin_f):
    S = p_dil.shape[0]
    outs, lses = [], []
    for g, (window, dil) in enumerate(DIL_PAIRS):
        assert window // dil == DIL_BLK
        o, l = _banded_group(p_dil, cos_f, sin_f, g, dil)
        outs.append(o)
        lses.append(l)
    tm = min(S, 1024)
    spec = pl.BlockSpec((tm, DIL_OUT), lambda i: (i, 0))
    return pl.pallas_call(
        _dil_combine_kernel,
        grid=(S // tm,),
        in_specs=[spec] * 6,
        out_specs=spec,
        out_shape=jax.ShapeDtypeStruct((S, DIL_OUT), BF16),
        compiler_params=_cp("parallel"),
        name="dilated_combine",
    )(*outs, *lses)


def _rwkv_prep_kernel(has_vmix, *refs):
    if has_vmix:
        (p_ref, prev_ref, mu_ref, w0_ref, w2_ref, a0_ref, a2_ref, g2_ref, kk_ref, ka_ref, rk_ref, hd_ref,
         lt_ref, pick_ref, vf_ref, v0_ref, v1_ref, v2_ref,
         at_o, rt_o, bt_o, kt_o, v16_o, gc_o, bonus_o, g_o) = refs
        v_o = None
    else:
        (p_ref, prev_ref, mu_ref, w0_ref, w2_ref, a0_ref, a2_ref, g2_ref, kk_ref, ka_ref, rk_ref, hd_ref,
         lt_ref, pick_ref,
         at_o, rt_o, bt_o, kt_o, v16_o, gc_o, bonus_o, g_o, v_o) = refs
    W = RWKV_WIDTH
    p = p_ref[...]
    last = jnp.where(pl.program_id(0) == 0, 0.0, prev_ref[7:8, :])
    row = lax.broadcasted_iota(jnp.int32, p.shape, 0)
    p_prev = jnp.where(row == 0, last, pltpu.roll(p, 1, axis=0))
    xs = p + (p_prev - p) * mu_ref[...]
    r, k, v = xs[:, 0:W], xs[:, W:2 * W], xs[:, 2 * W:3 * W]
    w_low = xs[:, 3 * W:3 * W + LORA_PAD]
    a_low = xs[:, 3 * W + LORA_PAD:3 * W + 2 * LORA_PAD]
    g_low = xs[:, 3 * W + 2 * LORA_PAD:]
    lora = lambda t, w_ref: _dot(t.astype(BF16), w_ref[...].astype(BF16))
    log_w = _log_sigmoid(w0_ref[...] + lora(jnp.tanh(w_low), w2_ref)) - 0.5
    a = _sigmoid(a0_ref[...] + lora(a_low, a2_ref))
    g = lora(_sigmoid(g_low), g2_ref)
    if has_vmix:
        mix = _sigmoid(v0_ref[...] + lora(lora(v, v1_ref), v2_ref))
        v = v + (vf_ref[...] - v) * mix
    kk = k * kk_ref[...]
    kk = kk / jnp.maximum(jnp.sqrt(_group_sum_lanes(kk * kk, hd_ref[...])), 1e-12)
    k = k * (1.0 + (a - 1.0) * ka_ref[...])
    lw = -jnp.exp(log_w)
    lw_parts = _split3(lw)
    cum = _group_rows(lt_ref[...], lw_parts)
    e_in = jnp.exp(cum)
    e_inv = jnp.exp(-cum)
    at_o[...] = (-kk * jnp.exp(cum - lw)).astype(at_o.dtype)
    rt_o[...] = (r * e_in).astype(rt_o.dtype)
    bt_o[...] = (kk * a * e_inv).astype(bt_o.dtype)
    kt_o[...] = (k * e_inv).astype(kt_o.dtype)
    v16_o[...] = v.astype(v16_o.dtype)
    gc_o[...] = jnp.exp(sum(_dot(pick_ref[...], part) for part in lw_parts))
    bonus_o[...] = _group_sum_lanes(r * k * rk_ref[...], hd_ref[...]) * v
    g_o[...] = g
    if v_o is not None:
        v_o[...] = v


def _pad_rows(w, rows):
    return jnp.zeros((rows, w.shape[1]), w.dtype).at[:w.shape[0]].set(w)


def _pad_rwkv_cols(t):
    W = RWKV_WIDTH
    z = jnp.zeros(t.shape[:-1] + (LORA_PAD - RWKV_DECAY_LORA,), t.dtype)
    o1, o2 = 3 * W + RWKV_DECAY_LORA, 3 * W + RWKV_DECAY_LORA + RWKV_AAA_LORA
    return jnp.concatenate([t[..., :o1], z, t[..., o1:o2], z, t[..., o2:]], axis=-1)


def _head_indicator():
    i = np.arange(SLAB) // RWKV_HEAD_SIZE
    return jnp.asarray((i[:, None] == i[None, :]).astype(np.float32), BF16)


def _rwkv_prep(p, mu, w0, w2, a0, a2, g2, k_k, k_a, r_k, v_first, v_mix):
    S = p.shape[0]
    W = RWKV_WIDTH
    C = RWKV_CHUNK
    R = min(S, 512)
    has_vmix = v_mix is not None
    row = lambda n: pl.BlockSpec((1, n), lambda i: (0, 0))
    full = lambda a: pl.BlockSpec(a.shape, lambda i: (0, 0))
    tile = pl.BlockSpec((R, W), lambda i: (i, 0))
    w2p, a2p = _pad_rows(w2, LORA_PAD), _pad_rows(a2, LORA_PAD)
    hd = _head_indicator()
    ri = np.arange(SLAB)
    same = (ri[:, None] // C) == (ri[None, :] // C)
    ltri = jnp.asarray((same & (ri[None, :] <= ri[:, None])).astype(np.float32), BF16)
    pick = jnp.asarray((np.arange(R // C)[:, None] == np.arange(R)[None, :] // C).astype(np.float32), BF16)
    args = [p, p, _pad_rwkv_cols(mu)[None, :], w0[None, :], w2p, a0[None, :], a2p, g2, k_k[None, :], k_a[None, :],
            r_k[None, :], hd, ltri, pick]
    in_specs = [pl.BlockSpec((R, RWKV_IN_PAD), lambda i: (i, 0)),
                pl.BlockSpec((8, RWKV_IN_PAD), lambda i: (jnp.maximum(i * (R // 8) - 1, 0), 0)),
                row(RWKV_IN_PAD), row(W), full(w2p), row(W), full(a2p), full(g2), row(W), row(W), row(W),
                full(hd), full(ltri), full(pick)]
    if has_vmix:
        v0, v1, v2 = v_mix
        args += [v_first, v0[None, :], v1, v2]
        in_specs += [tile, row(W), full(v1), full(v2)]
    gc_tile = pl.BlockSpec((R // C, W), lambda i: (i, 0))
    out_specs = [tile] * 5 + [gc_tile, tile, tile]
    out_shape = ([jax.ShapeDtypeStruct((S, W), BF16)] * 5
                 + [jax.ShapeDtypeStruct((S // C, W), F32)] + [jax.ShapeDtypeStruct((S, W), F32)] * 2)
    if not has_vmix:
        out_specs.append(tile)
        out_shape.append(jax.ShapeDtypeStruct((S, W), F32))
    return pl.pallas_call(
        functools.partial(_rwkv_prep_kernel, has_vmix),
        grid=(S // R,),
        in_specs=in_specs,
        out_specs=out_specs,
        out_shape=out_shape,
        compiler_params=_cp("parallel"),
        name="rwkv_prep",
    )(*args)


def _rwkv_scan_kernel(at_ref, rt_ref, bt_ref, kt_ref, v_ref, gc_ref, mask_ref, eye_ref, y_ref, st_ref):
    C = RWKV_CHUNK
    N = RWKV_HEAD_SIZE
    n_chunks = at_ref.shape[0] // C

    @pl.when(pl.program_id(0) == 0)
    def _():
        st_ref[...] = jnp.zeros_like(st_ref)

    mask = mask_ref[...]
    eye = eye_ref[...]

    H = RWKV_HEADS
    K = RWKV_CHUNKS_PER_TRIP
    HS = range(H)
    zeros_cn = jnp.zeros((C, N), BF16)

    def body(trip, carry):
        units = []
        for kc in range(K):
            c = trip * K + kc
            rows = pl.ds(pl.multiple_of(c * C, C), C)
            tiles = [ref[rows, :] for ref in (at_ref, rt_ref, bt_ref, kt_ref, v_ref)]
            g_row = gc_ref[pl.ds(c, 1), :]
            for h in HS:
                s = slice(h * N, (h + 1) * N)
                units.append(tuple(t[:, s] for t in tiles) + (g_row[:, s],))
        U = range(len(units))
        a_t, r_t, b_t, k_t, vv, g_c = ([u[i] for u in units] for i in range(6))
        m_all = [_dot_nt(jnp.concatenate([a_t[u], r_t[u]], axis=0), jnp.concatenate([b_t[u], k_t[u]], axis=0)) * mask
                 for u in U]
        m16 = [m.astype(BF16) for m in m_all]
        m_ab16 = [m[:C, :C] for m in m16]
        t_inv = [eye + m[:C, :C] for m in m_all]
        x = [_dot(m, m) for m in m_ab16]
        for step in range(5):
            x16 = [v.astype(BF16) for v in x]
            t_inv = [t_inv[u] + _dot(x16[u], t_inv[u].astype(BF16)) for u in U]
            if step < 4:
                x = [_dot(v, v) for v in x16]
        t16 = [t.astype(BF16) for t in t_inv]
        mv = [_dot(m16[u], jnp.concatenate([zeros_cn, vv[u]], axis=0)) for u in U]
        p16 = [_dot(t16[u], a_t[u]).astype(BF16) for u in U]
        q16 = [_dot(t16[u], mv[u][:C].astype(BF16)).astype(BF16) for u in U]
        m_rb16 = [m[C:, :C] for m in m16]
        p2 = [r_t[u].astype(F32) + _dot(m_rb16[u], p16[u]) for u in U]
        q2 = [mv[u][C:] + _dot(m_rb16[u], q16[u]) for u in U]
        b_end = [(b_t[u] * g_c[u]).astype(BF16) for u in U]
        k_end = [(k_t[u] * g_c[u]).astype(BF16) for u in U]
        g_mat = [_dot_tn(b_end[u], p16[u]) + eye * g_c[u] for u in U]
        h_mat = [_dot_tn(jnp.concatenate([b_end[u], k_end[u]], axis=0), jnp.concatenate([q16[u], vv[u]], axis=0))
                 for u in U]
        pg16 = [jnp.concatenate([p2[u], g_mat[u]], axis=0).astype(BF16) for u in U]
        st = [st_ref[h] for h in HS]
        for kc in range(K):
            ys = []
            for h in HS:
                u = kc * H + h
                ys_st = _dot(pg16[u], st[h].astype(BF16))
                ys.append(ys_st[:C] + q2[u])
                st[h] = ys_st[C:] + h_mat[u]
            rows = pl.ds(pl.multiple_of((trip * K + kc) * C, C), C)
            y_ref[rows, :] = jnp.concatenate(ys, axis=1)
        for h in HS:
            st_ref[h] = st[h]
        return carry

    lax.fori_loop(0, n_chunks // K, body, 0)


def _rwkv_scan(at, rt, bt, kt, v16, gc):
    S, W = at.shape
    C = RWKV_CHUNK
    R = min(S, 512)
    ti = np.arange(C)
    strict = (ti[None, :] < ti[:, None]).astype(np.float32)
    incl = (ti[None, :] <= ti[:, None]).astype(np.float32)
    mask = jnp.asarray(np.block([[strict, strict], [incl, incl]]))
    eye = jnp.eye(C, dtype=F32)
    tile = pl.BlockSpec((R, W), lambda i: (i, 0))
    const = lambda a_: pl.BlockSpec(a_.shape, lambda i: (0, 0))
    return pl.pallas_call(
        _rwkv_scan_kernel,
        grid=(S // R,),
        in_specs=[tile] * 5 + [pl.BlockSpec((R // C, W), lambda i: (i, 0)), const(mask), const(eye)],
        out_specs=tile,
        out_shape=jax.ShapeDtypeStruct((S, W), F32),
        scratch_shapes=[pltpu.VMEM((RWKV_HEADS, RWKV_HEAD_SIZE, RWKV_HEAD_SIZE), F32)],
        compiler_params=_cp("arbitrary"),
        name="rwkv_scan",
    )(at, rt, bt, kt, v16, gc, mask, eye)


def _rwkv_post_kernel(y_ref, bonus_ref, g_ref, lw_ref, lb_ref, hd_ref, o_ref):
    hd = hd_ref[...]
    inv_n = 1.0 / RWKV_HEAD_SIZE
    y = y_ref[...]
    yc = y - _group_sum_lanes(y, hd) * inv_n
    var = _group_sum_lanes(yc * yc, hd) * inv_n
    yn = yc * lax.rsqrt(var + RWKV_LNX_EPS) * lw_ref[...] + lb_ref[...]
    o_ref[...] = ((yn + bonus_ref[...]) * g_ref[...]).astype(o_ref.dtype)


def _rwkv_post(y, bonus, g, lnx_w, lnx_b):
    S, W = y.shape
    R = min(S, 512)
    tile = pl.BlockSpec((R, W), lambda i: (i, 0))
    row = pl.BlockSpec((1, W), lambda i: (0, 0))
    hd = _head_indicator()
    return pl.pallas_call(
        _rwkv_post_kernel,
        grid=(S // R,),
        in_specs=[tile] * 3 + [row] * 2 + [pl.BlockSpec((SLAB, SLAB), lambda i: (0, 0))],
        out_specs=tile,
        out_shape=jax.ShapeDtypeStruct((S, W), BF16),
        compiler_params=_cp("parallel"),
        name="rwkv_post",
    )(y, bonus, g, lnx_w[None, :], lnx_b[None, :], hd)


def _rwkv(p, mu, w0, w2, a0, a2, g2, k_k, k_a, r_k, lnx_w, lnx_b, v_first, v_mix):
    outs = _rwkv_prep(p, mu, w0, w2, a0, a2, g2, k_k, k_a, r_k, v_first, v_mix)
    at, rt, bt, kt, v16, gc, bonus, g = outs[:8]
    y = _rwkv_scan(at, rt, bt, kt, v16, gc)
    out = _rwkv_post(y, bonus, g, lnx_w, lnx_b)
    return out, (outs[8] if v_mix is None else v_first)


def _moba_prep_kernel(k_ref, v_ref, c_ref, s_ref, kr_ref, vt_ref, km_ref, kn_ref):
    cos_f, sin_f = c_ref[...], s_ref[...]
    parts = []
    for h in range(MOBA_HEADS):
        parts.append(_rope(k_ref[:, h * HEAD_DIM:(h + 1) * HEAD_DIM], cos_f, sin_f))
    kr = jnp.concatenate(parts, axis=1)
    lane = lax.broadcasted_iota(jnp.int32, (k_ref.shape[0], HEAD_DIM), 1)
    tag = jnp.where(lane == pl.program_id(0), 1.0, 0.0).astype(BF16)
    kr_ref[...] = jnp.concatenate(
        [t for h in range(MOBA_HEADS) for t in (parts[h].astype(BF16), tag)], axis=1)
    vt = v_ref[...].T
    ones = jnp.where(lax.broadcasted_iota(jnp.int32, (MOBA_VROWS - HEAD_DIM, vt.shape[1]), 0) == 0, 1.0, 0.0)
    vt_ref[...] = jnp.concatenate(
        [t for h in range(MOBA_HEADS) for t in (vt[h * HEAD_DIM:(h + 1) * HEAD_DIM], ones)], axis=0).astype(BF16)
    km_ref[...] = jnp.broadcast_to(jnp.mean(kr, axis=0, keepdims=True), km_ref.shape)
    norms = []
    for h in range(MOBA_HEADS):
        k16 = parts[h].astype(BF16).astype(F32)
        n2 = jnp.max(jnp.sum(k16 * k16, axis=1, keepdims=True), axis=0, keepdims=True)
        norms.append(jnp.broadcast_to(jnp.sqrt(n2), (kn_ref.shape[0], HEAD_DIM)))
    kn_ref[...] = jnp.concatenate(norms, axis=1)


def _moba_kernel(q_ref, c_ref, s_ref, k_ref, vt_ref, km_ref, kn_ref, o_ref):
    blk = MOBA_BLOCK
    nb = km_ref.shape[0]
    T = MOBA_TILES_PER_STEP
    cb0 = pl.program_id(1) * T
    bi = lax.broadcasted_iota(jnp.int32, (nb, blk), 0)
    ki = lax.broadcasted_iota(jnp.int32, (blk, blk), 0)
    qi = lax.broadcasted_iota(jnp.int32, (blk, blk), 1)

    def update(m, acc, s, vt):
        m_new = jnp.maximum(m, jnp.max(s, axis=0, keepdims=True))
        return m_new, jnp.exp(m - m_new) * acc + _dot(vt, jnp.exp(s - m_new).astype(BF16))

    chains = []
    for h in range(MOBA_HEADS_PER_STEP):
        kdim = slice(h * 2 * HEAD_DIM, h * 2 * HEAD_DIM + HEAD_DIM)
        vrow = slice(h * MOBA_VROWS, (h + 1) * MOBA_VROWS)
        for t in range(T):
            cb = cb0 + t
            qrows = slice(t * blk, (t + 1) * blk)
            q = _rope(q_ref[qrows, h * HEAD_DIM:(h + 1) * HEAD_DIM], c_ref[qrows, :], s_ref[qrows, :])
            qt = (q * (HEAD_DIM ** -0.5)).T
            gate = jnp.where(bi < cb, _dot_hi(km_ref[:, h * HEAD_DIM:(h + 1) * HEAD_DIM], qt), -jnp.inf)
            sel = jnp.zeros((nb, blk), F32)
            for _ in range(MOBA_TOPK):
                m = jnp.max(gate, axis=0, keepdims=True)
                idx = jnp.min(jnp.where(gate == m, bi, nb), axis=0, keepdims=True)
                hit = bi == idx
                sel = jnp.where(hit, jnp.where(m > -jnp.inf, 1.0, sel), sel)
                gate = jnp.where(hit, -jnp.inf, gate)
            qt16 = qt.astype(BF16)
            own = pl.ds(pl.multiple_of(cb * blk, blk), blk)
            s = jnp.where(ki <= qi, _dot(k_ref[own, kdim], qt16), -jnp.inf)
            m_run = jnp.max(s, axis=0, keepdims=True)
            acc = _dot(vt_ref[cb, vrow, :], jnp.exp(s - m_run).astype(BF16))
            for u in range(t):
                j = cb0 + u
                picked = jnp.sum(jnp.where(bi == j, sel, 0.0), axis=0, keepdims=True)
                rows = pl.ds(pl.multiple_of(j * blk, blk), blk)
                s = jnp.where(picked > 0.5, _dot(k_ref[rows, kdim], qt16), -jnp.inf)
                m_run, acc = update(m_run, acc, s, vt_ref[j, vrow, :])
            in_loop = bi < cb0
            bias = (jnp.where(in_loop, 1.0 - sel, 1.0) * MOBA_MASKED).astype(BF16)
            qa16 = jnp.concatenate([qt16, bias, jnp.zeros((HEAD_DIM - nb, blk), BF16)], axis=0)
            q32 = qt16.astype(F32)
            q_norm = jnp.sqrt(jnp.sum(q32 * q32, axis=0, keepdims=True))
            ceil = jnp.where(jnp.where(in_loop, sel, 0.0) > 0.5, kn_ref[:, h * HEAD_DIM:h * HEAD_DIM + 1] * q_norm,
                             -jnp.inf)
            m_ref = jnp.maximum(m_run, jnp.max(ceil, axis=0, keepdims=True) * MOBA_CEIL_SLACK)
            chains.append(dict(kcol=slice(h * 2 * HEAD_DIM, (h + 1) * 2 * HEAD_DIM), vrow=vrow, qa16=qa16,
                               m=m_run, acc=acc, m_ref=m_ref))
    CH = range(len(chains))
    G = MOBA_GROUP
    n_trips = (cb0 + G - 1) // G

    def scores(j):
        rows = pl.ds(pl.multiple_of(j * blk, blk), blk)
        return [_dot(k_ref[rows, chains[c]["kcol"]], chains[c]["qa16"]) for c in CH]

    def fixed_reference_loop():
        m_ref = [c["m_ref"] for c in chains]

        def body(jg, acc):
            acc = list(acc)
            js = [jg * G + g for g in range(G)]
            s = scores(js[0])
            for g in range(G):
                s_next = scores(js[g + 1]) if g + 1 < G else None
                for c in CH:
                    acc[c] = acc[c] + _dot(vt_ref[js[g], chains[c]["vrow"], :], jnp.exp(s[c] - m_ref[c]).astype(BF16))
                s = s_next
            return tuple(acc)

        return lax.fori_loop(0, n_trips, body, tuple(jnp.exp(c["m"] - c["m_ref"]) * c["acc"] for c in chains))

    def running_max_loop():
        def body(jg, carry):
            m, acc = list(carry[0]), list(carry[1])
            js = [jg * G + g for g in range(G)]
            s = scores(js[0])
            for g in range(G):
                s_next = scores(js[g + 1]) if g + 1 < G else None
                for c in CH:
                    m[c], acc[c] = update(m[c], acc[c], s[c], vt_ref[js[g], chains[c]["vrow"], :])
                s = s_next
            return tuple(m), tuple(acc)

        return lax.fori_loop(0, n_trips, body, (tuple(c["m"] for c in chains), tuple(c["acc"] for c in chains)))[1]

    gap = functools.reduce(jnp.maximum, [jnp.max(c["m_ref"] - c["m"]) for c in chains])
    acc = lax.cond(gap <= MOBA_MAX_GAP, fixed_reference_loop, running_max_loop)
    out = [(a[:HEAD_DIM] / a[HEAD_DIM:HEAD_DIM + 1]).T for a in acc]
    o_ref[...] = jnp.concatenate(
        [jnp.concatenate(out[h * T:(h + 1) * T], axis=0) for h in range(MOBA_HEADS_PER_STEP)], axis=1
    ).astype(o_ref.dtype)


def _moba(p, cos_f, sin_f):
    S = p.shape[0]
    blk = MOBA_BLOCK
    nb = S // blk
    assert nb <= HEAD_DIM, "block ids are one-hot encoded in HEAD_DIM lanes"
    assert nb % MOBA_GROUP == 0, "the block loop over-runs to a multiple of MOBA_GROUP"
    HW = MOBA_HEADS * HEAD_DIM
    kr, vt, km, kn = pl.pallas_call(
        _moba_prep_kernel,
        grid=(nb,),
        in_specs=[pl.BlockSpec((blk, HW), lambda i: (i, 1)), pl.BlockSpec((blk, HW), lambda i: (i, 2)),
                  pl.BlockSpec((blk, HEAD_DIM), lambda i: (i, 0)), pl.BlockSpec((blk, HEAD_DIM), lambda i: (i, 0))],
        out_specs=[pl.BlockSpec((blk, 2 * HW), lambda i: (i, 0)),
                   pl.BlockSpec((None, MOBA_HEADS * MOBA_VROWS, blk), lambda i: (i, 0, 0)),
                   pl.BlockSpec((8, HW), lambda i: (i, 0)), pl.BlockSpec((8, HW), lambda i: (i, 0))],
        out_shape=[jax.ShapeDtypeStruct((S, 2 * HW), BF16),
                   jax.ShapeDtypeStruct((nb, MOBA_HEADS * MOBA_VROWS, blk), BF16),
                   jax.ShapeDtypeStruct((nb * 8, HW), F32), jax.ShapeDtypeStruct((nb * 8, HW), F32)],
        compiler_params=_cp("parallel"),
        name="moba_prep",
    )(p, p, cos_f, sin_f)
    k_mean = km.reshape(nb, 8, HW)[:, 0]
    k_norm = kn.reshape(nb, 8, HW)[:, 0]
    P = MOBA_HEADS_PER_STEP
    T = MOBA_TILES_PER_STEP
    assert nb % T == 0
    tab = pl.BlockSpec((T * blk, HEAD_DIM), lambda h, cb: (cb, 0))
    once = pl.Buffered(1)
    return pl.pallas_call(
        _moba_kernel,
        grid=(MOBA_HEADS // P, nb // T),
        in_specs=[pl.BlockSpec((T * blk, P * HEAD_DIM), lambda h, cb: (cb, h)), tab, tab,
                  pl.BlockSpec((S, P * 2 * HEAD_DIM), lambda h, cb: (0, h), pipeline_mode=once),
                  pl.BlockSpec((nb, P * MOBA_VROWS, blk), lambda h, cb: (0, h, 0), pipeline_mode=once),
                  pl.BlockSpec((nb, P * HEAD_DIM), lambda h, cb: (0, h)),
                  pl.BlockSpec((nb, P * HEAD_DIM), lambda h, cb: (0, h))],
        out_specs=pl.BlockSpec((T * blk, P * HEAD_DIM), lambda h, cb: (cb, h)),
        out_shape=jax.ShapeDtypeStruct((S, HW), BF16),
        compiler_params=_cp("parallel", "arbitrary"),
        name="moba",
    )(p, cos_f, sin_f, kr, vt, k_mean, k_norm)


def _split_w_in(w_in_l):
    o = np.cumsum((N_BRANCH * D_MODEL, GLA_IN, DIL_IN, RWKV_IN, MOBA_IN))
    w_gate, w_gla, w_dil, w_rwkv, w_moba = (w_in_l[:, a:b] for a, b in zip((0,) + tuple(o[:-1]), o))
    w_gla = jnp.concatenate([w_gla, jnp.zeros((D_MODEL, GLA_IN_PAD - GLA_IN), w_gla.dtype)], axis=1)
    w_rwkv = _pad_rwkv_cols(w_rwkv)
    return tuple(w.astype(BF16) for w in (w_gate, w_gla, w_dil, w_rwkv, w_moba))


def kernel(x, c, positions, w_ada, b_ada, norm1, w_in, gla_w_a2, gla_b_a2, gla_gnorm, rwkv_mu, rwkv_w0, rwkv_w2, rwkv_a0, rwkv_a2, rwkv_g2, rwkv_k_k, rwkv_k_a, rwkv_r_k, rwkv_lnx_w, rwkv_lnx_b, rwkv_v0, rwkv_v1, rwkv_v2, w_branch_a, w_branch_b, w_branch_c, w_branch_d, w_out, norm2, w_ffn_in, w_ffn_out, norm_f):
    B, S, D = x.shape
    assert B == 1 and D == D_MODEL
    xs = x.reshape(S, D)
    cos_f, sin_f = _rope_tables(positions)
    c8 = jnp.broadcast_to(c, (8, D))
    v_first = None
    for l in range(DEPTH):
        mod = _adaln(c8, w_ada, b_ada, l)[0:1]
        shift1, scale1, gate1, shift2, scale2, gate2 = jnp.split(mod, 6, axis=-1)

        h = _normmod(xs, norm1[l][None, :], scale1, shift1, BF16)
        w_gate, w_gla, w_dil, w_rwkv, w_moba = _split_w_in(w_in[l])
        p_gla = _mm(h, w_gla, 1024, GLA_IN_PAD, F32, "proj_gla")
        p_dil = _mm(h, w_dil, 1024, DIL_IN, F32, "proj_dil")
        p_rwkv = _mm(h, w_rwkv, 1024, RWKV_IN_PAD, F32, "proj_rwkv")
        p_moba = _mm(h, w_moba, 1024, MOBA_IN, F32, "proj_moba")

        o_gla = _gla(p_gla, gla_w_a2[l], gla_b_a2[l], gla_gnorm[l])
        o_dil = _dilated(p_dil, cos_f, sin_f)
        v_mix = None if l == 0 else (rwkv_v0[l - 1], rwkv_v1[l - 1], rwkv_v2[l - 1])
        o_rwkv, v_first = _rwkv(p_rwkv, rwkv_mu[l], rwkv_w0[l], rwkv_w2[l], rwkv_a0[l], rwkv_a2[l], rwkv_g2[l],
                                rwkv_k_k[l], rwkv_k_a[l], rwkv_r_k[l], rwkv_lnx_w[l], rwkv_lnx_b[l], v_first, v_mix)
        o_moba = _moba(p_moba, cos_f, sin_f)

        merged = _merge(h, w_gate, (o_gla, o_dil, o_rwkv, o_moba),
                        tuple(w[l].astype(BF16) for w in (w_branch_a, w_branch_b, w_branch_c, w_branch_d)), 1024, 512)
        xs = _mm_resid(merged, w_out[l].astype(BF16), xs, gate1, 1024, 1024, "out_proj_resid")

        act = _norm_mm_swiglu(xs, norm2[l][None, :], scale2, shift2, w_ffn_in, l, 1024, 512)
        xs = _mm_resid(act, w_ffn_out[l].astype(BF16), xs, gate2, 1024, 512, "ffn_out_resid")

    zero = jnp.zeros((1, D), F32)
    return _normmod(xs, norm_f[None, :], zero, zero, F32).reshape(B, S, D)
```
